```python
import math
import jax, jax.numpy as jnp
from jax import lax
import numpy as np

D_MODEL = 2048
BATCH = 2
SEQ = 4096
DEPTH = 2

RWKV_HEAD_DIM = 64
RWKV_WIDTH = D_MODEL // 2
RWKV_HEADS = RWKV_WIDTH // RWKV_HEAD_DIM
DECAY_LORA = 64
ICLR_LORA = 64
VRES_LORA = 32
GATE_LORA = 160
LNX_EPS = 64e-5

ATTN_GROUPS = ((128, 1), (512, 4), (2048, 16))
ATTN_HEADS_PER_GROUP = 4
ATTN_HEAD_DIM = D_MODEL // 16
ATTN_HEADS = len(ATTN_GROUPS) * ATTN_HEADS_PER_GROUP
ATTN_OUT_WIDTH = ATTN_HEADS_PER_GROUP * ATTN_HEAD_DIM

SGU_CHUNK = 128
SGU_GROUPS = 4
SGU_WIDTH = D_MODEL // 4
SGU_GROUP_DIM = SGU_WIDTH // SGU_GROUPS

FFN_DIM = ((8 * D_MODEL // 3 + 127) // 128) * 128
RMS_EPS = 1e-6
LN_EPS = 1e-5
N_BRANCHES = 3

N_SHIFT = 3 * RWKV_WIDTH + DECAY_LORA + ICLR_LORA + GATE_LORA
N_QKV = 3 * ATTN_HEADS * ATTN_HEAD_DIM
N_SGU = 2 * SGU_WIDTH
N_GATE = N_BRANCHES * D_MODEL
N_IN = N_SHIFT + N_QKV + N_SGU + N_GATE

kernel_name = "hybrid_rwkv7_dilated_alibi_sgu_macaron"


def rms_norm(x, g):
    xf = x.astype(jnp.float32)
    y = xf * lax.rsqrt(jnp.mean(xf * xf, axis=-1, keepdims=True) + RMS_EPS)
    return (y * g.astype(jnp.float32)).astype(x.dtype)


def token_shift(p):
    return jnp.pad(p, ((0, 0), (1, 0), (0, 0)))[:, :-1]


def swiglu_ffn(x, w_gu, w_down):
    gate, up = jnp.split(x @ w_gu, 2, axis=-1)
    return (jax.nn.silu(gate) * up) @ w_down


def alibi_slopes(n_heads):
    def geometric(n):
        start = 2.0 ** (-8.0 / n)
        return [start ** (i + 1) for i in range(n)]
    closest = 2 ** int(math.floor(math.log2(n_heads)))
    slopes = geometric(closest)
    if closest < n_heads:
        slopes += geometric(2 * closest)[0::2][: n_heads - closest]
    return np.array(sorted(slopes, reverse=True), dtype=np.float32)


def wkv7_scan(r, w, k, v, a, b):
    B, T, H, N = r.shape

    def step(S, inp):
        r_t, w_t, k_t, v_t, a_t, b_t = inp
        sa = jnp.einsum('bhij,bhj->bhi', S, a_t)
        S = (S * w_t[:, :, None, :] + sa[:, :, :, None] * b_t[:, :, None, :]
             + v_t[:, :, :, None] * k_t[:, :, None, :])
        return S, jnp.einsum('bhij,bhj->bhi', S, r_t)

    S0 = jnp.zeros((B, H, N, N), jnp.float32)
    xs = tuple(jnp.moveaxis(z, 1, 0) for z in (r, w, k, v, a, b))
    _, ys = lax.scan(step, S0, xs)
    return jnp.moveaxis(ys, 0, 1)


def rwkv7_time_mix(xr, xk, v, xw, xa, xg, decay_w0, decay_w2, iclr_a0, iclr_a2, gate_g2,
                   k_k, k_a, r_k, lnx_w, lnx_b):
    B, T, _ = xr.shape
    H, N = RWKV_HEADS, RWKV_HEAD_DIM
    f32 = jnp.float32
    heads = lambda z: z.astype(f32).reshape(B, T, H, N)
    w_log = -jax.nn.softplus(-(decay_w0 + jnp.tanh(xw) @ decay_w2).astype(f32)) - 0.5
    decay = jnp.exp(-jnp.exp(w_log))
    a = jax.nn.sigmoid((iclr_a0 + xa @ iclr_a2).astype(f32))
    g = jax.nn.sigmoid(xg) @ gate_g2
    k = xk.astype(f32)
    kk = heads(k * k_k)
    kk = kk / jnp.maximum(jnp.linalg.norm(kk, axis=-1, keepdims=True), 1e-12)
    k = k * (1.0 + (a - 1.0) * k_a)
    r_h, k_h, v_h, a_h = heads(xr), heads(k), heads(v), heads(a)
    y = wkv7_scan(r_h, heads(decay), k_h, v_h, -kk, kk * a_h)
    mu = jnp.mean(y, axis=-1, keepdims=True)
    var = jnp.mean(jnp.square(y - mu), axis=-1, keepdims=True)
    y = ((y - mu) * lax.rsqrt(var + LNX_EPS)).reshape(B, T, H * N) * lnx_w + lnx_b
    bonus = jnp.sum(r_h * k_h * r_k, axis=-1, keepdims=True) * v_h
    y = (y + bonus.reshape(B, T, H * N)) * g
    return y.astype(xr.dtype)


def dilated_attention_group(q, k, v, window, dilation, slopes):
    B, T, Hg, E = q.shape
    blk = window // dilation
    span = blk * dilation
    t_pad = -(-T // span) * span
    L = t_pad // dilation
    nb = L // blk

    def to_sub(z):
        z = jnp.pad(z, ((0, 0), (0, t_pad - T), (0, 0), (0, 0)))
        z = z.reshape(B, L, dilation, Hg, E).transpose(0, 2, 3, 1, 4)
        return z.reshape(B, dilation, Hg, nb, blk, E)

    def with_prev(z):
        prev = jnp.pad(z, ((0, 0), (0, 0), (0, 0), (1, 0), (0, 0), (0, 0)))[:, :, :, :-1]
        return jnp.concatenate([prev, z], axis=4)

    qs = to_sub(q)
    kc, vc = with_prev(to_sub(k)), with_prev(to_sub(v))
    s = jnp.einsum('brhnqe,brhnke->brhnqk', qs, kc).astype(jnp.float32) * (E ** -0.5)
    i = jnp.arange(blk)[:, None]
    j = jnp.arange(2 * blk)[None, :]
    rel = blk + i - j
    blk_idx = jnp.arange(nb)[:, None, None]
    mask = (rel >= 0) & (rel <= blk) & ((blk_idx > 0) | (j >= blk))
    dist = (rel * dilation).astype(jnp.float32)
    bias = -slopes[:, None, None] * dist
    s = jnp.where(mask, s + bias[None, None, :, None], -jnp.inf)
    lse = jax.nn.logsumexp(s, axis=-1)
    p = jnp.exp(s - lse[..., None])
    o = jnp.einsum('brhnqk,brhnke->brhnqe', p.astype(v.dtype), vc)
    o = o.reshape(B, dilation, Hg, L, E).transpose(0, 3, 1, 2, 4).reshape(B, t_pad, Hg, E)[:, :T]
    lse = lse.reshape(B, dilation, Hg, L).transpose(0, 3, 1, 2).reshape(B, t_pad, Hg)[:, :T]
    return o, lse


def spatial_gating(u, v, ln_g, ln_b, w_s, b_s):
    B, T, C = v.shape
    vf = v.astype(jnp.float32)
    mu = jnp.mean(vf, axis=-1, keepdims=True)
    var = jnp.mean(jnp.square(vf - mu), axis=-1, keepdims=True)
    vn = ((vf - mu) * lax.rsqrt(var + LN_EPS) * ln_g + ln_b).astype(v.dtype)
    vn = vn.reshape(B, T // SGU_CHUNK, SGU_CHUNK, SGU_GROUPS, SGU_GROUP_DIM)
    causal = jnp.tril(jnp.ones((SGU_CHUNK, SGU_CHUNK), dtype=bool))
    w = jnp.where(causal, w_s, jnp.zeros_like(w_s))
    mixed = jnp.einsum('gts,bcsgd->bctgd', w, vn) + b_s.T[:, :, None]
    return u * mixed.reshape(B, T, C)


def setup_inputs(seed: int = 0) -> dict:
    key = jax.random.key(seed)
    ks = iter(jax.random.split(key, 48))
    L, D, F, W = DEPTH, D_MODEL, FFN_DIM, RWKV_WIDTH

    def normal(shape, scale):
        return jax.random.normal(next(ks), shape, jnp.float32) * scale

    def uniform(shape, lo, hi):
        return jax.random.uniform(next(ks), shape, jnp.float32, lo, hi)

    def gain(shape):
        return 1.0 + normal(shape, 0.02)

    return {
        "x": normal((BATCH, SEQ, D), 1.0),
        "ffn1_pre_g": gain((L, D)),
        "ffn1_w_gu": normal((L, D, 2 * F), D ** -0.5),
        "ffn1_w_down": normal((L, F, D), F ** -0.5),
        "ffn1_post_g": gain((L, D)),
        "mix_pre_g": gain((L, D)),
        "w_in": normal((L, D, N_IN), D ** -0.5),
        "shift_mu": uniform((L, N_SHIFT), 0.0, 1.0),
        "decay_w0": uniform((L, W), -6.0, 1.0),
        "decay_w2": normal((L, DECAY_LORA, W), 0.5 * DECAY_LORA ** -0.5),
        "iclr_a0": normal((L, W), 0.5),
        "iclr_a2": normal((L, ICLR_LORA, W), ICLR_LORA ** -0.5),
        "gate_g2": normal((L, GATE_LORA, W), GATE_LORA ** -0.5),
        "k_k": uniform((L, W), 0.7, 1.0),
        "k_a": uniform((L, W), 0.8, 1.2),
        "r_k": normal((L, RWKV_HEADS, RWKV_HEAD_DIM), 0.1),
        "lnx_w": gain((L, W)),
        "lnx_b": normal((L, W), 0.02),
        "vres_w1": normal((L - 1, D, VRES_LORA), D ** -0.5),
        "vres_mu": uniform((L - 1, VRES_LORA), 0.0, 1.0),
        "vres_v0": normal((L - 1, W), 0.5),
        "vres_w2": normal((L - 1, VRES_LORA, W), VRES_LORA ** -0.5),
        "sgu_ln_g": gain((L, SGU_WIDTH)),
        "sgu_ln_b": normal((L, SGU_WIDTH), 0.02),
        "sgu_w_s": normal((L, SGU_GROUPS, SGU_CHUNK, SGU_CHUNK), 0.5 * SGU_CHUNK ** -0.5),
        "sgu_b": 1.0 + normal((L, SGU_GROUPS, SGU_CHUNK), 0.1),
        "w_b_rwkv": normal((L, W, D), W ** -0.5),
        "w_b_attn": normal((L, ATTN_OUT_WIDTH, D), ATTN_OUT_WIDTH ** -0.5),
        "w_b_sgu": normal((L, SGU_WIDTH, D), SGU_WIDTH ** -0.5),
        "w_out": normal((L, D, D), D ** -0.5),
        "mix_post_g": gain((L, D)),
        "ffn2_pre_g": gain((L, D)),
        "ffn2_w_gu": normal((L, D, 2 * F), D ** -0.5),
        "ffn2_w_down": normal((L, F, D), F ** -0.5),
        "ffn2_post_g": gain((L, D)),
    }


def reference(x, ffn1_pre_g, ffn1_w_gu, ffn1_w_down, ffn1_post_g, mix_pre_g, w_in, shift_mu,
              decay_w0, decay_w2, iclr_a0, iclr_a2, gate_g2, k_k, k_a, r_k, lnx_w, lnx_b,
              vres_w1, vres_mu, vres_v0, vres_w2, sgu_ln_g, sgu_ln_b, sgu_w_s, sgu_b,
              w_b_rwkv, w_b_attn, w_b_sgu, w_out, mix_post_g,
              ffn2_pre_g, ffn2_w_gu, ffn2_w_down, ffn2_post_g):
    B, T, D = x.shape
    W = RWKV_WIDTH
    slopes = jnp.asarray(alibi_slopes(ATTN_HEADS))
    shift_split = [W, 2 * W, 3 * W, 3 * W + DECAY_LORA, 3 * W + DECAY_LORA + ICLR_LORA]
    in_split = [N_SHIFT, N_SHIFT + N_QKV, N_SHIFT + N_QKV + N_SGU]
    h = x
    v_first = None
    for l in range(DEPTH):
        h = h + 0.5 * rms_norm(swiglu_ffn(rms_norm(h, ffn1_pre_g[l]), ffn1_w_gu[l], ffn1_w_down[l]),
                               ffn1_post_g[l])

        xn = rms_norm(h, mix_pre_g[l])
        proj = xn @ w_in[l]
        p_shift, p_qkv, p_sgu, p_gate = jnp.split(proj, in_split, axis=-1)

        p_shift = p_shift + (token_shift(p_shift) - p_shift) * shift_mu[l]
        xr, xk, xv, xw, xa, xg = jnp.split(p_shift, shift_split, axis=-1)
        if l == 0:
            v = xv
            v_first = xv
        else:
            pv = xn @ vres_w1[l - 1]
            pv = pv + (token_shift(pv) - pv) * vres_mu[l - 1]
            v = xv + (v_first - xv) * jax.nn.sigmoid(vres_v0[l - 1] + pv @ vres_w2[l - 1])
        y_rwkv = rwkv7_time_mix(xr, xk, v, xw, xa, xg, decay_w0[l], decay_w2[l], iclr_a0[l],
                                iclr_a2[l], gate_g2[l], k_k[l], k_a[l], r_k[l], lnx_w[l], lnx_b[l])

        q, k, va = (z.reshape(B, T, ATTN_HEADS, ATTN_HEAD_DIM) for z in jnp.split(p_qkv, 3, axis=-1))
        outs, lses = [], []
        for gi, (window, dilation) in enumerate(ATTN_GROUPS):
            hs = slice(gi * ATTN_HEADS_PER_GROUP, (gi + 1) * ATTN_HEADS_PER_GROUP)
            o_g, lse_g = dilated_attention_group(q[:, :, hs], k[:, :, hs], va[:, :, hs],
                                                 window, dilation, slopes[hs])
            outs.append(o_g)
            lses.append(lse_g)
        wts = jax.nn.softmax(jnp.stack(lses, axis=0), axis=0)
        y_attn = jnp.sum(wts[..., None] * jnp.stack(outs, axis=0).astype(jnp.float32), axis=0)
        y_attn = y_attn.reshape(B, T, ATTN_OUT_WIDTH).astype(x.dtype)

        u_s, v_s = jnp.split(jax.nn.gelu(p_sgu), 2, axis=-1)
        y_sgu = spatial_gating(u_s, v_s, sgu_ln_g[l], sgu_ln_b[l], sgu_w_s[l], sgu_b[l])

        gates = jax.nn.sigmoid(p_gate).reshape(B, T, N_BRANCHES, D)
        merged = (gates[:, :, 0] * (y_rwkv @ w_b_rwkv[l])
                  + gates[:, :, 1] * (y_attn @ w_b_attn[l])
                  + gates[:, :, 2] * (y_sgu @ w_b_sgu[l]))
        h = h + rms_norm(merged @ w_out[l], mix_post_g[l])

        h = h + 0.5 * rms_norm(swiglu_ffn(rms_norm(h, ffn2_pre_g[l]), ffn2_w_gu[l], ffn2_w_down[l]),
                               ffn2_post_g[l])
    return h
```

```python
import functools
import math

import numpy as np
import jax
import jax.numpy as jnp
from jax import lax
from jax.experimental import pallas as pl
from jax.experimental.pallas import tpu as pltpu

BF = jnp.bfloat16
F32 = jnp.float32

D_MODEL = 2048
RWKV_HEAD_DIM = 64
RWKV_WIDTH = D_MODEL // 2
DECAY_LORA = 64
ICLR_LORA = 64
VRES_LORA = 32
GATE_LORA = 160
LNX_EPS = 64e-5
ATTN_GROUPS = ((128, 1), (512, 4), (2048, 16))
ATTN_HEADS_PER_GROUP = 4
ATTN_HEAD_DIM = 128
ATTN_HEADS = len(ATTN_GROUPS) * ATTN_HEADS_PER_GROUP
ATTN_BLK = 128
SGU_CHUNK = 128
SGU_GROUPS = 4
SGU_WIDTH = D_MODEL // 4
RMS_EPS = 1e-6
LN_EPS = 1e-5
N_BRANCHES = 3

LANES = 128
WKV_CHUNK = 64
NEG_BIG = -1e30

SEG_R, SEG_K, SEG_V = 0, RWKV_WIDTH, 2 * RWKV_WIDTH
SEG_W = 3 * RWKV_WIDTH
SEG_A = SEG_W + 128
SEG_G = SEG_A + 128
SEG_P = SEG_G + 256
N_SHIFT_PAD = SEG_P + 256
N_QKV = 3 * ATTN_HEADS * ATTN_HEAD_DIM
N_SGU = 2 * SGU_WIDTH
N_GATE = N_BRANCHES * D_MODEL

VMEM_LIMIT = 56 * 1024 * 1024


def _cparams(*sem):
    return pltpu.CompilerParams(dimension_semantics=sem, vmem_limit_bytes=VMEM_LIMIT)


def _dot(a, b):
    return jnp.dot(a, b, preferred_element_type=F32)


def _dot_nt(a, b):
    return lax.dot_general(a, b, (((1,), (1,)), ((), ())), preferred_element_type=F32)


def _rms(x, g):
    return x * lax.rsqrt(jnp.mean(x * x, axis=-1, keepdims=True) + RMS_EPS) * g


def _ffn_kernel(h_ref, pre_g_ref, wg_ref, wu_ref, wd_ref, post_g_ref, o_ref, xn_scr, acc_scr):
    f = pl.program_id(1)

    @pl.when(f == 0)
    def _():
        xn_scr[...] = _rms(h_ref[...], pre_g_ref[...]).astype(BF)
        acc_scr[...] = jnp.zeros_like(acc_scr)

    xn = xn_scr[...]
    g = _dot(xn, wg_ref[...])
    u = _dot(xn, wu_ref[...])
    a = (g * jax.nn.sigmoid(g) * u).astype(BF)
    acc_scr[...] += _dot(a, wd_ref[...])

    @pl.when(f == pl.num_programs(1) - 1)
    def _():
        o_ref[...] = h_ref[...] + 0.5 * _rms(acc_scr[...], post_g_ref[...])


def _ffn(h, pre_g, wg, wu, wd, post_g, *, tm=512, tf=512):
    M, D = h.shape
    Fp = wg.shape[1]
    return pl.pallas_call(
        _ffn_kernel,
        grid=(M // tm, Fp // tf),
        in_specs=[
            pl.BlockSpec((tm, D), lambda i, f: (i, 0)),
            pl.BlockSpec((1, D), lambda i, f: (0, 0)),
            pl.BlockSpec((D, tf), lambda i, f: (0, f)),
            pl.BlockSpec((D, tf), lambda i, f: (0, f)),
            pl.BlockSpec((tf, D), lambda i, f: (f, 0)),
            pl.BlockSpec((1, D), lambda i, f: (0, 0)),
        ],
        out_specs=pl.BlockSpec((tm, D), lambda i, f: (i, 0)),
        out_shape=jax.ShapeDtypeStruct((M, D), F32),
        scratch_shapes=[pltpu.VMEM((tm, D), BF), pltpu.VMEM((tm, D), F32)],
        compiler_params=_cparams("parallel", "arbitrary"),
        name="ffn",
    )(h, pre_g, wg, wu, wd, post_g)


def _gelu_tanh(x):
    c = math.sqrt(2.0 / math.pi)
    return x * (0.5 * (1.0 + jnp.tanh(c * (x + 0.044715 * (x * x * x)))))


def _nmm_kernel(h_ref, g_ref, w_ref, o_ref, xn_scr, *, act):
    @pl.when(pl.program_id(1) == 0)
    def _():
        xn_scr[...] = _rms(h_ref[...], g_ref[...]).astype(BF)

    y = _dot(xn_scr[...], w_ref[...])
    if act == "gelu":
        y = _gelu_tanh(y)
    elif act == "sigmoid":
        y = jax.nn.sigmoid(y)
    o_ref[...] = y.astype(o_ref.dtype)


def _norm_matmul(h, g, w, *, act=None, out_dtype=F32, tm=1024, tn=768, name="in_proj"):
    M, D = h.shape
    N = w.shape[1]
    return pl.pallas_call(
        functools.partial(_nmm_kernel, act=act),
        grid=(M // tm, N // tn),
        in_specs=[
            pl.BlockSpec((tm, D), lambda i, j: (i, 0)),
            pl.BlockSpec((1, D), lambda i, j: (0, 0)),
            pl.BlockSpec((D, tn), lambda i, j: (0, j)),
        ],
        out_specs=pl.BlockSpec((tm, tn), lambda i, j: (i, j)),
        out_shape=jax.ShapeDtypeStruct((M, N), out_dtype),
        scratch_shapes=[pltpu.VMEM((tm, D), BF)],
        compiler_params=_cparams("parallel", "arbitrary"),
        name=name,
    )(h, g, w)


def _prep_kernel(*refs, seq, has_vres):
    if has_vres:
        (ps_ref, prev_ref, mu_ref, w0_ref, w2_ref, a0_ref, a2_ref, g2_ref, kk_ref, ka_ref,
         vf_ref, v0_ref, vw2_ref,
         r_o, k_o, v_o, lw_o, kk_o, a_o, g_o) = refs
    else:
        (ps_ref, prev_ref, mu_ref, w0_ref, w2_ref, a0_ref, a2_ref, g2_ref, kk_ref, ka_ref,
         r_o, k_o, v_o, lw_o, kk_o, a_o, g_o) = refs
    tm = ps_ref.shape[0]
    W = RWKV_WIDTH
    x = ps_ref[...]
    at_seq_start = (pl.program_id(0) * tm) % seq == 0
    last_prev = jnp.where(at_seq_start, 0.0, prev_ref[7:8, :])
    row = lax.broadcasted_iota(jnp.int32, x.shape, 0)
    prev = jnp.where(row == 0, last_prev, pltpu.roll(x, 1, axis=0))
    xs = x + (prev - x) * mu_ref[...]

    xr = xs[:, SEG_R:SEG_R + W]
    xk = xs[:, SEG_K:SEG_K + W]
    xv = xs[:, SEG_V:SEG_V + W]
    xw = xs[:, SEG_W:SEG_W + 128]
    xa = xs[:, SEG_A:SEG_A + 128]
    xg = xs[:, SEG_G:SEG_G + 256]

    z = w0_ref[...] + _dot(jnp.tanh(xw).astype(BF), w2_ref[...])
    w_log = -jax.nn.softplus(-z) - 0.5
    lw_o[...] = -jnp.exp(w_log)
    a = jax.nn.sigmoid(a0_ref[...] + _dot(xa.astype(BF), a2_ref[...]))
    a_o[...] = a
    g_o[...] = _dot(jax.nn.sigmoid(xg).astype(BF), g2_ref[...])
    kk_o[...] = xk * kk_ref[...]
    k_o[...] = xk * (1.0 + (a - 1.0) * ka_ref[...])
    r_o[...] = xr
    if has_vres:
        pv = xs[:, SEG_P:SEG_P + 128]
        mix = jax.nn.sigmoid(v0_ref[...] + _dot(pv.astype(BF), vw2_ref[...]))
        v_o[...] = xv + (vf_ref[...] - xv) * mix
    else:
        v_o[...] = xv


def _rwkv_prep(ps, mu, w0, w2, a0, a2, g2, k_k, k_a, vres, *, seq, tm=256):
    M, NA = ps.shape
    W = RWKV_WIDTH
    row = lambda n: pl.BlockSpec((1, n), lambda i: (0, 0))
    full = lambda a: pl.BlockSpec(a.shape, lambda i: (0, 0))
    tok = pl.BlockSpec((tm, W), lambda i: (i, 0))
    in_specs = [
        pl.BlockSpec((tm, NA), lambda i: (i, 0)),
        pl.BlockSpec((8, NA), lambda i: (jnp.maximum(i * (tm // 8) - 1, 0), 0)),
        row(NA), row(W), full(w2), row(W), full(a2), full(g2), row(W), row(W),
    ]
    args = [ps, ps, mu, w0, w2, a0, a2, g2, k_k, k_a]
    if vres is not None:
        v_first, v0, vw2 = vres
        in_specs += [tok, row(W), full(vw2)]
        args += [v_first, v0, vw2]
    return pl.pallas_call(
        functools.partial(_prep_kernel, seq=seq, has_vres=vres is not None),
        grid=(M // tm,),
        in_specs=in_specs,
        out_specs=[tok] * 7,
        out_shape=[jax.ShapeDtypeStruct((M, W), F32)] * 7,
        compiler_params=_cparams("parallel"),
        name="rwkv_prep",
    )(*args)


def _wkv_kernel(r_ref, k_ref, v_ref, lw_ref, kk_ref, a_ref, g_ref, rk_ref, lnw_ref, lnb_ref,
                y_ref, p_scr, *, chunks_per_step):
    C = WKV_CHUNK
    C2 = 2 * C

    @pl.when(pl.program_id(2) == 0)
    def _():
        p_scr[...] = jnp.zeros_like(p_scr)

    lane = lax.broadcasted_iota(jnp.int32, (C, LANES), 1)
    lo = lane < RWKV_HEAD_DIM
    ti = lax.broadcasted_iota(jnp.int32, (C, C), 0)
    tj = lax.broadcasted_iota(jnp.int32, (C, C), 1)
    tri = jnp.where(ti >= tj, 1.0, 0.0).astype(BF)
    si = lax.broadcasted_iota(jnp.int32, (C2, C2), 0)
    sj = lax.broadcasted_iota(jnp.int32, (C2, C2), 1)
    same = (si // C) == (sj // C)
    strict = same & ((si % C) > (sj % C))
    incl = same & ((si % C) >= (sj % C))
    eye = si == sj
    eye_f = jnp.where(eye, 1.0, 0.0)

    def seg_sum(x):
        s_lo = jnp.sum(jnp.where(lo, x, 0.0), axis=1, keepdims=True)
        s_hi = jnp.sum(jnp.where(lo, 0.0, x), axis=1, keepdims=True)
        return jnp.where(lo, s_lo, s_hi)

    def stack(x):
        return jnp.concatenate([jnp.where(lo, x, 0.0), jnp.where(lo, 0.0, x)], axis=0)

    rk = rk_ref[...]
    lnw = lnw_ref[...]
    lnb = lnb_ref[...]
    P = p_scr[...]
    for c in range(chunks_per_step):
        sl = pl.ds(c * C, C)
        r = r_ref[sl, :]
        k = k_ref[sl, :]
        v = v_ref[sl, :]
        lw = lw_ref[sl, :]
        kkr = kk_ref[sl, :]
        a = a_ref[sl, :]
        g = g_ref[sl, :]

        kk = kkr / jnp.maximum(jnp.sqrt(seg_sum(kkr * kkr)), 1e-12)
        lw_hi = lw.astype(BF)
        lw_lo = (lw - lw_hi.astype(F32)).astype(BF)
        cum = _dot(tri, lw_hi) + _dot(tri, lw_lo)
        tot = cum[C - 1:C, :]
        e_prev = jnp.exp(cum - lw)
        e_pos = jnp.exp(cum)
        e_neg = jnp.exp(-cum)
        e_rest = jnp.exp(tot - cum)
        kka = kk * a
        A_st = stack(-kk * e_prev).astype(BF)
        B_st = stack(kka * e_neg).astype(BF)
        K_st = stack(k * e_neg).astype(BF)
        R_st = stack(r * e_pos).astype(BF)
        V_st = stack(v).astype(BF)
        BpT = stack(kka * e_rest).T.astype(BF)
        KpT = stack(k * e_rest).T.astype(BF)

        Aab = jnp.where(strict, _dot_nt(A_st, B_st), 0.0)
        Aak = jnp.where(strict, _dot_nt(A_st, K_st), 0.0)
        Mrb = jnp.where(incl, _dot_nt(R_st, B_st), 0.0)
        Mrk = jnp.where(incl, _dot_nt(R_st, K_st), 0.0)

        X = Aab
        T = eye_f + X
        n = 2
        while n < C:
            Xb = X.astype(BF)
            X = _dot(Xb, Xb)
            T = T + _dot(T.astype(BF), X.astype(BF))
            n *= 2
        Tb = T.astype(BF)
        Wm = _dot(Tb, A_st)
        U0 = _dot(Tb, _dot(Aak.astype(BF), V_st).astype(BF))
        Pb = P.astype(BF)
        U = _dot(Wm.astype(BF), Pb) + U0
        Ub = U.astype(BF)
        Y_st = _dot(R_st, Pb) + _dot(Mrb.astype(BF), Ub) + _dot(Mrk.astype(BF), V_st)
        y = Y_st[:C, :] + Y_st[C:, :]

        gcol = jnp.exp(jnp.sum(jnp.where(eye, jnp.broadcast_to(tot, (C2, LANES)), 0.0),
                               axis=1, keepdims=True))
        P = gcol * P + _dot(BpT, Ub) + _dot(KpT, V_st)

        mu = seg_sum(y) * (1.0 / RWKV_HEAD_DIM)
        yc = y - mu
        var = seg_sum(yc * yc) * (1.0 / RWKV_HEAD_DIM)
        yn = yc * lax.rsqrt(var + LNX_EPS) * lnw + lnb
        bonus = seg_sum(r * k * rk) * v
        y_ref[sl, :] = ((yn + bonus) * g).astype(y_ref.dtype)
    p_scr[...] = P


def _wkv(r, k, v, lw, kkr, a, g, r_k, lnx_w, lnx_b, *, batch, seq, chunks_per_step=4):
    M, W = r.shape
    rows = chunks_per_step * WKV_CHUNK
    steps = seq // rows
    tok = pl.BlockSpec((rows, LANES), lambda b, p, c: (b * steps + c, p))
    par = pl.BlockSpec((1, LANES), lambda b, p, c: (0, p))
    return pl.pallas_call(
        functools.partial(_wkv_kernel, chunks_per_step=chunks_per_step),
        grid=(batch, W // LANES, steps),
        in_specs=[tok] * 7 + [par] * 3,
        out_specs=tok,
        out_shape=jax.ShapeDtypeStruct((M, W), BF),
        scratch_shapes=[pltpu.VMEM((LANES, LANES), F32)],
        compiler_params=_cparams("parallel", "parallel", "arbitrary"),
        name="wkv",
    )(r, k, v, lw, kkr, a, g, r_k, lnx_w, lnx_b)


def _attn_kernel(slopes_ref, q_ref, k_ref, v_ref, o_ref, m_scr, l_scr, acc_scr, *, seq):
    slot = pl.program_id(1)
    grp = pl.program_id(2)
    blk = ATTN_BLK
    scale = ATTN_HEAD_DIM ** -0.5
    qi = lax.broadcasted_iota(jnp.int32, (blk, blk), 0)
    kj = lax.broadcasted_iota(jnp.int32, (blk, blk), 1)
    rel_cur = (qi - kj).astype(F32)
    rel_prev = rel_cur + float(blk)
    ok_cur = kj <= qi
    ok_prev = kj >= qi

    def rows(start, d):
        if d == 1:
            return pl.ds(start, blk)
        return pl.ds(start, blk, stride=d)

    for gi, (window, d) in enumerate(ATTN_GROUPS):
        assert window // d == blk and seq % window == 0

        @pl.when(grp == gi)
        def _(gi=gi, d=d):
            slope = slopes_ref[gi * ATTN_HEADS_PER_GROUP + slot]
            bias_cur = jnp.where(ok_cur, -slope * float(d) * rel_cur, NEG_BIG)
            bias_prev = jnp.where(ok_prev, -slope * float(d) * rel_prev, NEG_BIG)
            span = blk * d

            for res in range(d):
                def body(n, carry, res=res):
                    cur = rows(n * span + res, d)
                    prv = rows(jnp.maximum(n - 1, 0) * span + res, d)
                    q = q_ref[cur, :].astype(BF)
                    s_c = _dot_nt(q, k_ref[cur, :].astype(BF)) * scale + bias_cur
                    s_p = _dot_nt(q, k_ref[prv, :].astype(BF)) * scale + bias_prev
                    s_p = jnp.where(n > 0, s_p, NEG_BIG)
                    m_loc = jnp.maximum(jnp.max(s_c, axis=1, keepdims=True),
                                        jnp.max(s_p, axis=1, keepdims=True))
                    p_c = jnp.exp(s_c - m_loc)
                    p_p = jnp.exp(s_p - m_loc)
                    l_loc = jnp.sum(p_c, axis=1, keepdims=True) + jnp.sum(p_p, axis=1, keepdims=True)
                    o_loc = (_dot(p_c.astype(BF), v_ref[cur, :].astype(BF))
                             + _dot(p_p.astype(BF), v_ref[prv, :].astype(BF)))
                    if gi == 0:
                        m_new, l_new, acc = m_loc, l_loc, o_loc
                    else:
                        m_old = m_scr[cur, :][:, :1]
                        l_old = l_scr[cur, :][:, :1]
                        m_new = jnp.maximum(m_old, m_loc)
                        alpha = jnp.exp(m_old - m_new)
                        beta = jnp.exp(m_loc - m_new)
                        l_new = alpha * l_old + beta * l_loc
                        acc = alpha * acc_scr[cur, :] + beta * o_loc
                    m_scr[cur, :] = jnp.broadcast_to(m_new, (blk, LANES))
                    l_scr[cur, :] = jnp.broadcast_to(l_new, (blk, LANES))
                    acc_scr[cur, :] = acc
                    return carry

                lax.fori_loop(0, seq // span, body, 0)

    @pl.when(grp == len(ATTN_GROUPS) - 1)
    def _():
        o_ref[...] = (acc_scr[...] / l_scr[...]).astype(o_ref.dtype)


def _attention(qkv, slopes, *, batch, seq):
    M = qkv.shape[0]
    E = ATTN_HEAD_DIM
    hpg = ATTN_HEADS_PER_GROUP

    def spec(which):
        return pl.BlockSpec((seq, E), lambda b, s, g: (b, which * ATTN_HEADS + g * hpg + s))

    return pl.pallas_call(
        functools.partial(_attn_kernel, seq=seq),
        grid=(batch, hpg, len(ATTN_GROUPS)),
        in_specs=[pl.BlockSpec(memory_space=pltpu.SMEM), spec(0), spec(1), spec(2)],
        out_specs=pl.BlockSpec((seq, E), lambda b, s, g: (b, s)),
        out_shape=jax.ShapeDtypeStruct((M, hpg * E), BF),
        scratch_shapes=[pltpu.VMEM((seq, LANES), F32)] * 3,
        compiler_params=_cparams("parallel", "parallel", "arbitrary"),
        name="dilated_attn",
    )(slopes, qkv, qkv, qkv)


def _sgu_kernel(x_ref, lg_ref, lb_ref, ws_ref, bt_ref, o_ref):
    tm = x_ref.shape[0]
    CH = SGU_CHUNK
    GD = SGU_WIDTH // SGU_GROUPS
    u = x_ref[:, :SGU_WIDTH]
    v = x_ref[:, SGU_WIDTH:]
    mu = jnp.mean(v, axis=-1, keepdims=True)
    vc = v - mu
    var = jnp.mean(vc * vc, axis=-1, keepdims=True)
    vn = (vc * lax.rsqrt(var + LN_EPS) * lg_ref[...] + lb_ref[...]).astype(BF)
    ti = lax.broadcasted_iota(jnp.int32, (CH, CH), 0)
    si = lax.broadcasted_iota(jnp.int32, (CH, CH), 1)
    causal = ti >= si
    for gidx in range(SGU_GROUPS):
        w = jnp.where(causal, ws_ref[gidx], 0.0).astype(BF)
        bias = bt_ref[:, gidx:gidx + 1]
        cs = slice(gidx * GD, (gidx + 1) * GD)
        for c in range(tm // CH):
            rs = slice(c * CH, (c + 1) * CH)
            mixed = _dot(w, vn[rs, cs]) + bias
            o_ref[rs, cs] = (u[rs, cs] * mixed).astype(o_ref.dtype)


def _sgu(gs, ln_g, ln_b, w_s, b_t, *, tm=512):
    M = gs.shape[0]
    return pl.pallas_call(
        _sgu_kernel,
        grid=(M // tm,),
        in_specs=[
            pl.BlockSpec((tm, N_SGU), lambda i: (i, 0)),
            pl.BlockSpec((1, SGU_WIDTH), lambda i: (0, 0)),
            pl.BlockSpec((1, SGU_WIDTH), lambda i: (0, 0)),
            pl.BlockSpec(w_s.shape, lambda i: (0, 0, 0)),
            pl.BlockSpec(b_t.shape, lambda i: (0, 0)),
        ],
        out_specs=pl.BlockSpec((tm, SGU_WIDTH), lambda i: (i, 0)),
        out_shape=jax.ShapeDtypeStruct((M, SGU_WIDTH), BF),
        compiler_params=_cparams("parallel"),
        name="sgu",
    )(gs, ln_g, ln_b, w_s, b_t)


def _merge_kernel(h_ref, yr_ref, ya_ref, ys_ref, g0_ref, g1_ref, g2_ref, wr_ref, wa_ref, ws_ref,
                  wo_ref, pg_ref, o_ref, acc_scr):
    j = pl.program_id(1)

    @pl.when(j == 0)
    def _():
        acc_scr[...] = jnp.zeros_like(acc_scr)

    merged = (g0_ref[...] * _dot(yr_ref[...], wr_ref[...])
              + g1_ref[...] * _dot(ya_ref[...], wa_ref[...])
              + g2_ref[...] * _dot(ys_ref[...], ws_ref[...]))
    acc_scr[...] += _dot(merged.astype(BF), wo_ref[...])

    @pl.when(j == pl.num_programs(1) - 1)
    def _():
        o_ref[...] = h_ref[...] + _rms(acc_scr[...], pg_ref[...])


def _merge(h, yr, ya, ys, gates, wr, wa, ws, wo, post_g, *, tm=512, tn=512):
    M, D = h.shape
    nj = D // tn
    act = lambda a: pl.BlockSpec((tm, a.shape[1]), lambda i, j: (i, 0))
    gate = lambda br: pl.BlockSpec((tm, tn), lambda i, j: (i, br * nj + j))
    wcol = lambda a: pl.BlockSpec((a.shape[0], tn), lambda i, j: (0, j))
    return pl.pallas_call(
        _merge_kernel,
        grid=(M // tm, nj),
        in_specs=[
            pl.BlockSpec((tm, D), lambda i, j: (i, 0)),
            act(yr), act(ya), act(ys), gate(0), gate(1), gate(2),
            wcol(wr), wcol(wa), wcol(ws),
            pl.BlockSpec((tn, D), lambda i, j: (j, 0)),
            pl.BlockSpec((1, D), lambda i, j: (0, 0)),
        ],
        out_specs=pl.BlockSpec((tm, D), lambda i, j: (i, 0)),
        out_shape=jax.ShapeDtypeStruct((M, D), F32),
        scratch_shapes=[pltpu.VMEM((tm, D), F32)],
        compiler_params=_cparams("parallel", "arbitrary"),
        name="merge_out",
    )(h, yr, ya, ys, gates, gates, gates, wr, wa, ws, wo, post_g)


def _alibi_slopes(n_heads):
    def geometric(n):
        start = 2.0 ** (-8.0 / n)
        return [start ** (i + 1) for i in range(n)]
    closest = 2 ** int(math.floor(math.log2(n_heads)))
    slopes = geometric(closest)
    if closest < n_heads:
        slopes += geometric(2 * closest)[0::2][: n_heads - closest]
    return np.array(sorted(slopes, reverse=True), dtype=np.float32)


def _pad_to(x, n, axis):
    pad = [(0, 0)] * x.ndim
    pad[axis] = (0, n - x.shape[axis])
    return jnp.pad(x, pad)


def _ffn_weights(w_gu, w_down, tf):
    F = w_down.shape[0]
    Fp = -(-F // tf) * tf
    wg = _pad_to(w_gu[:, :F].astype(BF), Fp, 1)
    wu = _pad_to(w_gu[:, F:].astype(BF), Fp, 1)
    wd = _pad_to(w_down.astype(BF), Fp, 0)
    return wg, wu, wd


def _shift_layout(w_shift, extra):
    W = RWKV_WIDTH
    o = 3 * W
    parts = [
        w_shift[..., :o],
        _pad_to(w_shift[..., o:o + DECAY_LORA], 128, -1),
        _pad_to(w_shift[..., o + DECAY_LORA:o + DECAY_LORA + ICLR_LORA], 128, -1),
        _pad_to(w_shift[..., o + DECAY_LORA + ICLR_LORA:], 256, -1),
        _pad_to(extra, 256, -1),
    ]
    return jnp.concatenate(parts, axis=-1)


def kernel(x, ffn1_pre_g, ffn1_w_gu, ffn1_w_down, ffn1_post_g, mix_pre_g, w_in, shift_mu, decay_w0, decay_w2, iclr_a0, iclr_a2, gate_g2, k_k, k_a, r_k, lnx_w, lnx_b, vres_w1, vres_mu, vres_v0, vres_w2, sgu_ln_g, sgu_ln_b, sgu_w_s, sgu_b, w_b_rwkv, w_b_attn, w_b_sgu, w_out, mix_post_g, ffn2_pre_g, ffn2_w_gu, ffn2_w_down, ffn2_post_g):
    B, T, D = x.shape
    depth = w_in.shape[0]
    W = RWKV_WIDTH
    n_shift = 3 * W + DECAY_LORA + ICLR_LORA + GATE_LORA
    tf = 512
    slopes = jnp.asarray(_alibi_slopes(ATTN_HEADS))
    row = lambda p: p.reshape(1, -1)

    h = x.reshape(B * T, D)
    v_first = None
    for l in range(depth):
        wg, wu, wd = _ffn_weights(ffn1_w_gu[l], ffn1_w_down[l], tf)
        h = _ffn(h, row(ffn1_pre_g[l]), wg, wu, wd, row(ffn1_post_g[l]), tf=tf)

        if l == 0:
            extra_w = jnp.zeros((D, VRES_LORA), F32)
            extra_mu = jnp.zeros((VRES_LORA,), F32)
        else:
            extra_w = vres_w1[l - 1]
            extra_mu = vres_mu[l - 1]
        w_shift = _shift_layout(w_in[l][:, :n_shift], extra_w).astype(BF)
        mu = _shift_layout(shift_mu[l], extra_mu).reshape(1, -1)
        o = n_shift
        w_qkv = w_in[l][:, o:o + N_QKV].astype(BF)
        w_sgu = w_in[l][:, o + N_QKV:o + N_QKV + N_SGU].astype(BF)
        w_gate = w_in[l][:, o + N_QKV + N_SGU:].astype(BF)
        g_in = row(mix_pre_g[l])
        p_shift = _norm_matmul(h, g_in, w_shift, name="proj_shift")
        qkv = _norm_matmul(h, g_in, w_qkv, name="proj_qkv")
        gs = _norm_matmul(h, g_in, w_sgu, act="gelu", tn=512, name="proj_sgu")
        gates = _norm_matmul(h, g_in, w_gate, act="sigmoid", name="proj_gate")

        vres = None
        if l > 0:
            vres = (v_first, row(vres_v0[l - 1]), _pad_to(vres_w2[l - 1], 128, 0).astype(BF))
        r, k, v, lw, kkr, a, g = _rwkv_prep(
            p_shift, mu, row(decay_w0[l]), _pad_to(decay_w2[l], 128, 0).astype(BF),
            row(iclr_a0[l]), _pad_to(iclr_a2[l], 128, 0).astype(BF),
            _pad_to(gate_g2[l], 256, 0).astype(BF), row(k_k[l]), row(k_a[l]), vres, seq=T)
        if l == 0:
            v_first = v
        y_rwkv = _wkv(r, k, v, lw, kkr, a, g, row(r_k[l]), row(lnx_w[l]), row(lnx_b[l]),
                      batch=B, seq=T)

        y_attn = _attention(qkv, slopes, batch=B, seq=T)
        y_sgu = _sgu(gs, row(sgu_ln_g[l]), row(sgu_ln_b[l]), sgu_w_s[l], sgu_b[l].T)

        h = _merge(h, y_rwkv, y_attn, y_sgu, gates, w_b_rwkv[l].astype(BF),
                   w_b_attn[l].astype(BF), w_b_sgu[l].astype(BF), w_out[l].astype(BF),
                   row(mix_post_g[l]))

        wg, wu, wd = _ffn_weights(ffn2_w_gu[l], ffn2_w_down[l], tf)
        h = _ffn(h, row(ffn2_pre_g[l]), wg, wu, wd, row(ffn2_post_g[l]), tf=tf)
    return h.reshape(B, T, D)
```

```python
import functools
import math

import numpy as np
import jax
import jax.numpy as jnp
from jax import lax
from jax.experimental import pallas as pl
from jax.experimental.pallas import tpu as pltpu

BF = jnp.bfloat16
F32 = jnp.float32

D_MODEL = 2048
RWKV_HEAD_DIM = 64
RWKV_WIDTH = D_MODEL // 2
DECAY_LORA = 64
ICLR_LORA = 64
VRES_LORA = 32
GATE_LORA = 160
LNX_EPS = 64e-5
ATTN_GROUPS = ((128, 1), (512, 4), (2048, 16))
ATTN_HEADS_PER_GROUP = 4
ATTN_HEAD_DIM = 128
ATTN_HEADS = len(ATTN_GROUPS) * ATTN_HEADS_PER_GROUP
ATTN_BLK = 128
ATTN_BLOCKS_PER_ITER = 4
SGU_CHUNK = 128
SGU_GROUPS = 4
SGU_WIDTH = D_MODEL // 4
RMS_EPS = 1e-6
LN_EPS = 1e-5
N_BRANCHES = 3

LANES = 128
WKV_CHUNK = 64
NEG_BIG = -1e30

SEG_R, SEG_K, SEG_V = 0, RWKV_WIDTH, 2 * RWKV_WIDTH
SEG_W = 3 * RWKV_WIDTH
SEG_A = SEG_W + 128
SEG_G = SEG_A + 128
SEG_P = SEG_G + 256
N_SHIFT_PAD = SEG_P + 256
N_QKV = 3 * ATTN_HEADS * ATTN_HEAD_DIM
N_SGU = 2 * SGU_WIDTH
N_GATE = N_BRANCHES * D_MODEL

VMEM_LIMIT = 56 * 1024 * 1024


def _cparams(*sem):
    return pltpu.CompilerParams(dimension_semantics=sem, vmem_limit_bytes=VMEM_LIMIT)


def _dot(a, b):
    return jnp.dot(a, b, preferred_element_type=F32)


def _dot_nt(a, b):
    return lax.dot_general(a, b, (((1,), (1,)), ((), ())), preferred_element_type=F32)


def _rms(x, g):
    return x * lax.rsqrt(jnp.mean(x * x, axis=-1, keepdims=True) + RMS_EPS) * g


def _ffn_kernel(h_ref, pre_g_ref, wg_ref, wu_ref, wd_ref, post_g_ref, o_ref, xn_scr, acc_scr):
    f = pl.program_id(1)

    @pl.when(f == 0)
    def _():
        xn_scr[...] = _rms(h_ref[...], pre_g_ref[...]).astype(BF)
        acc_scr[...] = jnp.zeros_like(acc_scr)

    xn = xn_scr[...]
    g = _dot(xn, wg_ref[...])
    u = _dot(xn, wu_ref[...])
    a = (g * jax.nn.sigmoid(g) * u).astype(BF)
    acc_scr[...] += _dot(a, wd_ref[...])

    @pl.when(f == pl.num_programs(1) - 1)
    def _():
        o_ref[...] = h_ref[...] + 0.5 * _rms(acc_scr[...], post_g_ref[...])


def _ffn(h, pre_g, wg, wu, wd, post_g, *, tm=512, tf=512):
    M, D = h.shape
    Fp = wg.shape[1]
    return pl.pallas_call(
        _ffn_kernel,
        grid=(M // tm, Fp // tf),
        in_specs=[
            pl.BlockSpec((tm, D), lambda i, f: (i, 0)),
            pl.BlockSpec((1, D), lambda i, f: (0, 0)),
            pl.BlockSpec((D, tf), lambda i, f: (0, f)),
            pl.BlockSpec((D, tf), lambda i, f: (0, f)),
            pl.BlockSpec((tf, D), lambda i, f: (f, 0)),
            pl.BlockSpec((1, D), lambda i, f: (0, 0)),
        ],
        out_specs=pl.BlockSpec((tm, D), lambda i, f: (i, 0)),
        out_shape=jax.ShapeDtypeStruct((M, D), F32),
        scratch_shapes=[pltpu.VMEM((tm, D), BF), pltpu.VMEM((tm, D), F32)],
        compiler_params=_cparams("parallel", "arbitrary"),
        name="ffn",
    )(h, pre_g, wg, wu, wd, post_g)


def _gelu_tanh(x):
    c = math.sqrt(2.0 / math.pi)
    return x * (0.5 * (1.0 + jnp.tanh(c * (x + 0.044715 * (x * x * x)))))


def _nmm_kernel(h_ref, g_ref, w_ref, o_ref, xn_scr, *, act):
    @pl.when(pl.program_id(1) == 0)
    def _():
        xn_scr[...] = _rms(h_ref[...], g_ref[...]).astype(BF)

    y = _dot(xn_scr[...], w_ref[...])
    if act == "gelu":
        y = _gelu_tanh(y)
    elif act == "sigmoid":
        y = jax.nn.sigmoid(y)
    o_ref[...] = y.astype(o_ref.dtype)


def _norm_matmul(h, g, w, *, act=None, out_dtype=F32, tm=1024, tn=768, name="in_proj"):
    M, D = h.shape
    N = w.shape[1]
    return pl.pallas_call(
        functools.partial(_nmm_kernel, act=act),
        grid=(M // tm, N // tn),
        in_specs=[
            pl.BlockSpec((tm, D), lambda i, j: (i, 0)),
            pl.BlockSpec((1, D), lambda i, j: (0, 0)),
            pl.BlockSpec((D, tn), lambda i, j: (0, j)),
        ],
        out_specs=pl.BlockSpec((tm, tn), lambda i, j: (i, j)),
        out_shape=jax.ShapeDtypeStruct((M, N), out_dtype),
        scratch_shapes=[pltpu.VMEM((tm, D), BF)],
        compiler_params=_cparams("parallel", "arbitrary"),
        name=name,
    )(h, g, w)


def _prep_kernel(*refs, seq, has_vres):
    if has_vres:
        (ps_ref, prev_ref, mu_ref, w0_ref, w2_ref, a0_ref, a2_ref, g2_ref, kk_ref, ka_ref,
         vf_ref, v0_ref, vw2_ref,
         r_o, k_o, v_o, lw_o, kk_o, a_o, g_o) = refs
    else:
        (ps_ref, prev_ref, mu_ref, w0_ref, w2_ref, a0_ref, a2_ref, g2_ref, kk_ref, ka_ref,
         r_o, k_o, v_o, lw_o, kk_o, a_o, g_o) = refs
    tm = ps_ref.shape[0]
    W = RWKV_WIDTH
    x = ps_ref[...]
    at_seq_start = (pl.program_id(0) * tm) % seq == 0
    last_prev = jnp.where(at_seq_start, 0.0, prev_ref[7:8, :])
    row = lax.broadcasted_iota(jnp.int32, x.shape, 0)
    prev = jnp.where(row == 0, last_prev, pltpu.roll(x, 1, axis=0))
    xs = x + (prev - x) * mu_ref[...]

    xr = xs[:, SEG_R:SEG_R + W]
    xk = xs[:, SEG_K:SEG_K + W]
    xv = xs[:, SEG_V:SEG_V + W]
    xw = xs[:, SEG_W:SEG_W + 128]
    xa = xs[:, SEG_A:SEG_A + 128]
    xg = xs[:, SEG_G:SEG_G + 256]

    z = w0_ref[...] + _dot(jnp.tanh(xw).astype(BF), w2_ref[...])
    w_log = -jax.nn.softplus(-z) - 0.5
    lw_o[...] = -jnp.exp(w_log)
    a = jax.nn.sigmoid(a0_ref[...] + _dot(xa.astype(BF), a2_ref[...]))
    a_o[...] = a
    g_o[...] = _dot(jax.nn.sigmoid(xg).astype(BF), g2_ref[...])
    kk_o[...] = xk * kk_ref[...]
    k_o[...] = xk * (1.0 + (a - 1.0) * ka_ref[...])
    r_o[...] = xr
    if has_vres:
        pv = xs[:, SEG_P:SEG_P + 128]
        mix = jax.nn.sigmoid(v0_ref[...] + _dot(pv.astype(BF), vw2_ref[...]))
        v_o[...] = xv + (vf_ref[...] - xv) * mix
    else:
        v_o[...] = xv


def _rwkv_prep(ps, mu, w0, w2, a0, a2, g2, k_k, k_a, vres, *, seq, tm=256):
    M, NA = ps.shape
    W = RWKV_WIDTH
    row = lambda n: pl.BlockSpec((1, n), lambda i: (0, 0))
    full = lambda a: pl.BlockSpec(a.shape, lambda i: (0, 0))
    tok = pl.BlockSpec((tm, W), lambda i: (i, 0))
    in_specs = [
        pl.BlockSpec((tm, NA), lambda i: (i, 0)),
        pl.BlockSpec((8, NA), lambda i: (jnp.maximum(i * (tm // 8) - 1, 0), 0)),
        row(NA), row(W), full(w2), row(W), full(a2), full(g2), row(W), row(W),
    ]
    args = [ps, ps, mu, w0, w2, a0, a2, g2, k_k, k_a]
    if vres is not None:
        v_first, v0, vw2 = vres
        in_specs += [tok, row(W), full(vw2)]
        args += [v_first, v0, vw2]
    return pl.pallas_call(
        functools.partial(_prep_kernel, seq=seq, has_vres=vres is not None),
        grid=(M // tm,),
        in_specs=in_specs,
        out_specs=[tok] * 7,
        out_shape=[jax.ShapeDtypeStruct((M, W), F32)] * 7,
        compiler_params=_cparams("parallel"),
        name="rwkv_prep",
    )(*args)


def _wkv_kernel(r_ref, k_ref, v_ref, lw_ref, kk_ref, a_ref, g_ref, rk_ref, lnw_ref, lnb_ref,
                y_ref, p_scr):
    C = WKV_CHUNK
    C2 = 2 * C

    @pl.when(pl.program_id(1) == 0)
    def _():
        p_scr[...] = jnp.zeros_like(p_scr)

    lane = lax.broadcasted_iota(jnp.int32, (C, LANES), 1)
    lo = lane < RWKV_HEAD_DIM
    ti = lax.broadcasted_iota(jnp.int32, (C, C), 0)
    tj = lax.broadcasted_iota(jnp.int32, (C, C), 1)
    tri = jnp.where(ti >= tj, 1.0, 0.0).astype(BF)
    si = lax.broadcasted_iota(jnp.int32, (C2, C2), 0)
    sj = lax.broadcasted_iota(jnp.int32, (C2, C2), 1)
    same = (si // C) == (sj // C)
    strict = same & ((si % C) > (sj % C))
    incl = same & ((si % C) >= (sj % C))
    eye = si == sj
    eye_f = jnp.where(eye, 1.0, 0.0)

    def seg_sum(x):
        s_lo = jnp.sum(jnp.where(lo, x, 0.0), axis=1, keepdims=True)
        s_hi = jnp.sum(jnp.where(lo, 0.0, x), axis=1, keepdims=True)
        return jnp.where(lo, s_lo, s_hi)

    def stack(x):
        return jnp.concatenate([jnp.where(lo, x, 0.0), jnp.where(lo, 0.0, x)], axis=0)

    pairs = range(r_ref.shape[1] // LANES)
    cols = [slice(p * LANES, (p + 1) * LANES) for p in pairs]
    each = lambda f, *xs: [f(*t) for t in zip(*xs)]
    bf = lambda xs: [x.astype(BF) for x in xs]

    r = [r_ref[:, cs] for cs in cols]
    k = [k_ref[:, cs] for cs in cols]
    v = [v_ref[:, cs] for cs in cols]
    lw = [lw_ref[:, cs] for cs in cols]
    kkr = [kk_ref[:, cs] for cs in cols]
    a = [a_ref[:, cs] for cs in cols]
    P = [p_scr[p] for p in pairs]

    def cumsum(x):
        hi = x.astype(BF)
        return _dot(tri, hi) + _dot(tri, (x - hi.astype(F32)).astype(BF))

    cum = each(cumsum, lw)
    tot = [c[C - 1:C, :] for c in cum]
    kk = each(lambda x: x / jnp.maximum(jnp.sqrt(seg_sum(x * x)), 1e-12), kkr)
    kka = each(lambda x, y: x * y, kk, a)
    A_st = bf(each(lambda x, c, w: stack(-x * jnp.exp(c - w)), kk, cum, lw))
    R_st = bf(each(lambda x, c: stack(x * jnp.exp(c)), r, cum))
    BK_st = bf(each(lambda x, y, c: jnp.concatenate([stack(x * jnp.exp(-c)), stack(y * jnp.exp(-c))], axis=0),
                    kka, k, cum))
    V_st = bf(each(stack, v))
    BKpT = bf(each(lambda x, y, c, t: jnp.concatenate([stack(x * jnp.exp(t - c)).T,
                                                       stack(y * jnp.exp(t - c)).T], axis=1),
                   kka, k, cum, tot))

    GA = each(_dot_nt, A_st, BK_st)
    GR = each(_dot_nt, R_st, BK_st)
    Aak = bf([jnp.where(strict, x[:, C2:], 0.0) for x in GA])
    Mrb = bf([jnp.where(incl, x[:, :C2], 0.0) for x in GR])
    Mrk = bf([jnp.where(incl, x[:, C2:], 0.0) for x in GR])

    X = [jnp.where(strict, x[:, :C2], 0.0) for x in GA]
    T = [eye_f + x for x in X]
    n = 2
    while n < C:
        Xb = bf(X)
        X = each(_dot, Xb, Xb)
        T = each(lambda t, x: t + _dot(t.astype(BF), x.astype(BF)), T, X)
        n *= 2
    AkV = bf(each(_dot, Aak, V_st))
    WU = each(lambda t, x, y: _dot(t.astype(BF), jnp.concatenate([x, y], axis=1)), T, A_st, AkV)
    Pb = bf(P)
    U = each(lambda wu, pb: _dot(wu[:, :LANES].astype(BF), pb) + wu[:, LANES:], WU, Pb)
    Ub = bf(U)
    gcol = [jnp.exp(jnp.sum(jnp.where(eye, jnp.broadcast_to(t, (C2, LANES)), 0.0), axis=1, keepdims=True))
            for t in tot]
    Pn = each(lambda gc, p_, m, u, vs: gc * p_ + _dot(m, jnp.concatenate([u, vs], axis=0)),
              gcol, P, BKpT, Ub, V_st)
    Y_st = each(lambda rs, pb, mb, u, mk, vs: _dot(rs, pb) + _dot(mb, u) + _dot(mk, vs),
                R_st, Pb, Mrb, Ub, Mrk, V_st)
    y = [x[:C, :] + x[C:, :] for x in Y_st]

    def finish(y_, r_, k_, v_, cs):
        mu = seg_sum(y_) * (1.0 / RWKV_HEAD_DIM)
        yc = y_ - mu
        var = seg_sum(yc * yc) * (1.0 / RWKV_HEAD_DIM)
        yn = yc * lax.rsqrt(var + LNX_EPS) * lnw_ref[:, cs] + lnb_ref[:, cs]
        bonus = seg_sum(r_ * k_ * rk_ref[:, cs]) * v_
        return ((yn + bonus) * g_ref[:, cs]).astype(y_ref.dtype)

    out = each(finish, y, r, k, v, cols)
    for p in pairs:
        p_scr[p] = Pn[p]
    for p in pairs:
        y_ref[:, cols[p]] = out[p]


def _wkv(r, k, v, lw, kkr, a, g, r_k, lnx_w, lnx_b, *, batch, seq):
    M, W = r.shape
    steps = seq // WKV_CHUNK
    tok = pl.BlockSpec((WKV_CHUNK, W), lambda b, c: (b * steps + c, 0))
    par = pl.BlockSpec((1, W), lambda b, c: (0, 0))
    return pl.pallas_call(
        _wkv_kernel,
        grid=(batch, steps),
        in_specs=[tok] * 7 + [par] * 3,
        out_specs=tok,
        out_shape=jax.ShapeDtypeStruct((M, W), BF),
        scratch_shapes=[pltpu.VMEM((W // LANES, LANES, LANES), F32)],
        compiler_params=_cparams("parallel", "arbitrary"),
        name="wkv",
    )(r, k, v, lw, kkr, a, g, r_k, lnx_w, lnx_b)


def _attn_kernel(slopes_ref, q_ref, k_ref, v_ref, o_ref, m_scr, l_scr, acc_scr, *, seq):
    slot = pl.program_id(1)
    grp = pl.program_id(2)
    blk = ATTN_BLK
    scale = ATTN_HEAD_DIM ** -0.5
    qi = lax.broadcasted_iota(jnp.int32, (blk, 2 * blk), 0)
    kj = lax.broadcasted_iota(jnp.int32, (blk, 2 * blk), 1)
    rel = (qi + blk - kj).astype(F32)
    in_band = (kj >= qi) & (kj <= qi + blk)
    is_prev = kj < blk
    nb = ATTN_BLOCKS_PER_ITER
    each = lambda f, *xs: [f(*t) for t in zip(*xs)]

    def rows(start, d):
        if d == 1:
            return pl.ds(start, blk)
        return pl.ds(start, blk, stride=d)

    for gi, (window, d) in enumerate(ATTN_GROUPS):
        assert window // d == blk and seq % window == 0 and (seq // blk) % nb == 0

        @pl.when(grp == gi)
        def _(gi=gi, d=d):
            slope = slopes_ref[gi * ATTN_HEADS_PER_GROUP + slot]
            bias = jnp.where(in_band, -slope * float(d) * rel, NEG_BIG)
            span = blk * d

            def body(it, carry):
                ids = [it * nb + j for j in range(nb)]
                ns = [i // d for i in ids]
                cur = [rows(n * span + i % d, d) for n, i in zip(ns, ids)]
                prv = [rows(jnp.maximum(n - 1, 0) * span + i % d, d) for n, i in zip(ns, ids)]
                q = [q_ref[c, :].astype(BF) for c in cur]
                kcat = [jnp.concatenate([k_ref[p_, :], k_ref[c, :]], axis=0).astype(BF) for p_, c in zip(prv, cur)]
                vcat = [jnp.concatenate([v_ref[p_, :], v_ref[c, :]], axis=0).astype(BF) for p_, c in zip(prv, cur)]
                s = each(lambda q_, k_: _dot_nt(q_, k_) * scale + bias, q, kcat)
                s = each(lambda s_, n: jnp.where(is_prev & (n == 0), NEG_BIG, s_), s, ns)
                m_loc = [jnp.max(s_, axis=1, keepdims=True) for s_ in s]
                p = each(lambda s_, m_: jnp.exp(s_ - m_), s, m_loc)
                l_loc = [jnp.sum(p_, axis=1, keepdims=True) for p_ in p]
                o_loc = each(lambda p_, v_: _dot(p_.astype(BF), v_), p, vcat)
                if gi == 0:
                    m_new, l_new, acc = m_loc, l_loc, o_loc
                else:
                    m_old = [m_scr[c, :][:, :1] for c in cur]
                    l_old = [l_scr[c, :][:, :1] for c in cur]
                    a_old = [acc_scr[c, :] for c in cur]
                    m_new = each(jnp.maximum, m_old, m_loc)
                    alpha = each(lambda mo, mn: jnp.exp(mo - mn), m_old, m_new)
                    beta = each(lambda ml, mn: jnp.exp(ml - mn), m_loc, m_new)
                    l_new = each(lambda al, lo_, be, ll: al * lo_ + be * ll, alpha, l_old, beta, l_loc)
                    acc = each(lambda al, ao, be, ol: al * ao + be * ol, alpha, a_old, beta, o_loc)
                for c, m_, l_, a_ in zip(cur, m_new, l_new, acc):
                    m_scr[c, :] = jnp.broadcast_to(m_, (blk, LANES))
                    l_scr[c, :] = jnp.broadcast_to(l_, (blk, LANES))
                    acc_scr[c, :] = a_
                return carry

            lax.fori_loop(0, seq // blk // nb, body, 0)

    @pl.when(grp == len(ATTN_GROUPS) - 1)
    def _():
        o_ref[...] = (acc_scr[...] / l_scr[...]).astype(o_ref.dtype)


def _attention(qkv, slopes, *, batch, seq):
    M = qkv.shape[0]
    E = ATTN_HEAD_DIM
    hpg = ATTN_HEADS_PER_GROUP

    def spec(which):
        return pl.BlockSpec((seq, E), lambda b, s, g: (b, which * ATTN_HEADS + g * hpg + s))

    return pl.pallas_call(
        functools.partial(_attn_kernel, seq=seq),
        grid=(batch, hpg, len(ATTN_GROUPS)),
        in_specs=[pl.BlockSpec(memory_space=pltpu.SMEM), spec(0), spec(1), spec(2)],
        out_specs=pl.BlockSpec((seq, E), lambda b, s, g: (b, s)),
        out_shape=jax.ShapeDtypeStruct((M, hpg * E), BF),
        scratch_shapes=[pltpu.VMEM((seq, LANES), F32)] * 3,
        compiler_params=_cparams("parallel", "parallel", "arbitrary"),
        name="dilated_attn",
    )(slopes, qkv, qkv, qkv)


def _sgu_kernel(x_ref, lg_ref, lb_ref, ws_ref, bt_ref, o_ref):
    tm = x_ref.shape[0]
    CH = SGU_CHUNK
    GD = SGU_WIDTH // SGU_GROUPS
    u = x_ref[:, :SGU_WIDTH]
    v = x_ref[:, SGU_WIDTH:]
    mu = jnp.mean(v, axis=-1, keepdims=True)
    vc = v - mu
    var = jnp.mean(vc * vc, axis=-1, keepdims=True)
    vn = (vc * lax.rsqrt(var + LN_EPS) * lg_ref[...] + lb_ref[...]).astype(BF)
    ti = lax.broadcasted_iota(jnp.int32, (CH, CH), 0)
    si = lax.broadcasted_iota(jnp.int32, (CH, CH), 1)
    causal = ti >= si
    for gidx in range(SGU_GROUPS):
        w = jnp.where(causal, ws_ref[gidx], 0.0).astype(BF)
        bias = bt_ref[:, gidx:gidx + 1]
        cs = slice(gidx * GD, (gidx + 1) * GD)
        for c in range(tm // CH):
            rs = slice(c * CH, (c + 1) * CH)
            mixed = _dot(w, vn[rs, cs]) + bias
            o_ref[rs, cs] = (u[rs, cs] * mixed).astype(o_ref.dtype)


def _sgu(gs, ln_g, ln_b, w_s, b_t, *, tm=512):
    M = gs.shape[0]
    return pl.pallas_call(
        _sgu_kernel,
        grid=(M // tm,),
        in_specs=[
            pl.BlockSpec((tm, N_SGU), lambda i: (i, 0)),
            pl.BlockSpec((1, SGU_WIDTH), lambda i: (0, 0)),
            pl.BlockSpec((1, SGU_WIDTH), lambda i: (0, 0)),
            pl.BlockSpec(w_s.shape, lambda i: (0, 0, 0)),
            pl.BlockSpec(b_t.shape, lambda i: (0, 0)),
        ],
        out_specs=pl.BlockSpec((tm, SGU_WIDTH), lambda i: (i, 0)),
        out_shape=jax.ShapeDtypeStruct((M, SGU_WIDTH), BF),
        compiler_params=_cparams("parallel"),
        name="sgu",
    )(gs, ln_g, ln_b, w_s, b_t)


def _merge_kernel(h_ref, yr_ref, ya_ref, ys_ref, g0_ref, g1_ref, g2_ref, wr_ref, wa_ref, ws_ref,
                  wo_ref, pg_ref, o_ref, acc_scr):
    j = pl.program_id(1)

    @pl.when(j == 0)
    def _():
        acc_scr[...] = jnp.zeros_like(acc_scr)

    merged = (g0_ref[...] * _dot(yr_ref[...], wr_ref[...])
              + g1_ref[...] * _dot(ya_ref[...], wa_ref[...])
              + g2_ref[...] * _dot(ys_ref[...], ws_ref[...]))
    acc_scr[...] += _dot(merged.astype(BF), wo_ref[...])

    @pl.when(j == pl.num_programs(1) - 1)
    def _():
        o_ref[...] = h_ref[...] + _rms(acc_scr[...], pg_ref[...])


def _merge(h, yr, ya, ys, gates, wr, wa, ws, wo, post_g, *, tm=512, tn=512):
    M, D = h.shape
    nj = D // tn
    act = lambda a: pl.BlockSpec((tm, a.shape[1]), lambda i, j: (i, 0))
    gate = lambda br: pl.BlockSpec((tm, tn), lambda i, j: (i, br * nj + j))
    wcol = lambda a: pl.BlockSpec((a.shape[0], tn), lambda i, j: (0, j))
    return pl.pallas_call(
        _merge_kernel,
        grid=(M // tm, nj),
        in_specs=[
            pl.BlockSpec((tm, D), lambda i, j: (i, 0)),
            act(yr), act(ya), act(ys), gate(0), gate(1), gate(2),
            wcol(wr), wcol(wa), wcol(ws),
            pl.BlockSpec((tn, D), lambda i, j: (j, 0)),
            pl.BlockSpec((1, D), lambda i, j: (0, 0)),
        ],
        out_specs=pl.BlockSpec((tm, D), lambda i, j: (i, 0)),
        out_shape=jax.ShapeDtypeStruct((M, D), F32),
        scratch_shapes=[pltpu.VMEM((tm, D), F32)],
        compiler_params=_cparams("parallel", "arbitrary"),
        name="merge_out",
    )(h, yr, ya, ys, gates, gates, gates, wr, wa, ws, wo, post_g)


def _alibi_slopes(n_heads):
    def geometric(n):
        start = 2.0 ** (-8.0 / n)
        return [start ** (i + 1) for i in range(n)]
    closest = 2 ** int(math.floor(math.log2(n_heads)))
    slopes = geometric(closest)
    if closest < n_heads:
        slopes += geometric(2 * closest)[0::2][: n_heads - closest]
    return np.array(sorted(slopes, reverse=True), dtype=np.float32)


def _pad_to(x, n, axis):
    pad = [(0, 0)] * x.ndim
    pad[axis] = (0, n - x.shape[axis])
    return jnp.pad(x, pad)


def _ffn_weights(w_gu, w_down, tf):
    F = w_down.shape[0]
    Fp = -(-F // tf) * tf
    wg = _pad_to(w_gu[:, :F].astype(BF), Fp, 1)
    wu = _pad_to(w_gu[:, F:].astype(BF), Fp, 1)
    wd = _pad_to(w_down.astype(BF), Fp, 0)
    return wg, wu, wd


def _shift_layout(w_shift, extra):
    W = RWKV_WIDTH
    o = 3 * W
    parts = [
        w_shift[..., :o],
        _pad_to(w_shift[..., o:o + DECAY_LORA], 128, -1),
        _pad_to(w_shift[..., o + DECAY_LORA:o + DECAY_LORA + ICLR_LORA], 128, -1),
        _pad_to(w_shift[..., o + DECAY_LORA + ICLR_LORA:], 256, -1),
        _pad_to(extra, 256, -1),
    ]
    return jnp.concatenate(parts, axis=-1)


def kernel(x, ffn1_pre_g, ffn1_w_gu, ffn1_w_down, ffn1_post_g, mix_pre_g, w_in, shift_mu, decay_w0, decay_w2, iclr_a0, iclr_a2, gate_g2, k_k, k_a, r_k, lnx_w, lnx_b, vres_w1, vres_mu, vres_v0, vres_w2, sgu_ln_g, sgu_ln_b, sgu_w_s, sgu_b, w_b_rwkv, w_b_attn, w_b_sgu, w_out, mix_post_g, ffn2_pre_g, ffn2_w_gu, ffn2_w_down, ffn2_post_g):
    B, T, D = x.shape
    depth = w_in.shape[0]
    W = RWKV_WIDTH
    n_shift = 3 * W + DECAY_LORA + ICLR_LORA + GATE_LORA
    tf = 512
    slopes = jnp.asarray(_alibi_slopes(ATTN_HEADS))
    row = lambda p: p.reshape(1, -1)

    h = x.reshape(B * T, D)
    v_first = None
    for l in range(depth):
        wg, wu, wd = _ffn_weights(ffn1_w_gu[l], ffn1_w_down[l], tf)
        h = _ffn(h, row(ffn1_pre_g[l]), wg, wu, wd, row(ffn1_post_g[l]), tf=tf)

        if l == 0:
            extra_w = jnp.zeros((D, VRES_LORA), F32)
            extra_mu = jnp.zeros((VRES_LORA,), F32)
        else:
            extra_w = vres_w1[l - 1]
            extra_mu = vres_mu[l - 1]
        w_shift = _shift_layout(w_in[l][:, :n_shift], extra_w).astype(BF)
        mu = _shift_layout(shift_mu[l], extra_mu).reshape(1, -1)
        o = n_shift
        w_qkv = w_in[l][:, o:o + N_QKV].astype(BF)
        w_sgu = w_in[l][:, o + N_QKV:o + N_QKV + N_SGU].astype(BF)
        w_gate = w_in[l][:, o + N_QKV + N_SGU:].astype(BF)
        g_in = row(mix_pre_g[l])
        p_shift = _norm_matmul(h, g_in, w_shift, name="proj_shift")
        qkv = _norm_matmul(h, g_in, w_qkv, name="proj_qkv")
        gs = _norm_matmul(h, g_in, w_sgu, act="gelu", tn=512, name="proj_sgu")
        gates = _norm_matmul(h, g_in, w_gate, act="sigmoid", name="proj_gate")

        vres = None
        if l > 0:
            vres = (v_first, row(vres_v0[l - 1]), _pad_to(vres_w2[l - 1], 128, 0).astype(BF))
        r, k, v, lw, kkr, a, g = _rwkv_prep(
            p_shift, mu, row(decay_w0[l]), _pad_to(decay_w2[l], 128, 0).astype(BF),
            row(iclr_a0[l]), _pad_to(iclr_a2[l], 128, 0).astype(BF),
            _pad_to(gate_g2[l], 256, 0).astype(BF), row(k_k[l]), row(k_a[l]), vres, seq=T)
        if l == 0:
            v_first = v
        y_rwkv = _wkv(r, k, v, lw, kkr, a, g, row(r_k[l]), row(lnx_w[l]), row(lnx_b[l]),
                      batch=B, seq=T)

        y_attn = _attention(qkv, slopes, batch=B, seq=T)
        y_sgu = _sgu(gs, row(sgu_ln_g[l]), row(sgu_ln_b[l]), sgu_w_s[l], sgu_b[l].T)

        h = _merge(h, y_rwkv, y_attn, y_sgu, gates, w_b_rwkv[l].astype(BF),
                   w_b_attn[l].astype(BF), w_b_sgu[l].astype(BF), w_out[l].astype(BF),
                   row(mix_post_g[l]))

        wg, wu, wd = _ffn_weights(ffn2_w_gu[l], ffn2_w_down[l], tf)
        h = _ffn(h, row(ffn2_pre_g[l]), wg, wu, wd, row(ffn2_post_g[l]), tf=tf)
    return h.reshape(B, T, D)
```

```python
import functools
import math

import numpy as np
import jax
import jax.numpy as jnp
from jax import lax
from jax.experimental import pallas as pl
from jax.experimental.pallas import tpu as pltpu

BF = jnp.bfloat16
F32 = jnp.float32

D_MODEL = 2048
RWKV_HEAD_DIM = 64
RWKV_WIDTH = D_MODEL // 2
DECAY_LORA = 64
ICLR_LORA = 64
VRES_LORA = 32
GATE_LORA = 160
LNX_EPS = 64e-5
ATTN_GROUPS = ((128, 1), (512, 4), (2048, 16))
ATTN_HEADS_PER_GROUP = 4
ATTN_HEAD_DIM = 128
ATTN_HEADS = len(ATTN_GROUPS) * ATTN_HEADS_PER_GROUP
ATTN_BLK = 128
ATTN_BLOCKS_PER_ITER = 4
SGU_CHUNK = 128
SGU_GROUPS = 4
SGU_WIDTH = D_MODEL // 4
RMS_EPS = 1e-6
LN_EPS = 1e-5
N_BRANCHES = 3

LANES = 128
WKV_CHUNK = 64
FFN_TF = 512
NEG_BIG = -1e30

N_SHIFT = 3 * RWKV_WIDTH + DECAY_LORA + ICLR_LORA + GATE_LORA
SEG_R, SEG_K, SEG_V = 0, RWKV_WIDTH, 2 * RWKV_WIDTH
SEG_WA = 3 * RWKV_WIDTH
SEG_G = SEG_WA + 128
SEG_P = SEG_G + 256
SHIFT_BLOCKS = SEG_P // LANES
LANE_OFF = N_SHIFT % LANES
N_QKV = 3 * ATTN_HEADS * ATTN_HEAD_DIM
N_SGU = 2 * SGU_WIDTH
N_GATE = N_BRANCHES * D_MODEL
COL_SHIFT = 0
COL_QKV = SEG_P + LANES
COL_SGU = COL_QKV + N_QKV
COL_GATE = COL_SGU + N_SGU
N_PROJ = COL_GATE + N_GATE

VMEM_LIMIT = 56 * 1024 * 1024


def _cparams(*sem):
    return pltpu.CompilerParams(dimension_semantics=sem, vmem_limit_bytes=VMEM_LIMIT)


def _dot(a, b):
    return jnp.dot(a, b, preferred_element_type=F32)


def _dot_nt(a, b):
    return lax.dot_general(a, b, (((1,), (1,)), ((), ())), preferred_element_type=F32)


def _rms(x, g):
    return x * lax.rsqrt(jnp.mean(x * x, axis=-1, keepdims=True) + RMS_EPS) * g


def _ffn_kernel(h_ref, pre_g_ref, wg_ref, wu_ref, wd_ref, post_g_ref, o_ref, xn_scr, acc_scr):
    f = pl.program_id(1)

    @pl.when(f == 0)
    def _():
        xn_scr[...] = _rms(h_ref[...], pre_g_ref[...]).astype(BF)
        acc_scr[...] = jnp.zeros_like(acc_scr)

    xn = xn_scr[...]
    g = _dot(xn, wg_ref[...])
    u = _dot(xn, wu_ref[...])
    a = (g * jax.nn.sigmoid(g) * u).astype(BF)
    acc_scr[...] += _dot(a, wd_ref[...])

    @pl.when(f == pl.num_programs(1) - 1)
    def _():
        o_ref[...] = h_ref[...] + 0.5 * _rms(acc_scr[...], post_g_ref[...])


def _ffn(h, pre_g, wgu, wd, post_g, *, tm=512, tf=FFN_TF):
    M, D = h.shape
    Fp = wd.shape[0]
    nf = Fp // tf
    assert wgu.shape == (D, 2 * Fp) and nf * tf == Fp
    return pl.pallas_call(
        _ffn_kernel,
        grid=(M // tm, nf),
        in_specs=[
            pl.BlockSpec((tm, D), lambda i, f: (i, 0)),
            pl.BlockSpec((1, D), lambda i, f: (0, 0)),
            pl.BlockSpec((D, tf), lambda i, f: (0, f)),
            pl.BlockSpec((D, tf), lambda i, f: (0, nf + f)),
            pl.BlockSpec((tf, D), lambda i, f: (f, 0)),
            pl.BlockSpec((1, D), lambda i, f: (0, 0)),
        ],
        out_specs=pl.BlockSpec((tm, D), lambda i, f: (i, 0)),
        out_shape=jax.ShapeDtypeStruct((M, D), F32),
        scratch_shapes=[pltpu.VMEM((tm, D), BF), pltpu.VMEM((tm, D), F32)],
        compiler_params=_cparams("parallel", "arbitrary"),
        name="ffn",
    )(h, pre_g, wgu, wgu, wd, post_g)


def _gelu_tanh(x):
    c = math.sqrt(2.0 / math.pi)
    return x * (0.5 * (1.0 + jnp.tanh(c * (x + 0.044715 * (x * x * x)))))


def _nmm_kernel(h_ref, g_ref, w_ref, o_ref, xn_scr):
    @pl.when(pl.program_id(1) == 0)
    def _():
        xn_scr[...] = _rms(h_ref[...], g_ref[...]).astype(BF)

    o_ref[...] = _dot(xn_scr[...], w_ref[...]).astype(o_ref.dtype)


def _norm_matmul(h, g, w, *, out_dtype=F32, tm=1024, tn=1536, name="in_proj"):
    M, D = h.shape
    N = w.shape[1]
    return pl.pallas_call(
        _nmm_kernel,
        grid=(M // tm, N // tn),
        in_specs=[
            pl.BlockSpec((tm, D), lambda i, j: (i, 0)),
            pl.BlockSpec((1, D), lambda i, j: (0, 0)),
            pl.BlockSpec((D, tn), lambda i, j: (0, j)),
        ],
        out_specs=pl.BlockSpec((tm, tn), lambda i, j: (i, j)),
        out_shape=jax.ShapeDtypeStruct((M, N), out_dtype),
        scratch_shapes=[pltpu.VMEM((tm, D), BF)],
        compiler_params=_cparams("parallel", "arbitrary"),
        name=name,
    )(h, g, w)


def _prep_kernel(*refs, seq, has_vres):
    if has_vres:
        (ps_ref, prev_ref, mu_ref, w0_ref, w2_ref, a0_ref, a2_ref, g2_ref, kk_ref, ka_ref,
         vf_ref, v0_ref, vw2_ref,
         r_o, k_o, v_o, lw_o, kk_o, a_o, g_o) = refs
    else:
        (ps_ref, prev_ref, mu_ref, w0_ref, w2_ref, a0_ref, a2_ref, g2_ref, kk_ref, ka_ref,
         r_o, k_o, v_o, lw_o, kk_o, a_o, g_o) = refs
    tm = ps_ref.shape[0]
    W = RWKV_WIDTH
    x = ps_ref[...]
    at_seq_start = (pl.program_id(0) * tm) % seq == 0
    last_prev = jnp.where(at_seq_start, 0.0, prev_ref[7:8, :])
    row = lax.broadcasted_iota(jnp.int32, x.shape, 0)
    prev = jnp.where(row == 0, last_prev, pltpu.roll(x, 1, axis=0))
    xs = x + (prev - x) * mu_ref[...]

    xr = xs[:, SEG_R:SEG_R + W]
    xk = xs[:, SEG_K:SEG_K + W]
    xv = xs[:, SEG_V:SEG_V + W]
    xwa = xs[:, SEG_WA:SEG_WA + 128]
    xg = xs[:, SEG_G:SEG_G + 256]

    z = w0_ref[...] + _dot(jnp.tanh(xwa).astype(BF), w2_ref[...])
    w_log = -jax.nn.softplus(-z) - 0.5
    lw_o[...] = -jnp.exp(w_log)
    a = jax.nn.sigmoid(a0_ref[...] + _dot(xwa.astype(BF), a2_ref[...]))
    a_o[...] = a
    g_o[...] = _dot(jax.nn.sigmoid(xg).astype(BF), g2_ref[...])
    kk_o[...] = xk * kk_ref[...]
    k_o[...] = xk * (1.0 + (a - 1.0) * ka_ref[...])
    r_o[...] = xr
    if has_vres:
        pv = xs[:, SEG_P:SEG_P + 128]
        mix = jax.nn.sigmoid(v0_ref[...] + _dot(pv.astype(BF), vw2_ref[...]))
        v_o[...] = xv + (vf_ref[...] - xv) * mix
    else:
        v_o[...] = xv


def _rwkv_prep(ps, mu, w0, w2, a0, a2, g2, k_k, k_a, vres, *, seq, tm=256):
    M = ps.shape[0]
    NA = COL_QKV
    W = RWKV_WIDTH
    row = lambda n: pl.BlockSpec((1, n), lambda i: (0, 0))
    full = lambda a: pl.BlockSpec(a.shape, lambda i: (0, 0))
    tok = pl.BlockSpec((tm, W), lambda i: (i, 0))
    in_specs = [
        pl.BlockSpec((tm, NA), lambda i: (i, 0)),
        pl.BlockSpec((8, NA), lambda i: (jnp.maximum(i * (tm // 8) - 1, 0), 0)),
        row(NA), row(W), full(w2), row(W), full(a2), full(g2), row(W), row(W),
    ]
    args = [ps, ps, mu, w0, w2, a0, a2, g2, k_k, k_a]
    if vres is not None:
        v_first, v0, vw2 = vres
        in_specs += [tok, row(W), full(vw2)]
        args += [v_first, v0, vw2]
    return pl.pallas_call(
        functools.partial(_prep_kernel, seq=seq, has_vres=vres is not None),
        grid=(M // tm,),
        in_specs=in_specs,
        out_specs=[tok] * 7,
        out_shape=[jax.ShapeDtypeStruct((M, W), F32)] * 7,
        compiler_params=_cparams("parallel"),
        name="rwkv_prep",
    )(*args)


def _wkv_kernel(r_ref, k_ref, v_ref, lw_ref, kk_ref, a_ref, g_ref, rk_ref, lnw_ref, lnb_ref,
                y_ref, p_scr):
    C = WKV_CHUNK
    C2 = 2 * C

    @pl.when(pl.program_id(1) == 0)
    def _():
        p_scr[...] = jnp.zeros_like(p_scr)

    lane = lax.broadcasted_iota(jnp.int32, (C, LANES), 1)
    lo = lane < RWKV_HEAD_DIM
    ti = lax.broadcasted_iota(jnp.int32, (C, C), 0)
    tj = lax.broadcasted_iota(jnp.int32, (C, C), 1)
    tri = jnp.where(ti >= tj, 1.0, 0.0).astype(BF)
    si = lax.broadcasted_iota(jnp.int32, (C2, C2), 0)
    sj = lax.broadcasted_iota(jnp.int32, (C2, C2), 1)
    same = (si // C) == (sj // C)
    strict = same & ((si % C) > (sj % C))
    incl = same & ((si % C) >= (sj % C))
    eye = si == sj
    eye_f = jnp.where(eye, 1.0, 0.0)

    def seg_sum(x):
        s_lo = jnp.sum(jnp.where(lo, x, 0.0), axis=1, keepdims=True)
        s_hi = jnp.sum(jnp.where(lo, 0.0, x), axis=1, keepdims=True)
        return jnp.where(lo, s_lo, s_hi)

    def stack(x):
        return jnp.concatenate([jnp.where(lo, x, 0.0), jnp.where(lo, 0.0, x)], axis=0)

    pairs = range(r_ref.shape[1] // LANES)
    cols = [slice(p * LANES, (p + 1) * LANES) for p in pairs]
    each = lambda f, *xs: [f(*t) for t in zip(*xs)]
    bf = lambda xs: [x.astype(BF) for x in xs]

    r = [r_ref[:, cs] for cs in cols]
    k = [k_ref[:, cs] for cs in cols]
    v = [v_ref[:, cs] for cs in cols]
    lw = [lw_ref[:, cs] for cs in cols]
    kkr = [kk_ref[:, cs] for cs in cols]
    a = [a_ref[:, cs] for cs in cols]
    P = [p_scr[p] for p in pairs]

    def cumsum(x):
        hi = x.astype(BF)
        return _dot(tri, hi) + _dot(tri, (x - hi.astype(F32)).astype(BF))

    cum = each(cumsum, lw)
    tot = [c[C - 1:C, :] for c in cum]
    kk = each(lambda x: x / jnp.maximum(jnp.sqrt(seg_sum(x * x)), 1e-12), kkr)
    kka = each(lambda x, y: x * y, kk, a)
    A_st = bf(each(lambda x, c, w: stack(-x * jnp.exp(c - w)), kk, cum, lw))
    R_st = bf(each(lambda x, c: stack(x * jnp.exp(c)), r, cum))
    BK_st = bf(each(lambda x, y, c: jnp.concatenate([stack(x * jnp.exp(-c)), stack(y * jnp.exp(-c))], axis=0),
                    kka, k, cum))
    V_st = bf(each(stack, v))
    BKpT = bf(each(lambda x, y, c, t: jnp.concatenate([stack(x * jnp.exp(t - c)).T,
                                                       stack(y * jnp.exp(t - c)).T], axis=1),
                   kka, k, cum, tot))

    GA = each(_dot_nt, A_st, BK_st)
    GR = each(_dot_nt, R_st, BK_st)
    Aak = bf([jnp.where(strict, x[:, C2:], 0.0) for x in GA])
    Mrb = bf([jnp.where(incl, x[:, :C2], 0.0) for x in GR])
    Mrk = bf([jnp.where(incl, x[:, C2:], 0.0) for x in GR])

    X = [jnp.where(strict, x[:, :C2], 0.0) for x in GA]
    T = [eye_f + x for x in X]
    n = 2
    while n < C:
        Xb = bf(X)
        X = each(_dot, Xb, Xb)
        T = each(lambda t, x: t + _dot(t.astype(BF), x.astype(BF)), T, X)
        n *= 2
    AkV = bf(each(_dot, Aak, V_st))
    WU = each(lambda t, x, y: _dot(t.astype(BF), jnp.concatenate([x, y], axis=1)), T, A_st, AkV)
    Pb = bf(P)
    U = each(lambda wu, pb: _dot(wu[:, :LANES].astype(BF), pb) + wu[:, LANES:], WU, Pb)
    Ub = bf(U)
    gcol = [jnp.exp(jnp.sum(jnp.where(eye, jnp.broadcast_to(t, (C2, LANES)), 0.0), axis=1, keepdims=True))
            for t in tot]
    Pn = each(lambda gc, p_, m, u, vs: gc * p_ + _dot(m, jnp.concatenate([u, vs], axis=0)),
              gcol, P, BKpT, Ub, V_st)
    Y_st = each(lambda rs, pb, mb, u, mk, vs: _dot(rs, pb) + _dot(mb, u) + _dot(mk, vs),
                R_st, Pb, Mrb, Ub, Mrk, V_st)
    y = [x[:C, :] + x[C:, :] for x in Y_st]

    def finish(y_, r_, k_, v_, cs):
        mu = seg_sum(y_) * (1.0 / RWKV_HEAD_DIM)
        yc = y_ - mu
        var = seg_sum(yc * yc) * (1.0 / RWKV_HEAD_DIM)
        yn = yc * lax.rsqrt(var + LNX_EPS) * lnw_ref[:, cs] + lnb_ref[:, cs]
        bonus = seg_sum(r_ * k_ * rk_ref[:, cs]) * v_
        return ((yn + bonus) * g_ref[:, cs]).astype(y_ref.dtype)

    out = each(finish, y, r, k, v, cols)
    for p in pairs:
        p_scr[p] = Pn[p]
    for p in pairs:
        y_ref[:, cols[p]] = out[p]


def _wkv(r, k, v, lw, kkr, a, g, r_k, lnx_w, lnx_b, *, batch, seq):
    M, W = r.shape
    steps = seq // WKV_CHUNK
    tok = pl.BlockSpec((WKV_CHUNK, W), lambda b, c: (b * steps + c, 0))
    par = pl.BlockSpec((1, W), lambda b, c: (0, 0))
    return pl.pallas_call(
        _wkv_kernel,
        grid=(batch, steps),
        in_specs=[tok] * 7 + [par] * 3,
        out_specs=tok,
        out_shape=jax.ShapeDtypeStruct((M, W), BF),
        scratch_shapes=[pltpu.VMEM((W // LANES, LANES, LANES), F32)],
        compiler_params=_cparams("parallel", "arbitrary"),
        name="wkv",
    )(r, k, v, lw, kkr, a, g, r_k, lnx_w, lnx_b)


def _attn_kernel(slopes_ref, q_ref, k_ref, v_ref, o_ref, m_scr, l_scr, acc_scr, *, seq):
    slot = pl.program_id(1)
    grp = pl.program_id(2)
    blk = ATTN_BLK
    scale = ATTN_HEAD_DIM ** -0.5
    qi = lax.broadcasted_iota(jnp.int32, (blk, 2 * blk), 0)
    kj = lax.broadcasted_iota(jnp.int32, (blk, 2 * blk), 1)
    rel = (qi + blk - kj).astype(F32)
    in_band = (kj >= qi) & (kj <= qi + blk)
    is_prev = kj < blk
    nb = ATTN_BLOCKS_PER_ITER
    each = lambda f, *xs: [f(*t) for t in zip(*xs)]

    def rows(start, d):
        if d == 1:
            return pl.ds(start, blk)
        return pl.ds(start, blk, stride=d)

    for gi, (window, d) in enumerate(ATTN_GROUPS):
        assert window // d == blk and seq % window == 0 and (seq // blk) % nb == 0

        @pl.when(grp == gi)
        def _(gi=gi, d=d):
            slope = slopes_ref[gi * ATTN_HEADS_PER_GROUP + slot]
            bias = jnp.where(in_band, -slope * float(d) * rel, NEG_BIG)
            span = blk * d

            def body(it, carry):
                ids = [it * nb + j for j in range(nb)]
                ns = [i // d for i in ids]
                cur = [rows(n * span + i % d, d) for n, i in zip(ns, ids)]
                prv = [rows(jnp.maximum(n - 1, 0) * span + i % d, d) for n, i in zip(ns, ids)]
                q = [q_ref[c, :].astype(BF) for c in cur]
                kcat = [jnp.concatenate([k_ref[p_, :], k_ref[c, :]], axis=0).astype(BF) for p_, c in zip(prv, cur)]
                vcat = [jnp.concatenate([v_ref[p_, :], v_ref[c, :]], axis=0).astype(BF) for p_, c in zip(prv, cur)]
                s = each(lambda q_, k_: _dot_nt(q_, k_) * scale + bias, q, kcat)
                s = each(lambda s_, n: jnp.where(is_prev & (n == 0), NEG_BIG, s_), s, ns)
                m_loc = [jnp.max(s_, axis=1, keepdims=True) for s_ in s]
                p = each(lambda s_, m_: jnp.exp(s_ - m_), s, m_loc)
                l_loc = [jnp.sum(p_, axis=1, keepdims=True) for p_ in p]
                o_loc = each(lambda p_, v_: _dot(p_.astype(BF), v_), p, vcat)
                if gi == 0:
                    m_new, l_new, acc = m_loc, l_loc, o_loc
                else:
                    m_old = [m_scr[c, :][:, :1] for c in cur]
                    l_old = [l_scr[c, :][:, :1] for c in cur]
                    a_old = [acc_scr[c, :] for c in cur]
                    m_new = each(jnp.maximum, m_old, m_loc)
                    alpha = each(lambda mo, mn: jnp.exp(mo - mn), m_old, m_new)
                    beta = each(lambda ml, mn: jnp.exp(ml - mn), m_loc, m_new)
                    l_new = each(lambda al, lo_, be, ll: al * lo_ + be * ll, alpha, l_old, beta, l_loc)
                    acc = each(lambda al, ao, be, ol: al * ao + be * ol, alpha, a_old, beta, o_loc)
                for c, m_, l_, a_ in zip(cur, m_new, l_new, acc):
                    m_scr[c, :] = jnp.broadcast_to(m_, (blk, LANES))
                    l_scr[c, :] = jnp.broadcast_to(l_, (blk, LANES))
                    acc_scr[c, :] = a_
                return carry

            lax.fori_loop(0, seq // blk // nb, body, 0)

    @pl.when(grp == len(ATTN_GROUPS) - 1)
    def _():
        o_ref[...] = (acc_scr[...] / l_scr[...]).astype(o_ref.dtype)


def _attention(qkv, slopes, *, batch, seq):
    M = qkv.shape[0]
    E = ATTN_HEAD_DIM
    hpg = ATTN_HEADS_PER_GROUP

    def spec(which):
        first = COL_QKV // E + which * ATTN_HEADS
        return pl.BlockSpec((seq, E), lambda b, s, g: (b, first + g * hpg + s))

    return pl.pallas_call(
        functools.partial(_attn_kernel, seq=seq),
        grid=(batch, hpg, len(ATTN_GROUPS)),
        in_specs=[pl.BlockSpec(memory_space=pltpu.SMEM), spec(0), spec(1), spec(2)],
        out_specs=pl.BlockSpec((seq, E), lambda b, s, g: (b, s)),
        out_shape=jax.ShapeDtypeStruct((M, hpg * E), BF),
        scratch_shapes=[pltpu.VMEM((seq, LANES), F32)] * 3,
        compiler_params=_cparams("parallel", "parallel", "arbitrary"),
        name="dilated_attn",
    )(slopes, qkv, qkv, qkv)


def _sgu_kernel(x_ref, lg_ref, lb_ref, ws_ref, bt_ref, o_ref):
    tm = x_ref.shape[0]
    CH = SGU_CHUNK
    GD = SGU_WIDTH // SGU_GROUPS
    u = _gelu_tanh(x_ref[:, :SGU_WIDTH])
    v = _gelu_tanh(x_ref[:, SGU_WIDTH:])
    mu = jnp.mean(v, axis=-1, keepdims=True)
    vc = v - mu
    var = jnp.mean(vc * vc, axis=-1, keepdims=True)
    vn = (vc * lax.rsqrt(var + LN_EPS) * lg_ref[...] + lb_ref[...]).astype(BF)
    ti = lax.broadcasted_iota(jnp.int32, (CH, CH), 0)
    si = lax.broadcasted_iota(jnp.int32, (CH, CH), 1)
    causal = ti >= si
    for gidx in range(SGU_GROUPS):
        w = jnp.where(causal, ws_ref[gidx], 0.0).astype(BF)
        bias = bt_ref[:, gidx:gidx + 1]
        cs = slice(gidx * GD, (gidx + 1) * GD)
        for c in range(tm // CH):
            rs = slice(c * CH, (c + 1) * CH)
            mixed = _dot(w, vn[rs, cs]) + bias
            o_ref[rs, cs] = (u[rs, cs] * mixed).astype(o_ref.dtype)


def _sgu(proj, ln_g, ln_b, w_s, b_t, *, tm=512):
    M = proj.shape[0]
    return pl.pallas_call(
        _sgu_kernel,
        grid=(M // tm,),
        in_specs=[
            pl.BlockSpec((tm, N_SGU), lambda i: (i, COL_SGU // N_SGU)),
            pl.BlockSpec((1, SGU_WIDTH), lambda i: (0, 0)),
            pl.BlockSpec((1, SGU_WIDTH), lambda i: (0, 0)),
            pl.BlockSpec(w_s.shape, lambda i: (0, 0, 0)),
            pl.BlockSpec(b_t.shape, lambda i: (0, 0)),
        ],
        out_specs=pl.BlockSpec((tm, SGU_WIDTH), lambda i: (i, 0)),
        out_shape=jax.ShapeDtypeStruct((M, SGU_WIDTH), BF),
        compiler_params=_cparams("parallel"),
        name="sgu",
    )(proj, ln_g, ln_b, w_s, b_t)


def _merge_kernel(h_ref, yr_ref, ya_ref, ys_ref, g0_ref, g1_ref, g2_ref, wr_ref, wa_ref, ws_ref,
                  wo_ref, pg_ref, o_ref, acc_scr):
    j = pl.program_id(1)

    @pl.when(j == 0)
    def _():
        acc_scr[...] = jnp.zeros_like(acc_scr)

    merged = (jax.nn.sigmoid(g0_ref[...]) * _dot(yr_ref[...], wr_ref[...])
              + jax.nn.sigmoid(g1_ref[...]) * _dot(ya_ref[...], wa_ref[...])
              + jax.nn.sigmoid(g2_ref[...]) * _dot(ys_ref[...], ws_ref[...]))
    acc_scr[...] += _dot(merged.astype(BF), wo_ref[...])

    @pl.when(j == pl.num_programs(1) - 1)
    def _():
        o_ref[...] = h_ref[...] + _rms(acc_scr[...], pg_ref[...])


def _merge(h, yr, ya, ys, gates, wr, wa, ws, wo, post_g, *, tm=512, tn=512):
    M, D = h.shape
    nj = D // tn
    act = lambda a: pl.BlockSpec((tm, a.shape[1]), lambda i, j: (i, 0))
    gate = lambda br: pl.BlockSpec((tm, tn), lambda i, j: (i, COL_GATE // tn + br * nj + j))
    wcol = lambda a: pl.BlockSpec((a.shape[0], tn), lambda i, j: (0, j))
    return pl.pallas_call(
        _merge_kernel,
        grid=(M // tm, nj),
        in_specs=[
            pl.BlockSpec((tm, D), lambda i, j: (i, 0)),
            act(yr), act(ya), act(ys), gate(0), gate(1), gate(2),
            wcol(wr), wcol(wa), wcol(ws),
            pl.BlockSpec((tn, D), lambda i, j: (j, 0)),
            pl.BlockSpec((1, D), lambda i, j: (0, 0)),
        ],
        out_specs=pl.BlockSpec((tm, D), lambda i, j: (i, 0)),
        out_shape=jax.ShapeDtypeStruct((M, D), F32),
        scratch_shapes=[pltpu.VMEM((tm, D), F32)],
        compiler_params=_cparams("parallel", "arbitrary"),
        name="merge_out",
    )(h, yr, ya, ys, gates, gates, gates, wr, wa, ws, wo, post_g)


def _alibi_slopes(n_heads):
    def geometric(n):
        start = 2.0 ** (-8.0 / n)
        return [start ** (i + 1) for i in range(n)]
    closest = 2 ** int(math.floor(math.log2(n_heads)))
    slopes = geometric(closest)
    if closest < n_heads:
        slopes += geometric(2 * closest)[0::2][: n_heads - closest]
    return np.array(sorted(slopes, reverse=True), dtype=np.float32)


def _pad_rows(x, before, total):
    return jnp.pad(x, ((before, total - before - x.shape[0]), (0, 0)))


def _cast_kernel(src_ref, o_ref, *, axis, n_valid):
    valid = pl.program_id(axis) < n_valid

    @pl.when(valid)
    def _():
        o_ref[...] = src_ref[...].astype(BF)

    @pl.when(jnp.logical_not(valid))
    def _():
        o_ref[...] = jnp.zeros_like(o_ref)


def _cast_rows(w, l, *, rows, n_out):
    _, R, C = w.shape
    n_src = R // rows
    assert n_src * rows == R and n_out >= n_src
    return pl.pallas_call(
        functools.partial(_cast_kernel, axis=0, n_valid=n_src),
        grid=(n_out,),
        in_specs=[pl.BlockSpec((None, rows, C), lambda j: (l, jnp.minimum(j, n_src - 1), 0))],
        out_specs=pl.BlockSpec((rows, C), lambda j: (j, 0)),
        out_shape=jax.ShapeDtypeStruct((n_out * rows, C), BF),
        compiler_params=_cparams("parallel"),
        name="cast_rows",
    )(w)


def _cast_gate_up(w_gu, l, *, n_out):
    _, D, F2 = w_gu.shape
    n_src = F2 // 2 // LANES
    assert 2 * n_src * LANES == F2 and n_out >= n_src
    return pl.pallas_call(
        functools.partial(_cast_kernel, axis=1, n_valid=n_src),
        grid=(2, n_out),
        in_specs=[pl.BlockSpec((None, D, LANES),
                               lambda hf, j: (l, 0, hf * n_src + jnp.minimum(j, n_src - 1)))],
        out_specs=pl.BlockSpec((D, LANES), lambda hf, j: (0, hf * n_out + j)),
        out_shape=jax.ShapeDtypeStruct((D, 2 * n_out * LANES), BF),
        compiler_params=_cparams("parallel", "parallel"),
        name="cast_gate_up",
    )(w_gu)


def _win_kernel(a_ref, b_ref, x_ref, o_ref):
    j = pl.program_id(0)

    @pl.when(j < SHIFT_BLOCKS)
    def _():
        o_ref[...] = a_ref[...].astype(BF)

    @pl.when(j == SHIFT_BLOCKS)
    def _():
        o_ref[...] = x_ref[...].astype(BF)

    @pl.when(j > SHIFT_BLOCKS)
    def _():
        keep = LANES - LANE_OFF
        lane = lax.broadcasted_iota(jnp.int32, a_ref.shape, 1)
        o_ref[...] = jnp.where(lane < keep, pltpu.roll(a_ref[...], keep, axis=1),
                               pltpu.roll(b_ref[...], keep, axis=1)).astype(BF)


def _cast_w_in(w_in, extra, l):
    _, D, n_in = w_in.shape
    assert n_in - N_SHIFT == N_PROJ - COL_QKV and LANE_OFF > 0
    first = N_SHIFT // LANES
    last = (n_in - 1) // LANES

    def a_idx(j):
        return (l, 0, jnp.where(j <= SHIFT_BLOCKS, jnp.minimum(j, SHIFT_BLOCKS - 1),
                                first + j - SHIFT_BLOCKS - 1))

    def b_idx(j):
        return (l, 0, jnp.clip(first + j - SHIFT_BLOCKS, 0, last))

    return pl.pallas_call(
        _win_kernel,
        grid=(N_PROJ // LANES,),
        in_specs=[pl.BlockSpec((None, D, LANES), a_idx), pl.BlockSpec((None, D, LANES), b_idx),
                  pl.BlockSpec((D, LANES), lambda j: (0, 0))],
        out_specs=pl.BlockSpec((D, LANES), lambda j: (0, j)),
        out_shape=jax.ShapeDtypeStruct((D, N_PROJ), BF),
        compiler_params=_cparams("parallel"),
        name="cast_w_in",
    )(w_in, w_in, extra)


def kernel(x, ffn1_pre_g, ffn1_w_gu, ffn1_w_down, ffn1_post_g, mix_pre_g, w_in, shift_mu, decay_w0, decay_w2, iclr_a0, iclr_a2, gate_g2, k_k, k_a, r_k, lnx_w, lnx_b, vres_w1, vres_mu, vres_v0, vres_w2, sgu_ln_g, sgu_ln_b, sgu_w_s, sgu_b, w_b_rwkv, w_b_attn, w_b_sgu, w_out, mix_post_g, ffn2_pre_g, ffn2_w_gu, ffn2_w_down, ffn2_post_g):
    B, T, D = x.shape
    depth = w_in.shape[0]
    F = ffn1_w_down.shape[1]
    f_blocks = -(-F // FFN_TF) * FFN_TF // LANES
    slopes = jnp.asarray(_alibi_slopes(ATTN_HEADS))
    row = lambda p: p.reshape(1, -1)

    def ffn(h, pre_g, w_gu, w_down, post_g, l):
        wgu = _cast_gate_up(w_gu, l, n_out=f_blocks)
        wd = _cast_rows(w_down, l, rows=LANES, n_out=f_blocks)
        return _ffn(h, row(pre_g[l]), wgu, wd, row(post_g[l]))

    h = x.reshape(B * T, D)
    v_first = None
    for l in range(depth):
        h = ffn(h, ffn1_pre_g, ffn1_w_gu, ffn1_w_down, ffn1_post_g, l)

        if l == 0:
            extra_w = jnp.zeros((D, LANES), F32)
            extra_mu = jnp.zeros((LANES,), F32)
        else:
            extra_w = jnp.pad(vres_w1[l - 1], ((0, 0), (0, LANES - VRES_LORA)))
            extra_mu = jnp.pad(vres_mu[l - 1], (0, LANES - VRES_LORA))
        mu = jnp.concatenate([jnp.pad(shift_mu[l], (0, SEG_P - N_SHIFT)), extra_mu]).reshape(1, -1)
        proj = _norm_matmul(h, row(mix_pre_g[l]), _cast_w_in(w_in, extra_w, l))

        vres = None
        if l > 0:
            vres = (v_first, row(vres_v0[l - 1]), _pad_rows(vres_w2[l - 1], 0, LANES).astype(BF))
        r, k, v, lw, kkr, a, g = _rwkv_prep(
            proj, mu, row(decay_w0[l]), _pad_rows(decay_w2[l], 0, LANES).astype(BF),
            row(iclr_a0[l]), _pad_rows(iclr_a2[l], DECAY_LORA, LANES).astype(BF),
            _pad_rows(gate_g2[l], 0, 2 * LANES).astype(BF), row(k_k[l]), row(k_a[l]), vres, seq=T)
        if l == 0:
            v_first = v
        y_rwkv = _wkv(r, k, v, lw, kkr, a, g, row(r_k[l]), row(lnx_w[l]), row(lnx_b[l]),
                      batch=B, seq=T)

        y_attn = _attention(proj, slopes, batch=B, seq=T)
        y_sgu = _sgu(proj, row(sgu_ln_g[l]), row(sgu_ln_b[l]), sgu_w_s[l], sgu_b[l].T)

        cast = lambda w: _cast_rows(w, l, rows=2 * LANES, n_out=w.shape[1] // (2 * LANES))
        h = _merge(h, y_rwkv, y_attn, y_sgu, proj, cast(w_b_rwkv), cast(w_b_attn), cast(w_b_sgu),
                   cast(w_out), row(mix_post_g[l]))

        h = ffn(h, ffn2_pre_g, ffn2_w_gu, ffn2_w_down, ffn2_post_g, l)
    return h.reshape(B, T, D)
```

```python
import functools
import math

import numpy as np
import jax
import jax.numpy as jnp
from jax import lax
from jax.experimental import pallas as pl
from jax.experimental.pallas import tpu as pltpu

BF = jnp.bfloat16
F32 = jnp.float32

D_MODEL = 2048
RWKV_HEAD_DIM = 64
RWKV_WIDTH = D_MODEL // 2
DECAY_LORA = 64
ICLR_LORA = 64
VRES_LORA = 32
GATE_LORA = 160
LNX_EPS = 64e-5
ATTN_GROUPS = ((128, 1), (512, 4), (2048, 16))
ATTN_HEADS_PER_GROUP = 4
ATTN_HEAD_DIM = 128
ATTN_HEADS = len(ATTN_GROUPS) * ATTN_HEADS_PER_GROUP
ATTN_BLK = 128
ATTN_BLOCKS_PER_ITER = 4
SGU_CHUNK = 128
SGU_GROUPS = 4
SGU_WIDTH = D_MODEL // 4
RMS_EPS = 1e-6
LN_EPS = 1e-5
N_BRANCHES = 3

LANES = 128
WKV_CHUNK = 64
FFN_TF = 512
NEG_BIG = -1e30

N_SHIFT = 3 * RWKV_WIDTH + DECAY_LORA + ICLR_LORA + GATE_LORA
SEG_R, SEG_K, SEG_V = 0, RWKV_WIDTH, 2 * RWKV_WIDTH
SEG_WA = 3 * RWKV_WIDTH
SEG_G = SEG_WA + 128
SEG_P = SEG_G + 256
N_QKV = 3 * ATTN_HEADS * ATTN_HEAD_DIM
N_SGU = 2 * SGU_WIDTH
N_GATE = N_BRANCHES * D_MODEL
COL_SHIFT = 0
COL_QKV = SEG_P + LANES
COL_SGU = COL_QKV + N_QKV
COL_GATE = COL_SGU + N_SGU
N_PROJ = COL_GATE + N_GATE

VMEM_LIMIT = 56 * 1024 * 1024


def _cparams(*sem):
    return pltpu.CompilerParams(dimension_semantics=sem, vmem_limit_bytes=VMEM_LIMIT)


def _dot(a, b):
    return jnp.dot(a, b, preferred_element_type=F32)


def _dot_nt(a, b):
    return lax.dot_general(a, b, (((1,), (1,)), ((), ())), preferred_element_type=F32)


def _rms(x, g):
    return x * lax.rsqrt(jnp.mean(x * x, axis=-1, keepdims=True) + RMS_EPS) * g


def _ffn_kernel(h_ref, pre_g_ref, wg_ref, wu_ref, wd_ref, post_g_ref, o_ref, xn_scr, acc_scr):
    f = pl.program_id(1)

    @pl.when(f == 0)
    def _():
        xn_scr[...] = _rms(h_ref[...], pre_g_ref[...]).astype(BF)
        acc_scr[...] = jnp.zeros_like(acc_scr)

    xn = xn_scr[...]
    g = _dot(xn, wg_ref[...])
    u = _dot(xn, wu_ref[...])
    a = (g * jax.nn.sigmoid(g) * u).astype(BF)
    acc_scr[...] += _dot(a, wd_ref[...])

    @pl.when(f == pl.num_programs(1) - 1)
    def _():
        o_ref[...] = h_ref[...] + 0.5 * _rms(acc_scr[...], post_g_ref[...])


def _ffn(h, pre_g, wgu, wd, post_g, *, tm=512, tf=FFN_TF):
    M, D = h.shape
    Fp = wd.shape[0]
    nf = Fp // tf
    assert wgu.shape == (D, 2 * Fp) and nf * tf == Fp
    return pl.pallas_call(
        _ffn_kernel,
        grid=(M // tm, nf),
        in_specs=[
            pl.BlockSpec((tm, D), lambda i, f: (i, 0)),
            pl.BlockSpec((1, D), lambda i, f: (0, 0)),
            pl.BlockSpec((D, tf), lambda i, f: (0, f)),
            pl.BlockSpec((D, tf), lambda i, f: (0, nf + f)),
            pl.BlockSpec((tf, D), lambda i, f: (f, 0)),
            pl.BlockSpec((1, D), lambda i, f: (0, 0)),
        ],
        out_specs=pl.BlockSpec((tm, D), lambda i, f: (i, 0)),
        out_shape=jax.ShapeDtypeStruct((M, D), F32),
        scratch_shapes=[pltpu.VMEM((tm, D), BF), pltpu.VMEM((tm, D), F32)],
        compiler_params=_cparams("parallel", "arbitrary"),
        name="ffn",
    )(h, pre_g, wgu, wgu, wd, post_g)


def _gelu_tanh(x):
    c = math.sqrt(2.0 / math.pi)
    return x * (0.5 * (1.0 + jnp.tanh(c * (x + 0.044715 * (x * x * x)))))


def _nmm_kernel(h_ref, g_ref, w_ref, o_ref, xn_scr):
    @pl.when(pl.program_id(1) == 0)
    def _():
        xn_scr[...] = _rms(h_ref[...], g_ref[...]).astype(BF)

    o_ref[...] = _dot_nt(xn_scr[...], w_ref[...]).astype(o_ref.dtype)


def _norm_matmul(h, g, w_t, *, out_dtype=F32, tm=1024, tn=1536, name="in_proj"):
    M, D = h.shape
    N = w_t.shape[0]
    return pl.pallas_call(
        _nmm_kernel,
        grid=(M // tm, N // tn),
        in_specs=[
            pl.BlockSpec((tm, D), lambda i, j: (i, 0)),
            pl.BlockSpec((1, D), lambda i, j: (0, 0)),
            pl.BlockSpec((tn, D), lambda i, j: (j, 0)),
        ],
        out_specs=pl.BlockSpec((tm, tn), lambda i, j: (i, j)),
        out_shape=jax.ShapeDtypeStruct((M, N), out_dtype),
        scratch_shapes=[pltpu.VMEM((tm, D), BF)],
        compiler_params=_cparams("parallel", "arbitrary"),
        name=name,
    )(h, g, w_t)


def _prep_kernel(*refs, seq, has_vres):
    if has_vres:
        (ps_ref, prev_ref, mu_ref, w0_ref, w2_ref, a0_ref, a2_ref, g2_ref, kk_ref, ka_ref,
         vf_ref, v0_ref, vw2_ref,
         r_o, k_o, v_o, lw_o, kk_o, a_o, g_o) = refs
    else:
        (ps_ref, prev_ref, mu_ref, w0_ref, w2_ref, a0_ref, a2_ref, g2_ref, kk_ref, ka_ref,
         r_o, k_o, v_o, lw_o, kk_o, a_o, g_o) = refs
    tm = ps_ref.shape[0]
    W = RWKV_WIDTH
    x = ps_ref[...]
    at_seq_start = (pl.program_id(0) * tm) % seq == 0
    last_prev = jnp.where(at_seq_start, 0.0, prev_ref[7:8, :])
    row = lax.broadcasted_iota(jnp.int32, x.shape, 0)
    prev = jnp.where(row == 0, last_prev, pltpu.roll(x, 1, axis=0))
    xs = x + (prev - x) * mu_ref[...]

    xr = xs[:, SEG_R:SEG_R + W]
    xk = xs[:, SEG_K:SEG_K + W]
    xv = xs[:, SEG_V:SEG_V + W]
    xwa = xs[:, SEG_WA:SEG_WA + 128]
    xg = xs[:, SEG_G:SEG_G + 256]

    z = w0_ref[...] + _dot(jnp.tanh(xwa).astype(BF), w2_ref[...])
    w_log = -jax.nn.softplus(-z) - 0.5
    lw_o[...] = -jnp.exp(w_log)
    a = jax.nn.sigmoid(a0_ref[...] + _dot(xwa.astype(BF), a2_ref[...]))
    a_o[...] = a
    g_o[...] = _dot(jax.nn.sigmoid(xg).astype(BF), g2_ref[...])
    kk_o[...] = xk * kk_ref[...]
    k_o[...] = xk * (1.0 + (a - 1.0) * ka_ref[...])
    r_o[...] = xr
    if has_vres:
        pv = xs[:, SEG_P:SEG_P + 128]
        mix = jax.nn.sigmoid(v0_ref[...] + _dot(pv.astype(BF), vw2_ref[...]))
        v_o[...] = xv + (vf_ref[...] - xv) * mix
    else:
        v_o[...] = xv


def _rwkv_prep(ps, mu, w0, w2, a0, a2, g2, k_k, k_a, vres, *, seq, tm=256):
    M = ps.shape[0]
    NA = COL_QKV
    W = RWKV_WIDTH
    row = lambda n: pl.BlockSpec((1, n), lambda i: (0, 0))
    full = lambda a: pl.BlockSpec(a.shape, lambda i: (0, 0))
    tok = pl.BlockSpec((tm, W), lambda i: (i, 0))
    in_specs = [
        pl.BlockSpec((tm, NA), lambda i: (i, 0)),
        pl.BlockSpec((8, NA), lambda i: (jnp.maximum(i * (tm // 8) - 1, 0), 0)),
        row(NA), row(W), full(w2), row(W), full(a2), full(g2), row(W), row(W),
    ]
    args = [ps, ps, mu, w0, w2, a0, a2, g2, k_k, k_a]
    if vres is not None:
        v_first, v0, vw2 = vres
        in_specs += [tok, row(W), full(vw2)]
        args += [v_first, v0, vw2]
    return pl.pallas_call(
        functools.partial(_prep_kernel, seq=seq, has_vres=vres is not None),
        grid=(M // tm,),
        in_specs=in_specs,
        out_specs=[tok] * 7,
        out_shape=[jax.ShapeDtypeStruct((M, W), F32)] * 7,
        compiler_params=_cparams("parallel"),
        name="rwkv_prep",
    )(*args)


def _wkv_kernel(r_ref, k_ref, v_ref, lw_ref, kk_ref, a_ref, g_ref, rk_ref, lnw_ref, lnb_ref,
                y_ref, p_scr):
    C = WKV_CHUNK
    C2 = 2 * C

    @pl.when(pl.program_id(1) == 0)
    def _():
        p_scr[...] = jnp.zeros_like(p_scr)

    lane = lax.broadcasted_iota(jnp.int32, (C, LANES), 1)
    lo = lane < RWKV_HEAD_DIM
    ti = lax.broadcasted_iota(jnp.int32, (C, C), 0)
    tj = lax.broadcasted_iota(jnp.int32, (C, C), 1)
    tri = jnp.where(ti >= tj, 1.0, 0.0).astype(BF)
    si = lax.broadcasted_iota(jnp.int32, (C2, C2), 0)
    sj = lax.broadcasted_iota(jnp.int32, (C2, C2), 1)
    same = (si // C) == (sj // C)
    strict = same & ((si % C) > (sj % C))
    incl = same & ((si % C) >= (sj % C))
    eye = si == sj
    eye_f = jnp.where(eye, 1.0, 0.0)

    def seg_sum(x):
        s_lo = jnp.sum(jnp.where(lo, x, 0.0), axis=1, keepdims=True)
        s_hi = jnp.sum(jnp.where(lo, 0.0, x), axis=1, keepdims=True)
        return jnp.where(lo, s_lo, s_hi)

    def stack(x):
        return jnp.concatenate([jnp.where(lo, x, 0.0), jnp.where(lo, 0.0, x)], axis=0)

    pairs = range(r_ref.shape[1] // LANES)
    cols = [slice(p * LANES, (p + 1) * LANES) for p in pairs]
    each = lambda f, *xs: [f(*t) for t in zip(*xs)]
    bf = lambda xs: [x.astype(BF) for x in xs]

    r = [r_ref[:, cs] for cs in cols]
    k = [k_ref[:, cs] for cs in cols]
    v = [v_ref[:, cs] for cs in cols]
    lw = [lw_ref[:, cs] for cs in cols]
    kkr = [kk_ref[:, cs] for cs in cols]
    a = [a_ref[:, cs] for cs in cols]
    P = [p_scr[p] for p in pairs]

    def cumsum(x):
        hi = x.astype(BF)
        return _dot(tri, hi) + _dot(tri, (x - hi.astype(F32)).astype(BF))

    cum = each(cumsum, lw)
    tot = [c[C - 1:C, :] for c in cum]
    kk = each(lambda x: x / jnp.maximum(jnp.sqrt(seg_sum(x * x)), 1e-12), kkr)
    kka = each(lambda x, y: x * y, kk, a)
    A_st = bf(each(lambda x, c, w: stack(-x * jnp.exp(c - w)), kk, cum, lw))
    R_st = bf(each(lambda x, c: stack(x * jnp.exp(c)), r, cum))
    BK_st = bf(each(lambda x, y, c: jnp.concatenate([stack(x * jnp.exp(-c)), stack(y * jnp.exp(-c))], axis=0),
                    kka, k, cum))
    V_st = bf(each(stack, v))
    BKpT = bf(each(lambda x, y, c, t: jnp.concatenate([stack(x * jnp.exp(t - c)).T,
                                                       stack(y * jnp.exp(t - c)).T], axis=1),
                   kka, k, cum, tot))

    GA = each(_dot_nt, A_st, BK_st)
    GR = each(_dot_nt, R_st, BK_st)
    Aak = bf([jnp.where(strict, x[:, C2:], 0.0) for x in GA])
    Mrb = bf([jnp.where(incl, x[:, :C2], 0.0) for x in GR])
    Mrk = bf([jnp.where(incl, x[:, C2:], 0.0) for x in GR])

    X = [jnp.where(strict, x[:, :C2], 0.0) for x in GA]
    T = [eye_f + x for x in X]
    n = 2
    while n < C:
        Xb = bf(X)
        X = each(_dot, Xb, Xb)
        T = each(lambda t, x: t + _dot(t.astype(BF), x.astype(BF)), T, X)
        n *= 2
    AkV = bf(each(_dot, Aak, V_st))
    WU = each(lambda t, x, y: _dot(t.astype(BF), jnp.concatenate([x, y], axis=1)), T, A_st, AkV)
    Pb = bf(P)
    U = each(lambda wu, pb: _dot(wu[:, :LANES].astype(BF), pb) + wu[:, LANES:], WU, Pb)
    Ub = bf(U)
    gcol = [jnp.exp(jnp.sum(jnp.where(eye, jnp.broadcast_to(t, (C2, LANES)), 0.0), axis=1, keepdims=True))
            for t in tot]
    Pn = each(lambda gc, p_, m, u, vs: gc * p_ + _dot(m, jnp.concatenate([u, vs], axis=0)),
              gcol, P, BKpT, Ub, V_st)
    Y_st = each(lambda rs, pb, mb, u, mk, vs: _dot(rs, pb) + _dot(mb, u) + _dot(mk, vs),
                R_st, Pb, Mrb, Ub, Mrk, V_st)
    y = [x[:C, :] + x[C:, :] for x in Y_st]

    def finish(y_, r_, k_, v_, cs):
        mu = seg_sum(y_) * (1.0 / RWKV_HEAD_DIM)
        yc = y_ - mu
        var = seg_sum(yc * yc) * (1.0 / RWKV_HEAD_DIM)
        yn = yc * lax.rsqrt(var + LNX_EPS) * lnw_ref[:, cs] + lnb_ref[:, cs]
        bonus = seg_sum(r_ * k_ * rk_ref[:, cs]) * v_
        return ((yn + bonus) * g_ref[:, cs]).astype(y_ref.dtype)

    out = each(finish, y, r, k, v, cols)
    for p in pairs:
        p_scr[p] = Pn[p]
    for p in pairs:
        y_ref[:, cols[p]] = out[p]


def _wkv(r, k, v, lw, kkr, a, g, r_k, lnx_w, lnx_b, *, batch, seq):
    M, W = r.shape
    steps = seq // WKV_CHUNK
    tok = pl.BlockSpec((WKV_CHUNK, W), lambda b, c: (b * steps + c, 0))
    par = pl.BlockSpec((1, W), lambda b, c: (0, 0))
    return pl.pallas_call(
        _wkv_kernel,
        grid=(batch, steps),
        in_specs=[tok] * 7 + [par] * 3,
        out_specs=tok,
        out_shape=jax.ShapeDtypeStruct((M, W), BF),
        scratch_shapes=[pltpu.VMEM((W // LANES, LANES, LANES), F32)],
        compiler_params=_cparams("parallel", "arbitrary"),
        name="wkv",
    )(r, k, v, lw, kkr, a, g, r_k, lnx_w, lnx_b)


def _attn_kernel(slopes_ref, q_ref, k_ref, v_ref, o_ref, m_scr, l_scr, acc_scr, *, seq):
    slot = pl.program_id(1)
    grp = pl.program_id(2)
    blk = ATTN_BLK
    scale = ATTN_HEAD_DIM ** -0.5
    qi = lax.broadcasted_iota(jnp.int32, (blk, 2 * blk), 0)
    kj = lax.broadcasted_iota(jnp.int32, (blk, 2 * blk), 1)
    rel = (qi + blk - kj).astype(F32)
    in_band = (kj >= qi) & (kj <= qi + blk)
    is_prev = kj < blk
    nb = ATTN_BLOCKS_PER_ITER
    each = lambda f, *xs: [f(*t) for t in zip(*xs)]

    def rows(start, d):
        if d == 1:
            return pl.ds(start, blk)
        return pl.ds(start, blk, stride=d)

    for gi, (window, d) in enumerate(ATTN_GROUPS):
        assert window // d == blk and seq % window == 0 and (seq // blk) % nb == 0

        @pl.when(grp == gi)
        def _(gi=gi, d=d):
            slope = slopes_ref[gi * ATTN_HEADS_PER_GROUP + slot]
            bias = jnp.where(in_band, -slope * float(d) * rel, NEG_BIG)
            span = blk * d

            def body(it, carry):
                ids = [it * nb + j for j in range(nb)]
                ns = [i // d for i in ids]
                cur = [rows(n * span + i % d, d) for n, i in zip(ns, ids)]
                prv = [rows(jnp.maximum(n - 1, 0) * span + i % d, d) for n, i in zip(ns, ids)]
                q = [q_ref[c, :].astype(BF) for c in cur]
                kcat = [jnp.concatenate([k_ref[p_, :], k_ref[c, :]], axis=0).astype(BF) for p_, c in zip(prv, cur)]
                vcat = [jnp.concatenate([v_ref[p_, :], v_ref[c, :]], axis=0).astype(BF) for p_, c in zip(prv, cur)]
                s = each(lambda q_, k_: _dot_nt(q_, k_) * scale + bias, q, kcat)
                s = each(lambda s_, n: jnp.where(is_prev & (n == 0), NEG_BIG, s_), s, ns)
                m_loc = [jnp.max(s_, axis=1, keepdims=True) for s_ in s]
                p = each(lambda s_, m_: jnp.exp(s_ - m_), s, m_loc)
                l_loc = [jnp.sum(p_, axis=1, keepdims=True) for p_ in p]
                o_loc = each(lambda p_, v_: _dot(p_.astype(BF), v_), p, vcat)
                if gi == 0:
                    m_new, l_new, acc = m_loc, l_loc, o_loc
                else:
                    m_old = [m_scr[c, :][:, :1] for c in cur]
                    l_old = [l_scr[c, :][:, :1] for c in cur]
                    a_old = [acc_scr[c, :] for c in cur]
                    m_new = each(jnp.maximum, m_old, m_loc)
                    alpha = each(lambda mo, mn: jnp.exp(mo - mn), m_old, m_new)
                    beta = each(lambda ml, mn: jnp.exp(ml - mn), m_loc, m_new)
                    l_new = each(lambda al, lo_, be, ll: al * lo_ + be * ll, alpha, l_old, beta, l_loc)
                    acc = each(lambda al, ao, be, ol: al * ao + be * ol, alpha, a_old, beta, o_loc)
                for c, m_, l_, a_ in zip(cur, m_new, l_new, acc):
                    m_scr[c, :] = jnp.broadcast_to(m_, (blk, LANES))
                    l_scr[c, :] = jnp.broadcast_to(l_, (blk, LANES))
                    acc_scr[c, :] = a_
                return carry

            lax.fori_loop(0, seq // blk // nb, body, 0)

    @pl.when(grp == len(ATTN_GROUPS) - 1)
    def _():
        o_ref[...] = (acc_scr[...] / l_scr[...]).astype(o_ref.dtype)


def _attention(qkv, slopes, *, batch, seq):
    M = qkv.shape[0]
    E = ATTN_HEAD_DIM
    hpg = ATTN_HEADS_PER_GROUP

    def spec(which):
        first = COL_QKV // E + which * ATTN_HEADS
        return pl.BlockSpec((seq, E), lambda b, s, g: (b, first + g * hpg + s))

    return pl.pallas_call(
        functools.partial(_attn_kernel, seq=seq),
        grid=(batch, hpg, len(ATTN_GROUPS)),
        in_specs=[pl.BlockSpec(memory_space=pltpu.SMEM), spec(0), spec(1), spec(2)],
        out_specs=pl.BlockSpec((seq, E), lambda b, s, g: (b, s)),
        out_shape=jax.ShapeDtypeStruct((M, hpg * E), BF),
        scratch_shapes=[pltpu.VMEM((seq, LANES), F32)] * 3,
        compiler_params=_cparams("parallel", "parallel", "arbitrary"),
        name="dilated_attn",
    )(slopes, qkv, qkv, qkv)


def _sgu_kernel(x_ref, lg_ref, lb_ref, ws_ref, bt_ref, o_ref):
    tm = x_ref.shape[0]
    CH = SGU_CHUNK
    GD = SGU_WIDTH // SGU_GROUPS
    u = _gelu_tanh(x_ref[:, :SGU_WIDTH])
    v = _gelu_tanh(x_ref[:, SGU_WIDTH:])
    mu = jnp.mean(v, axis=-1, keepdims=True)
    vc = v - mu
    var = jnp.mean(vc * vc, axis=-1, keepdims=True)
    vn = (vc * lax.rsqrt(var + LN_EPS) * lg_ref[...] + lb_ref[...]).astype(BF)
    ti = lax.broadcasted_iota(jnp.int32, (CH, CH), 0)
    si = lax.broadcasted_iota(jnp.int32, (CH, CH), 1)
    causal = ti >= si
    for gidx in range(SGU_GROUPS):
        w = jnp.where(causal, ws_ref[gidx], 0.0).astype(BF)
        bias = bt_ref[:, gidx:gidx + 1]
        cs = slice(gidx * GD, (gidx + 1) * GD)
        for c in range(tm // CH):
            rs = slice(c * CH, (c + 1) * CH)
            mixed = _dot(w, vn[rs, cs]) + bias
            o_ref[rs, cs] = (u[rs, cs] * mixed).astype(o_ref.dtype)


def _sgu(proj, ln_g, ln_b, w_s, b_t, *, tm=512):
    M = proj.shape[0]
    return pl.pallas_call(
        _sgu_kernel,
        grid=(M // tm,),
        in_specs=[
            pl.BlockSpec((tm, N_SGU), lambda i: (i, COL_SGU // N_SGU)),
            pl.BlockSpec((1, SGU_WIDTH), lambda i: (0, 0)),
            pl.BlockSpec((1, SGU_WIDTH), lambda i: (0, 0)),
            pl.BlockSpec(w_s.shape, lambda i: (0, 0, 0)),
            pl.BlockSpec(b_t.shape, lambda i: (0, 0)),
        ],
        out_specs=pl.BlockSpec((tm, SGU_WIDTH), lambda i: (i, 0)),
        out_shape=jax.ShapeDtypeStruct((M, SGU_WIDTH), BF),
        compiler_params=_cparams("parallel"),
        name="sgu",
    )(proj, ln_g, ln_b, w_s, b_t)


def _merge_kernel(h_ref, yr_ref, ya_ref, ys_ref, g0_ref, g1_ref, g2_ref, wr_ref, wa_ref, ws_ref,
                  wo_ref, pg_ref, o_ref, acc_scr):
    j = pl.program_id(1)

    @pl.when(j == 0)
    def _():
        acc_scr[...] = jnp.zeros_like(acc_scr)

    merged = (jax.nn.sigmoid(g0_ref[...]) * _dot(yr_ref[...], wr_ref[...])
              + jax.nn.sigmoid(g1_ref[...]) * _dot(ya_ref[...], wa_ref[...])
              + jax.nn.sigmoid(g2_ref[...]) * _dot(ys_ref[...], ws_ref[...]))
    acc_scr[...] += _dot(merged.astype(BF), wo_ref[...])

    @pl.when(j == pl.num_programs(1) - 1)
    def _():
        o_ref[...] = h_ref[...] + _rms(acc_scr[...], pg_ref[...])


def _merge(h, yr, ya, ys, gates, wr, wa, ws, wo, post_g, *, tm=512, tn=512):
    M, D = h.shape
    nj = D // tn
    act = lambda a: pl.BlockSpec((tm, a.shape[1]), lambda i, j: (i, 0))
    gate = lambda br: pl.BlockSpec((tm, tn), lambda i, j: (i, COL_GATE // tn + br * nj + j))
    wcol = lambda a: pl.BlockSpec((a.shape[0], tn), lambda i, j: (0, j))
    return pl.pallas_call(
        _merge_kernel,
        grid=(M // tm, nj),
        in_specs=[
            pl.BlockSpec((tm, D), lambda i, j: (i, 0)),
            act(yr), act(ya), act(ys), gate(0), gate(1), gate(2),
            wcol(wr), wcol(wa), wcol(ws),
            pl.BlockSpec((tn, D), lambda i, j: (j, 0)),
            pl.BlockSpec((1, D), lambda i, j: (0, 0)),
        ],
        out_specs=pl.BlockSpec((tm, D), lambda i, j: (i, 0)),
        out_shape=jax.ShapeDtypeStruct((M, D), F32),
        scratch_shapes=[pltpu.VMEM((tm, D), F32)],
        compiler_params=_cparams("parallel", "arbitrary"),
        name="merge_out",
    )(h, yr, ya, ys, gates, gates, gates, wr, wa, ws, wo, post_g)


def _alibi_slopes(n_heads):
    def geometric(n):
        start = 2.0 ** (-8.0 / n)
        return [start ** (i + 1) for i in range(n)]
    closest = 2 ** int(math.floor(math.log2(n_heads)))
    slopes = geometric(closest)
    if closest < n_heads:
        slopes += geometric(2 * closest)[0::2][: n_heads - closest]
    return np.array(sorted(slopes, reverse=True), dtype=np.float32)


def _pad_rows(x, before, total):
    return jnp.pad(x, ((before, total - before - x.shape[0]), (0, 0)))


def _cast_kernel(src_ref, o_ref, *, axis, n_valid):
    valid = pl.program_id(axis) < n_valid

    @pl.when(valid)
    def _():
        o_ref[...] = src_ref[...].astype(BF)

    @pl.when(jnp.logical_not(valid))
    def _():
        o_ref[...] = jnp.zeros_like(o_ref)


def _cast_rows(w, l, *, rows, n_out):
    _, R, C = w.shape
    n_src = R // rows
    assert n_src * rows == R and n_out >= n_src
    return pl.pallas_call(
        functools.partial(_cast_kernel, axis=0, n_valid=n_src),
        grid=(n_out,),
        in_specs=[pl.BlockSpec((None, rows, C), lambda j: (l, jnp.minimum(j, n_src - 1), 0))],
        out_specs=pl.BlockSpec((rows, C), lambda j: (j, 0)),
        out_shape=jax.ShapeDtypeStruct((n_out * rows, C), BF),
        compiler_params=_cparams("parallel"),
        name="cast_rows",
    )(w)


def _gate_up_kernel(src_ref, o_ref):
    F = src_ref.shape[1] // 2
    Fp = o_ref.shape[1] // 2
    for half in range(2):
        o_ref[:, half * Fp:half * Fp + F] = src_ref[:, half * F:(half + 1) * F].astype(BF)
        if Fp > F:
            o_ref[:, half * Fp + F:(half + 1) * Fp] = jnp.zeros((o_ref.shape[0], Fp - F), BF)


def _cast_gate_up(w_gu, l, *, n_out, rows=128):
    _, D, F2 = w_gu.shape
    assert F2 % (2 * LANES) == 0 and n_out * LANES >= F2 // 2 and D % rows == 0
    return pl.pallas_call(
        _gate_up_kernel,
        grid=(D // rows,),
        in_specs=[pl.BlockSpec((None, rows, F2), lambda i: (l, i, 0))],
        out_specs=pl.BlockSpec((rows, 2 * n_out * LANES), lambda i: (i, 0)),
        out_shape=jax.ShapeDtypeStruct((D, 2 * n_out * LANES), BF),
        compiler_params=_cparams("parallel"),
        name="cast_gate_up",
    )(w_gu)


def _win_kernel(src_ref, x_ref, o_ref, *, rows):
    o_ref[...] = src_ref[...].astype(BF)

    @pl.when(pl.program_id(0) == SEG_P // rows)
    def _():
        o_ref[pl.ds(SEG_P % rows, LANES), :] = x_ref[...].astype(BF)


def _cast_w_in_t(w_in_t, extra_t, l, *, rows=512):
    _, n_in, D = w_in_t.shape
    assert n_in - N_SHIFT == N_PROJ - COL_QKV
    assert COL_QKV % rows == 0 and N_PROJ % rows == 0 and SEG_P % rows + LANES <= rows
    n_head = COL_QKV // rows
    sub = 8
    assert rows % sub == 0 and (COL_QKV - N_SHIFT) % sub == 0

    def src_idx(j):
        back = jnp.where(j < n_head, 0, (COL_QKV - N_SHIFT) // sub)
        return (l, (j * (rows // sub) - back) * sub, 0)

    return pl.pallas_call(
        functools.partial(_win_kernel, rows=rows),
        grid=(N_PROJ // rows,),
        in_specs=[pl.BlockSpec((None, pl.Element(rows), pl.Element(D)), src_idx),
                  pl.BlockSpec((LANES, D), lambda j: (0, 0))],
        out_specs=pl.BlockSpec((rows, D), lambda j: (j, 0)),
        out_shape=jax.ShapeDtypeStruct((N_PROJ, D), BF),
        compiler_params=_cparams("parallel"),
        name="cast_w_in",
    )(w_in_t, extra_t)


def kernel(x, ffn1_pre_g, ffn1_w_gu, ffn1_w_down, ffn1_post_g, mix_pre_g, w_in, shift_mu, decay_w0, decay_w2, iclr_a0, iclr_a2, gate_g2, k_k, k_a, r_k, lnx_w, lnx_b, vres_w1, vres_mu, vres_v0, vres_w2, sgu_ln_g, sgu_ln_b, sgu_w_s, sgu_b, w_b_rwkv, w_b_attn, w_b_sgu, w_out, mix_post_g, ffn2_pre_g, ffn2_w_gu, ffn2_w_down, ffn2_post_g):
    B, T, D = x.shape
    depth = w_in.shape[0]
    F = ffn1_w_down.shape[1]
    f_blocks = -(-F // FFN_TF) * FFN_TF // LANES
    slopes = jnp.asarray(_alibi_slopes(ATTN_HEADS))
    row = lambda p: p.reshape(1, -1)

    def ffn(h, pre_g, w_gu, w_down, post_g, l):
        wgu = _cast_gate_up(w_gu, l, n_out=f_blocks)
        wd = _cast_rows(w_down, l, rows=LANES, n_out=f_blocks)
        return _ffn(h, row(pre_g[l]), wgu, wd, row(post_g[l]))

    w_in_t = jnp.swapaxes(w_in, 1, 2)
    h = x.reshape(B * T, D)
    v_first = None
    for l in range(depth):
        h = ffn(h, ffn1_pre_g, ffn1_w_gu, ffn1_w_down, ffn1_post_g, l)

        if l == 0:
            extra_t = jnp.zeros((LANES, D), F32)
            extra_mu = jnp.zeros((LANES,), F32)
        else:
            extra_t = _pad_rows(vres_w1[l - 1].T, 0, LANES)
            extra_mu = jnp.pad(vres_mu[l - 1], (0, LANES - VRES_LORA))
        mu = jnp.concatenate([jnp.pad(shift_mu[l], (0, SEG_P - N_SHIFT)), extra_mu]).reshape(1, -1)
        proj = _norm_matmul(h, row(mix_pre_g[l]), _cast_w_in_t(w_in_t, extra_t, l))

        vres = None
        if l > 0:
            vres = (v_first, row(vres_v0[l - 1]), _pad_rows(vres_w2[l - 1], 0, LANES).astype(BF))
        r, k, v, lw, kkr, a, g = _rwkv_prep(
            proj, mu, row(decay_w0[l]), _pad_rows(decay_w2[l], 0, LANES).astype(BF),
            row(iclr_a0[l]), _pad_rows(iclr_a2[l], DECAY_LORA, LANES).astype(BF),
            _pad_rows(gate_g2[l], 0, 2 * LANES).astype(BF), row(k_k[l]), row(k_a[l]), vres, seq=T)
        if l == 0:
            v_first = v
        y_rwkv = _wkv(r, k, v, lw, kkr, a, g, row(r_k[l]), row(lnx_w[l]), row(lnx_b[l]),
                      batch=B, seq=T)

        y_attn = _attention(proj, slopes, batch=B, seq=T)
        y_sgu = _sgu(proj, row(sgu_ln_g[l]), row(sgu_ln_b[l]), sgu_w_s[l], sgu_b[l].T)

        cast = lambda w: _cast_rows(w, l, rows=2 * LANES, n_out=w.shape[1] // (2 * LANES))
        h = _merge(h, y_rwkv, y_attn, y_sgu, proj, cast(w_b_rwkv), cast(w_b_attn), cast(w_b_sgu),
                   cast(w_out), row(mix_post_g[l]))

        h = ffn(h, ffn2_pre_g, ffn2_w_gu, ffn2_w_down, ffn2_post_g, l)
    return h.reshape(B, T, D)
```

```python
import functools
import math

import numpy as np
import jax
import jax.numpy as jnp
from jax import lax
from jax.experimental import pallas as pl
from jax.experimental.pallas import tpu as pltpu

BF = jnp.bfloat16
F32 = jnp.float32

D_MODEL = 2048
RWKV_HEAD_DIM = 64
RWKV_WIDTH = D_MODEL // 2
DECAY_LORA = 64
ICLR_LORA = 64
VRES_LORA = 32
GATE_LORA = 160
LNX_EPS = 64e-5
ATTN_GROUPS = ((128, 1), (512, 4), (2048, 16))
ATTN_HEADS_PER_GROUP = 4
ATTN_HEAD_DIM = 128
ATTN_HEADS = len(ATTN_GROUPS) * ATTN_HEADS_PER_GROUP
ATTN_BLK = 128
ATTN_BLOCKS_PER_ITER = 4
SGU_CHUNK = 128
SGU_GROUPS = 4
SGU_WIDTH = D_MODEL // 4
RMS_EPS = 1e-6
LN_EPS = 1e-5
N_BRANCHES = 3

LANES = 128
WKV_CHUNK = 64
FFN_TF = 512
NEG_BIG = -1e30

N_SHIFT = 3 * RWKV_WIDTH + DECAY_LORA + ICLR_LORA + GATE_LORA
SEG_R, SEG_K, SEG_V = 0, RWKV_WIDTH, 2 * RWKV_WIDTH
SEG_WA = 3 * RWKV_WIDTH
SEG_G = SEG_WA + 128
SEG_P = SEG_G + 256
N_QKV = 3 * ATTN_HEADS * ATTN_HEAD_DIM
N_SGU = 2 * SGU_WIDTH
N_GATE = N_BRANCHES * D_MODEL
COL_SHIFT = 0
COL_QKV = SEG_P + LANES
COL_SGU = COL_QKV + N_QKV
COL_GATE = COL_SGU + N_SGU
N_PROJ = COL_GATE + N_GATE

VMEM_LIMIT = 56 * 1024 * 1024


def _cparams(*sem):
    return pltpu.CompilerParams(dimension_semantics=sem, vmem_limit_bytes=VMEM_LIMIT)


def _dot(a, b):
    return jnp.dot(a, b, preferred_element_type=F32)


def _dot_nt(a, b):
    return lax.dot_general(a, b, (((1,), (1,)), ((), ())), preferred_element_type=F32)


def _rms(x, g):
    return x * lax.rsqrt(jnp.mean(x * x, axis=-1, keepdims=True) + RMS_EPS) * g


def _ffn_kernel(h_ref, pre_g_ref, wg_ref, wu_ref, wd_ref, post_g_ref, o_ref, xn_scr, acc_scr):
    f = pl.program_id(1)

    @pl.when(f == 0)
    def _():
        xn_scr[...] = _rms(h_ref[...], pre_g_ref[...]).astype(BF)
        acc_scr[...] = jnp.zeros_like(acc_scr)

    xn = xn_scr[...]
    g = _dot(xn, wg_ref[...])
    u = _dot(xn, wu_ref[...])
    a = (g * jax.nn.sigmoid(g) * u).astype(BF)
    acc_scr[...] += _dot(a, wd_ref[...])

    @pl.when(f == pl.num_programs(1) - 1)
    def _():
        o_ref[...] = h_ref[...] + 0.5 * _rms(acc_scr[...], post_g_ref[...])


def _ffn(h, pre_g, wgu, wd, post_g, *, tm=512, tf=FFN_TF):
    M, D = h.shape
    Fp = wd.shape[0]
    nf = Fp // tf
    assert wgu.shape == (D, 2 * Fp) and nf * tf == Fp
    return pl.pallas_call(
        _ffn_kernel,
        grid=(M // tm, nf),
        in_specs=[
            pl.BlockSpec((tm, D), lambda i, f: (i, 0)),
            pl.BlockSpec((1, D), lambda i, f: (0, 0)),
            pl.BlockSpec((D, tf), lambda i, f: (0, f)),
            pl.BlockSpec((D, tf), lambda i, f: (0, nf + f)),
            pl.BlockSpec((tf, D), lambda i, f: (f, 0)),
            pl.BlockSpec((1, D), lambda i, f: (0, 0)),
        ],
        out_specs=pl.BlockSpec((tm, D), lambda i, f: (i, 0)),
        out_shape=jax.ShapeDtypeStruct((M, D), F32),
        scratch_shapes=[pltpu.VMEM((tm, D), BF), pltpu.VMEM((tm, D), F32)],
        compiler_params=_cparams("parallel", "arbitrary"),
        name="ffn",
    )(h, pre_g, wgu, wgu, wd, post_g)


def _gelu_tanh(x):
    c = math.sqrt(2.0 / math.pi)
    return x * (0.5 * (1.0 + jnp.tanh(c * (x + 0.044715 * (x * x * x)))))


def _nmm_kernel(h_ref, g_ref, w_ref, o_ref, xn_scr):
    @pl.when(pl.program_id(1) == 0)
    def _():
        xn_scr[...] = _rms(h_ref[...], g_ref[...]).astype(BF)

    o_ref[...] = _dot_nt(xn_scr[...], w_ref[...]).astype(o_ref.dtype)


def _norm_matmul(h, g, w_t, *, out_dtype=F32, tm=1024, tn=1536, name="in_proj"):
    M, D = h.shape
    N = w_t.shape[0]
    return pl.pallas_call(
        _nmm_kernel,
        grid=(M // tm, N // tn),
        in_specs=[
            pl.BlockSpec((tm, D), lambda i, j: (i, 0)),
            pl.BlockSpec((1, D), lambda i, j: (0, 0)),
            pl.BlockSpec((tn, D), lambda i, j: (j, 0)),
        ],
        out_specs=pl.BlockSpec((tm, tn), lambda i, j: (i, j)),
        out_shape=jax.ShapeDtypeStruct((M, N), out_dtype),
        scratch_shapes=[pltpu.VMEM((tm, D), BF)],
        compiler_params=_cparams("parallel", "arbitrary"),
        name=name,
    )(h, g, w_t)


def _prep_kernel(*refs, seq, has_vres):
    if has_vres:
        (ps_ref, prev_ref, mu_ref, w0_ref, w2_ref, a0_ref, a2_ref, g2_ref, kk_ref, ka_ref,
         vf_ref, v0_ref, vw2_ref,
         r_o, k_o, v_o, lw_o, kk_o, a_o, g_o) = refs
    else:
        (ps_ref, prev_ref, mu_ref, w0_ref, w2_ref, a0_ref, a2_ref, g2_ref, kk_ref, ka_ref,
         r_o, k_o, v_o, lw_o, kk_o, a_o, g_o) = refs
    tm = ps_ref.shape[0]
    W = RWKV_WIDTH
    x = ps_ref[...]
    at_seq_start = (pl.program_id(0) * tm) % seq == 0
    last_prev = jnp.where(at_seq_start, 0.0, prev_ref[7:8, :])
    row = lax.broadcasted_iota(jnp.int32, x.shape, 0)
    prev = jnp.where(row == 0, last_prev, pltpu.roll(x, 1, axis=0))
    xs = x + (prev - x) * mu_ref[...]

    xr = xs[:, SEG_R:SEG_R + W]
    xk = xs[:, SEG_K:SEG_K + W]
    xv = xs[:, SEG_V:SEG_V + W]
    xwa = xs[:, SEG_WA:SEG_WA + 128]
    xg = xs[:, SEG_G:SEG_G + 256]

    z = w0_ref[...] + _dot(jnp.tanh(xwa).astype(BF), w2_ref[...])
    w_log = -jax.nn.softplus(-z) - 0.5
    lw_o[...] = -jnp.exp(w_log)
    a = jax.nn.sigmoid(a0_ref[...] + _dot(xwa.astype(BF), a2_ref[...]))
    a_o[...] = a
    g_o[...] = _dot(jax.nn.sigmoid(xg).astype(BF), g2_ref[...])
    kk_o[...] = xk * kk_ref[...]
    k_o[...] = xk * (1.0 + (a - 1.0) * ka_ref[...])
    r_o[...] = xr
    if has_vres:
        pv = xs[:, SEG_P:SEG_P + 128]
        mix = jax.nn.sigmoid(v0_ref[...] + _dot(pv.astype(BF), vw2_ref[...]))
        v_o[...] = xv + (vf_ref[...] - xv) * mix
    else:
        v_o[...] = xv


def _rwkv_prep(ps, mu, w0, w2, a0, a2, g2, k_k, k_a, vres, *, seq, tm=256):
    M = ps.shape[0]
    NA = COL_QKV
    W = RWKV_WIDTH
    row = lambda n: pl.BlockSpec((1, n), lambda i: (0, 0))
    full = lambda a: pl.BlockSpec(a.shape, lambda i: (0, 0))
    tok = pl.BlockSpec((tm, W), lambda i: (i, 0))
    in_specs = [
        pl.BlockSpec((tm, NA), lambda i: (i, 0)),
        pl.BlockSpec((8, NA), lambda i: (jnp.maximum(i * (tm // 8) - 1, 0), 0)),
        row(NA), row(W), full(w2), row(W), full(a2), full(g2), row(W), row(W),
    ]
    args = [ps, ps, mu, w0, w2, a0, a2, g2, k_k, k_a]
    if vres is not None:
        v_first, v0, vw2 = vres
        in_specs += [tok, row(W), full(vw2)]
        args += [v_first, v0, vw2]
    return pl.pallas_call(
        functools.partial(_prep_kernel, seq=seq, has_vres=vres is not None),
        grid=(M // tm,),
        in_specs=in_specs,
        out_specs=[tok] * 7,
        out_shape=[jax.ShapeDtypeStruct((M, W), F32)] * 7,
        compiler_params=_cparams("parallel"),
        name="rwkv_prep",
    )(*args)


def _wkv_kernel(r_ref, k_ref, v_ref, lw_ref, kk_ref, a_ref, g_ref, rk_ref, lnw_ref, lnb_ref,
                y_ref, p_scr):
    C = WKV_CHUNK
    C2 = 2 * C

    @pl.when(pl.program_id(1) == 0)
    def _():
        p_scr[...] = jnp.zeros_like(p_scr)

    lane = lax.broadcasted_iota(jnp.int32, (C, LANES), 1)
    lo = lane < RWKV_HEAD_DIM
    ti = lax.broadcasted_iota(jnp.int32, (C, C), 0)
    tj = lax.broadcasted_iota(jnp.int32, (C, C), 1)
    tri = jnp.where(ti >= tj, 1.0, 0.0).astype(BF)
    tq = lax.broadcasted_iota(jnp.int32, (C, LANES), 0)
    sq = lane % C
    strict = tq > sq
    incl = tq >= sq
    eye_c = jnp.where(tq == sq, 1.0, 0.0)
    si = lax.broadcasted_iota(jnp.int32, (C2, LANES), 0)
    sj = lax.broadcasted_iota(jnp.int32, (C2, LANES), 1)
    eye = si == sj

    def seg_sum(x):
        s_lo = jnp.sum(jnp.where(lo, x, 0.0), axis=1, keepdims=True)
        s_hi = jnp.sum(jnp.where(lo, 0.0, x), axis=1, keepdims=True)
        return jnp.where(lo, s_lo, s_hi)

    def stack(x):
        return jnp.concatenate([jnp.where(lo, x, 0.0), jnp.where(lo, 0.0, x)], axis=0)

    def bstack(x):
        return stack(x).astype(BF)

    n_pairs = r_ref.shape[1] // LANES
    n_chunks = r_ref.shape[0] // C
    items = [(c, p) for c in range(n_chunks) for p in range(n_pairs)]
    rows = [slice(c * C, (c + 1) * C) for c, _ in items]
    cols = [slice(p * LANES, (p + 1) * LANES) for _, p in items]
    each = lambda f, *xs: [f(*t) for t in zip(*xs)]
    bf = lambda xs: [x.astype(BF) for x in xs]
    load = lambda ref: [ref[rs, cs] for rs, cs in zip(rows, cols)]

    r, k, v, lw, kkr, a = (load(ref) for ref in (r_ref, k_ref, v_ref, lw_ref, kk_ref, a_ref))

    def cumsum(x):
        hi = x.astype(BF)
        both = _dot(tri, jnp.concatenate([hi, (x - hi.astype(F32)).astype(BF)], axis=1))
        return both[:, :LANES] + both[:, LANES:]

    cum = each(cumsum, lw)
    tot = [c[C - 1:C, :] for c in cum]
    kk = each(lambda x: x / jnp.maximum(jnp.sqrt(seg_sum(x * x)), 1e-12), kkr)
    kka = each(lambda x, y: x * y, kk, a)
    A_c = each(lambda x, c, w: -x * jnp.exp(c - w), kk, cum, lw)
    R_c = bf(each(lambda x, c: x * jnp.exp(c), r, cum))
    AR = each(lambda a_, r_: jnp.concatenate([a_.astype(BF), r_], axis=0), A_c, R_c)
    BK_st = each(lambda x, y, c: jnp.concatenate([bstack(x * jnp.exp(-c)), bstack(y * jnp.exp(-c))], axis=0),
                 kka, k, cum)
    V_st = each(bstack, v)
    BKpT = bf(each(lambda x, y, c, t: jnp.concatenate([stack(x * jnp.exp(t - c)).T,
                                                       stack(y * jnp.exp(t - c)).T], axis=1),
                   kka, k, cum, tot))
    gcol = [jnp.exp(jnp.sum(jnp.where(eye, jnp.broadcast_to(t, (C2, LANES)), 0.0), axis=1, keepdims=True))
            for t in tot]

    G = each(_dot_nt, AR, BK_st)
    Aab = [jnp.where(strict, x[:C, :LANES], 0.0) for x in G]
    Aak = bf([jnp.where(strict, x[:C, LANES:], 0.0) for x in G])
    Mrb = bf([jnp.where(incl, x[C:, :LANES], 0.0) for x in G])
    Mrk = bf([jnp.where(incl, x[C:, LANES:], 0.0) for x in G])
    AkV = each(_dot, Aak, V_st)

    S = [eye_c + x for x in Aab]
    Pw = each(lambda x: _dot(x.astype(BF), bstack(x)), Aab)
    n = 2
    while 2 * n < C:
        prod = each(lambda p_, s_: _dot(p_.astype(BF), jnp.concatenate([bstack(p_), bstack(s_)], axis=1)),
                    Pw, S)
        S = each(lambda s_, pr: s_ + pr[:, LANES:], S, prod)
        Pw = [pr[:, :LANES] for pr in prod]
        n *= 2
    S = each(lambda s_, p_: s_ + _dot(p_.astype(BF), bstack(s_)), S, Pw)
    WU = each(lambda s_, a_, u0: _dot(s_.astype(BF), jnp.concatenate([bstack(a_), bstack(u0)], axis=1)),
              S, A_c, AkV)

    def finish(y_, r_, k_, v_, rs, cs):
        mu = seg_sum(y_) * (1.0 / RWKV_HEAD_DIM)
        yc = y_ - mu
        var = seg_sum(yc * yc) * (1.0 / RWKV_HEAD_DIM)
        yn = yc * lax.rsqrt(var + LNX_EPS) * lnw_ref[:, cs] + lnb_ref[:, cs]
        bonus = seg_sum(r_ * k_ * rk_ref[:, cs]) * v_
        return ((yn + bonus) * g_ref[rs, cs]).astype(y_ref.dtype)

    P = [p_scr[p] for p in range(n_pairs)]
    for c in range(n_chunks):
        sel = lambda xs: xs[c * n_pairs:(c + 1) * n_pairs]
        Pb = bf(P)
        U = each(lambda wu, pb: _dot(wu[:, :LANES].astype(BF), pb) + wu[:, LANES:], sel(WU), Pb)
        U_st = each(bstack, U)
        y = each(lambda r_, mb, mk, pb, u, vs: _dot(jnp.concatenate([r_, mb, mk], axis=1),
                                                    jnp.concatenate([pb, u, vs], axis=0)),
                 sel(R_c), sel(Mrb), sel(Mrk), Pb, U_st, sel(V_st))
        P = each(lambda gc, p_, m, u, vs: gc * p_ + _dot(m, jnp.concatenate([u, vs], axis=0)),
                 sel(gcol), P, sel(BKpT), U_st, sel(V_st))
        out = each(finish, y, sel(r), sel(k), sel(v), sel(rows), sel(cols))
        for o, rs, cs in zip(out, sel(rows), sel(cols)):
            y_ref[rs, cs] = o
    for p in range(n_pairs):
        p_scr[p] = P[p]


def _wkv(r, k, v, lw, kkr, a, g, r_k, lnx_w, lnx_b, *, batch, seq, chunks_per_step=2):
    M, W = r.shape
    rows = chunks_per_step * WKV_CHUNK
    steps = seq // rows
    tok = pl.BlockSpec((rows, W), lambda b, c: (b * steps + c, 0))
    par = pl.BlockSpec((1, W), lambda b, c: (0, 0))
    return pl.pallas_call(
        _wkv_kernel,
        grid=(batch, steps),
        in_specs=[tok] * 7 + [par] * 3,
        out_specs=tok,
        out_shape=jax.ShapeDtypeStruct((M, W), BF),
        scratch_shapes=[pltpu.VMEM((W // LANES, LANES, LANES), F32)],
        compiler_params=_cparams("parallel", "arbitrary"),
        name="wkv",
    )(r, k, v, lw, kkr, a, g, r_k, lnx_w, lnx_b)


def _attn_kernel(slopes_ref, q_ref, k_ref, v_ref, o_ref, m_scr, l_scr, acc_scr, *, seq):
    slot = pl.program_id(1)
    grp = pl.program_id(2)
    blk = ATTN_BLK
    scale = ATTN_HEAD_DIM ** -0.5
    qi = lax.broadcasted_iota(jnp.int32, (blk, 2 * blk), 0)
    kj = lax.broadcasted_iota(jnp.int32, (blk, 2 * blk), 1)
    rel = (qi + blk - kj).astype(F32)
    in_band = (kj >= qi) & (kj <= qi + blk)
    is_prev = kj < blk
    nb = ATTN_BLOCKS_PER_ITER
    each = lambda f, *xs: [f(*t) for t in zip(*xs)]

    def rows(start, d):
        if d == 1:
            return pl.ds(start, blk)
        return pl.ds(start, blk, stride=d)

    for gi, (window, d) in enumerate(ATTN_GROUPS):
        assert window // d == blk and seq % window == 0 and (seq // blk) % nb == 0

        @pl.when(grp == gi)
        def _(gi=gi, d=d):
            slope = slopes_ref[gi * ATTN_HEADS_PER_GROUP + slot]
            bias = jnp.where(in_band, -slope * float(d) * rel, NEG_BIG)
            span = blk * d

            def body(it, carry):
                ids = [it * nb + j for j in range(nb)]
                ns = [i // d for i in ids]
                cur = [rows(n * span + i % d, d) for n, i in zip(ns, ids)]
                prv = [rows(jnp.maximum(n - 1, 0) * span + i % d, d) for n, i in zip(ns, ids)]
                q = [q_ref[c, :].astype(BF) for c in cur]
                kcat = [jnp.concatenate([k_ref[p_, :], k_ref[c, :]], axis=0).astype(BF) for p_, c in zip(prv, cur)]
                vcat = [jnp.concatenate([v_ref[p_, :], v_ref[c, :]], axis=0).astype(BF) for p_, c in zip(prv, cur)]
                s = each(lambda q_, k_: _dot_nt(q_, k_) * scale + bias, q, kcat)
                s = each(lambda s_, n: jnp.where(is_prev & (n == 0), NEG_BIG, s_), s, ns)
                m_loc = [jnp.max(s_, axis=1, keepdims=True) for s_ in s]
                p = each(lambda s_, m_: jnp.exp(s_ - m_), s, m_loc)
                l_loc = [jnp.sum(p_, axis=1, keepdims=True) for p_ in p]
                o_loc = each(lambda p_, v_: _dot(p_.astype(BF), v_), p, vcat)
                if gi == 0:
                    m_new, l_new, acc = m_loc, l_loc, o_loc
                else:
                    m_old = [m_scr[c, :][:, :1] for c in cur]
                    l_old = [l_scr[c, :][:, :1] for c in cur]
                    a_old = [acc_scr[c, :] for c in cur]
                    m_new = each(jnp.maximum, m_old, m_loc)
                    alpha = each(lambda mo, mn: jnp.exp(mo - mn), m_old, m_new)
                    beta = each(lambda ml, mn: jnp.exp(ml - mn), m_loc, m_new)
                    l_new = each(lambda al, lo_, be, ll: al * lo_ + be * ll, alpha, l_old, beta, l_loc)
                    acc = each(lambda al, ao, be, ol: al * ao + be * ol, alpha, a_old, beta, o_loc)
                for c, m_, l_, a_ in zip(cur, m_new, l_new, acc):
                    m_scr[c, :] = jnp.broadcast_to(m_, (blk, LANES))
                    l_scr[c, :] = jnp.broadcast_to(l_, (blk, LANES))
                    acc_scr[c, :] = a_
                return carry

            lax.fori_loop(0, seq // blk // nb, body, 0)

    @pl.when(grp == len(ATTN_GROUPS) - 1)
    def _():
        o_ref[...] = (acc_scr[...] / l_scr[...]).astype(o_ref.dtype)


def _attention(qkv, slopes, *, batch, seq):
    M = qkv.shape[0]
    E = ATTN_HEAD_DIM
    hpg = ATTN_HEADS_PER_GROUP

    def spec(which):
        first = COL_QKV // E + which * ATTN_HEADS
        return pl.BlockSpec((seq, E), lambda b, s, g: (b, first + g * hpg + s))

    return pl.pallas_call(
        functools.partial(_attn_kernel, seq=seq),
        grid=(batch, hpg, len(ATTN_GROUPS)),
        in_specs=[pl.BlockSpec(memory_space=pltpu.SMEM), spec(0), spec(1), spec(2)],
        out_specs=pl.BlockSpec((seq, E), lambda b, s, g: (b, s)),
        out_shape=jax.ShapeDtypeStruct((M, hpg * E), BF),
        scratch_shapes=[pltpu.VMEM((seq, LANES), F32)] * 3,
        compiler_params=_cparams("parallel", "parallel", "arbitrary"),
        name="dilated_attn",
    )(slopes, qkv, qkv, qkv)


def _sgu_kernel(x_ref, lg_ref, lb_ref, ws_ref, bt_ref, o_ref):
    tm = x_ref.shape[0]
    CH = SGU_CHUNK
    GD = SGU_WIDTH // SGU_GROUPS
    u = _gelu_tanh(x_ref[:, :SGU_WIDTH])
    v = _gelu_tanh(x_ref[:, SGU_WIDTH:])
    mu = jnp.mean(v, axis=-1, keepdims=True)
    vc = v - mu
    var = jnp.mean(vc * vc, axis=-1, keepdims=True)
    vn = (vc * lax.rsqrt(var + LN_EPS) * lg_ref[...] + lb_ref[...]).astype(BF)
    ti = lax.broadcasted_iota(jnp.int32, (CH, CH), 0)
    si = lax.broadcasted_iota(jnp.int32, (CH, CH), 1)
    causal = ti >= si
    for gidx in range(SGU_GROUPS):
        w = jnp.where(causal, ws_ref[gidx], 0.0).astype(BF)
        bias = bt_ref[:, gidx:gidx + 1]
        cs = slice(gidx * GD, (gidx + 1) * GD)
        for c in range(tm // CH):
            rs = slice(c * CH, (c + 1) * CH)
            mixed = _dot(w, vn[rs, cs]) + bias
            o_ref[rs, cs] = (u[rs, cs] * mixed).astype(o_ref.dtype)


def _sgu(proj, ln_g, ln_b, w_s, b_t, *, tm=512):
    M = proj.shape[0]
    return pl.pallas_call(
        _sgu_kernel,
        grid=(M // tm,),
        in_specs=[
            pl.BlockSpec((tm, N_SGU), lambda i: (i, COL_SGU // N_SGU)),
            pl.BlockSpec((1, SGU_WIDTH), lambda i: (0, 0)),
            pl.BlockSpec((1, SGU_WIDTH), lambda i: (0, 0)),
            pl.BlockSpec(w_s.shape, lambda i: (0, 0, 0)),
            pl.BlockSpec(b_t.shape, lambda i: (0, 0)),
        ],
        out_specs=pl.BlockSpec((tm, SGU_WIDTH), lambda i: (i, 0)),
        out_shape=jax.ShapeDtypeStruct((M, SGU_WIDTH), BF),
        compiler_params=_cparams("parallel"),
        name="sgu",
    )(proj, ln_g, ln_b, w_s, b_t)


def _merge_kernel(h_ref, yr_ref, ya_ref, ys_ref, g0_ref, g1_ref, g2_ref, wr_ref, wa_ref, ws_ref,
                  wo_ref, pg_ref, o_ref, acc_scr):
    j = pl.program_id(1)

    @pl.when(j == 0)
    def _():
        acc_scr[...] = jnp.zeros_like(acc_scr)

    merged = (jax.nn.sigmoid(g0_ref[...]) * _dot(yr_ref[...], wr_ref[...])
              + jax.nn.sigmoid(g1_ref[...]) * _dot(ya_ref[...], wa_ref[...])
              + jax.nn.sigmoid(g2_ref[...]) * _dot(ys_ref[...], ws_ref[...]))
    acc_scr[...] += _dot(merged.astype(BF), wo_ref[...])

    @pl.when(j == pl.num_programs(1) - 1)
    def _():
        o_ref[...] = h_ref[...] + _rms(acc_scr[...], pg_ref[...])


def _merge(h, yr, ya, ys, gates, wr, wa, ws, wo, post_g, *, tm=512, tn=512):
    M, D = h.shape
    nj = D // tn
    act = lambda a: pl.BlockSpec((tm, a.shape[1]), lambda i, j: (i, 0))
    gate = lambda br: pl.BlockSpec((tm, tn), lambda i, j: (i, COL_GATE // tn + br * nj + j))
    wcol = lambda a: pl.BlockSpec((a.shape[0], tn), lambda i, j: (0, j))
    return pl.pallas_call(
        _merge_kernel,
        grid=(M // tm, nj),
        in_specs=[
            pl.BlockSpec((tm, D), lambda i, j: (i, 0)),
            act(yr), act(ya), act(ys), gate(0), gate(1), gate(2),
            wcol(wr), wcol(wa), wcol(ws),
            pl.BlockSpec((tn, D), lambda i, j: (j, 0)),
            pl.BlockSpec((1, D), lambda i, j: (0, 0)),
        ],
        out_specs=pl.BlockSpec((tm, D), lambda i, j: (i, 0)),
        out_shape=jax.ShapeDtypeStruct((M, D), F32),
        scratch_shapes=[pltpu.VMEM((tm, D), F32)],
        compiler_params=_cparams("parallel", "arbitrary"),
        name="merge_out",
    )(h, yr, ya, ys, gates, gates, gates, wr, wa, ws, wo, post_g)


def _alibi_slopes(n_heads):
    def geometric(n):
        start = 2.0 ** (-8.0 / n)
        return [start ** (i + 1) for i in range(n)]
    closest = 2 ** int(math.floor(math.log2(n_heads)))
    slopes = geometric(closest)
    if closest < n_heads:
        slopes += geometric(2 * closest)[0::2][: n_heads - closest]
    return np.array(sorted(slopes, reverse=True), dtype=np.float32)


def _pad_rows(x, before, total):
    return jnp.pad(x, ((before, total - before - x.shape[0]), (0, 0)))


def _cast_kernel(src_ref, o_ref, *, axis, n_valid):
    valid = pl.program_id(axis) < n_valid

    @pl.when(valid)
    def _():
        o_ref[...] = src_ref[...].astype(BF)

    @pl.when(jnp.logical_not(valid))
    def _():
        o_ref[...] = jnp.zeros_like(o_ref)


def _cast_rows(w, l, *, rows, n_out):
    _, R, C = w.shape
    n_src = R // rows
    assert n_src * rows == R and n_out >= n_src
    return pl.pallas_call(
        functools.partial(_cast_kernel, axis=0, n_valid=n_src),
        grid=(n_out,),
        in_specs=[pl.BlockSpec((None, rows, C), lambda j: (l, jnp.minimum(j, n_src - 1), 0))],
        out_specs=pl.BlockSpec((rows, C), lambda j: (j, 0)),
        out_shape=jax.ShapeDtypeStruct((n_out * rows, C), BF),
        compiler_params=_cparams("parallel"),
        name="cast_rows",
    )(w)


def _gate_up_kernel(src_ref, o_ref):
    F = src_ref.shape[1] // 2
    Fp = o_ref.shape[1] // 2
    for half in range(2):
        o_ref[:, half * Fp:half * Fp + F] = src_ref[:, half * F:(half + 1) * F].astype(BF)
        if Fp > F:
            o_ref[:, half * Fp + F:(half + 1) * Fp] = jnp.zeros((o_ref.shape[0], Fp - F), BF)


def _cast_gate_up(w_gu, l, *, n_out, rows=128):
    _, D, F2 = w_gu.shape
    assert F2 % (2 * LANES) == 0 and n_out * LANES >= F2 // 2 and D % rows == 0
    return pl.pallas_call(
        _gate_up_kernel,
        grid=(D // rows,),
        in_specs=[pl.BlockSpec((None, rows, F2), lambda i: (l, i, 0))],
        out_specs=pl.BlockSpec((rows, 2 * n_out * LANES), lambda i: (i, 0)),
        out_shape=jax.ShapeDtypeStruct((D, 2 * n_out * LANES), BF),
        compiler_params=_cparams("parallel"),
        name="cast_gate_up",
    )(w_gu)


def _win_kernel(src_ref, x_ref, o_ref, *, rows):
    o_ref[...] = src_ref[...].astype(BF)

    @pl.when(pl.program_id(0) == SEG_P // rows)
    def _():
        o_ref[pl.ds(SEG_P % rows, LANES), :] = x_ref[...].astype(BF)


def _cast_w_in_t(w_in_t, extra_t, l, *, rows=512):
    _, n_in, D = w_in_t.shape
    assert n_in - N_SHIFT == N_PROJ - COL_QKV
    assert COL_QKV % rows == 0 and N_PROJ % rows == 0 and SEG_P % rows + LANES <= rows
    n_head = COL_QKV // rows
    sub = 8
    assert rows % sub == 0 and (COL_QKV - N_SHIFT) % sub == 0

    def src_idx(j):
        back = jnp.where(j < n_head, 0, (COL_QKV - N_SHIFT) // sub)
        return (l, (j * (rows // sub) - back) * sub, 0)

    return pl.pallas_call(
        functools.partial(_win_kernel, rows=rows),
        grid=(N_PROJ // rows,),
        in_specs=[pl.BlockSpec((None, pl.Element(rows), pl.Element(D)), src_idx),
                  pl.BlockSpec((LANES, D), lambda j: (0, 0))],
        out_specs=pl.BlockSpec((rows, D), lambda j: (j, 0)),
        out_shape=jax.ShapeDtypeStruct((N_PROJ, D), BF),
        compiler_params=_cparams("parallel"),
        name="cast_w_in",
    )(w_in_t, extra_t)


def kernel(x, ffn1_pre_g, ffn1_w_gu, ffn1_w_down, ffn1_post_g, mix_pre_g, w_in, shift_mu, decay_w0, decay_w2, iclr_a0, iclr_a2, gate_g2, k_k, k_a, r_k, lnx_w, lnx_b, vres_w1, vres_mu, vres_v0, vres_w2, sgu_ln_g, sgu_ln_b, sgu_w_s, sgu_b, w_b_rwkv, w_b_attn, w_b_sgu, w_out, mix_post_g, ffn2_pre_g, ffn2_w_gu, ffn2_w_down, ffn2_post_g):
    B, T, D = x.shape
    depth = w_in.shape[0]
    F = ffn1_w_down.shape[1]
    f_blocks = -(-F // FFN_TF) * FFN_TF // LANES
    slopes = jnp.asarray(_alibi_slopes(ATTN_HEADS))
    row = lambda p: p.reshape(1, -1)

    def ffn(h, pre_g, w_gu, w_down, post_g, l):
        wgu = _cast_gate_up(w_gu, l, n_out=f_blocks)
        wd = _cast_rows(w_down, l, rows=LANES, n_out=f_blocks)
        return _ffn(h, row(pre_g[l]), wgu, wd, row(post_g[l]))

    w_in_t = jnp.swapaxes(w_in, 1, 2)
    h = x.reshape(B * T, D)
    v_first = None
    for l in range(depth):
        h = ffn(h, ffn1_pre_g, ffn1_w_gu, ffn1_w_down, ffn1_post_g, l)

        if l == 0:
            extra_t = jnp.zeros((LANES, D), F32)
            extra_mu = jnp.zeros((LANES,), F32)
        else:
            extra_t = _pad_rows(vres_w1[l - 1].T, 0, LANES)
            extra_mu = jnp.pad(vres_mu[l - 1], (0, LANES - VRES_LORA))
        mu = jnp.concatenate([jnp.pad(shift_mu[l], (0, SEG_P - N_SHIFT)), extra_mu]).reshape(1, -1)
        proj = _norm_matmul(h, row(mix_pre_g[l]), _cast_w_in_t(w_in_t, extra_t, l))

        vres = None
        if l > 0:
            vres = (v_first, row(vres_v0[l - 1]), _pad_rows(vres_w2[l - 1], 0, LANES).astype(BF))
        r, k, v, lw, kkr, a, g = _rwkv_prep(
            proj, mu, row(decay_w0[l]), _pad_rows(decay_w2[l], 0, LANES).astype(BF),
            row(iclr_a0[l]), _pad_rows(iclr_a2[l], DECAY_LORA, LANES).astype(BF),
            _pad_rows(gate_g2[l], 0, 2 * LANES).astype(BF), row(k_k[l]), row(k_a[l]), vres, seq=T)
        if l == 0:
            v_first = v
        y_rwkv = _wkv(r, k, v, lw, kkr, a, g, row(r_k[l]), row(lnx_w[l]), row(lnx_b[l]),
                      batch=B, seq=T)

        y_attn = _attention(proj, slopes, batch=B, seq=T)
        y_sgu = _sgu(proj, row(sgu_ln_g[l]), row(sgu_ln_b[l]), sgu_w_s[l], sgu_b[l].T)

        cast = lambda w: _cast_rows(w, l, rows=2 * LANES, n_out=w.shape[1] // (2 * LANES))
        h = _merge(h, y_rwkv, y_attn, y_sgu, proj, cast(w_b_rwkv), cast(w_b_attn), cast(w_b_sgu),
                   cast(w_out), row(mix_post_g[l]))

        h = ffn(h, ffn2_pre_g, ffn2_w_gu, ffn2_w_down, ffn2_post_g, l)
    return h.reshape(B, T, D)
```

```python
import functools
import math

import numpy as np
import jax
import jax.numpy as jnp
from jax import lax
from jax.experimental import pallas as pl
from jax.experimental.pallas import tpu as pltpu

BF = jnp.bfloat16
F32 = jnp.float32

D_MODEL = 2048
RWKV_HEAD_DIM = 64
RWKV_WIDTH = D_MODEL // 2
DECAY_LORA = 64
ICLR_LORA = 64
VRES_LORA = 32
GATE_LORA = 160
LNX_EPS = 64e-5
ATTN_GROUPS = ((128, 1), (512, 4), (2048, 16))
ATTN_HEADS_PER_GROUP = 4
ATTN_HEAD_DIM = 128
ATTN_HEADS = len(ATTN_GROUPS) * ATTN_HEADS_PER_GROUP
ATTN_BLK = 128
ATTN_BLOCKS_PER_ITER = 4
SGU_CHUNK = 128
SGU_GROUPS = 4
SGU_WIDTH = D_MODEL // 4
RMS_EPS = 1e-6
LN_EPS = 1e-5
N_BRANCHES = 3

LANES = 128
WKV_CHUNK = 64
FFN_TF = 512
NEG_BIG = -1e30

N_SHIFT = 3 * RWKV_WIDTH + DECAY_LORA + ICLR_LORA + GATE_LORA
SEG_R, SEG_K, SEG_V = 0, RWKV_WIDTH, 2 * RWKV_WIDTH
SEG_WA = 3 * RWKV_WIDTH
SEG_G = SEG_WA + 128
SEG_P = SEG_G + 256
N_QKV = 3 * ATTN_HEADS * ATTN_HEAD_DIM
N_SGU = 2 * SGU_WIDTH
N_GATE = N_BRANCHES * D_MODEL
COL_SHIFT = 0
COL_QKV = SEG_P + LANES
COL_SGU = COL_QKV + N_QKV
COL_GATE = COL_SGU + N_SGU
N_PROJ = COL_GATE + N_GATE

VMEM_LIMIT = 56 * 1024 * 1024


def _cparams(*sem):
    return pltpu.CompilerParams(dimension_semantics=sem, vmem_limit_bytes=VMEM_LIMIT)


def _dot(a, b):
    return jnp.dot(a, b, preferred_element_type=F32)


def _dot_nt(a, b):
    return lax.dot_general(a, b, (((1,), (1,)), ((), ())), preferred_element_type=F32)


def _rms(x, g):
    return x * lax.rsqrt(jnp.mean(x * x, axis=-1, keepdims=True) + RMS_EPS) * g


def _ffn_kernel(h_ref, pre_g_ref, wg_ref, wu_ref, wd_ref, post_g_ref, o_ref, xn_scr, acc_scr):
    f = pl.program_id(1)

    @pl.when(f == 0)
    def _():
        xn_scr[...] = _rms(h_ref[...], pre_g_ref[...]).astype(BF)
        acc_scr[...] = jnp.zeros_like(acc_scr)

    xn = xn_scr[...]
    g = _dot(xn, wg_ref[...])
    u = _dot(xn, wu_ref[...])
    a = (g * jax.nn.sigmoid(g) * u).astype(BF)
    acc_scr[...] += _dot(a, wd_ref[...])

    @pl.when(f == pl.num_programs(1) - 1)
    def _():
        o_ref[...] = h_ref[...] + 0.5 * _rms(acc_scr[...], post_g_ref[...])


def _ffn(h, pre_g, wgu, wd, post_g, *, tm=512, tf=FFN_TF):
    M, D = h.shape
    Fp = wd.shape[0]
    nf = Fp // tf
    assert wgu.shape == (D, 2 * Fp) and nf * tf == Fp
    return pl.pallas_call(
        _ffn_kernel,
        grid=(M // tm, nf),
        in_specs=[
            pl.BlockSpec((tm, D), lambda i, f: (i, 0)),
            pl.BlockSpec((1, D), lambda i, f: (0, 0)),
            pl.BlockSpec((D, tf), lambda i, f: (0, f)),
            pl.BlockSpec((D, tf), lambda i, f: (0, nf + f)),
            pl.BlockSpec((tf, D), lambda i, f: (f, 0)),
            pl.BlockSpec((1, D), lambda i, f: (0, 0)),
        ],
        out_specs=pl.BlockSpec((tm, D), lambda i, f: (i, 0)),
        out_shape=jax.ShapeDtypeStruct((M, D), F32),
        scratch_shapes=[pltpu.VMEM((tm, D), BF), pltpu.VMEM((tm, D), F32)],
        compiler_params=_cparams("parallel", "arbitrary"),
        name="ffn",
    )(h, pre_g, wgu, wgu, wd, post_g)


def _gelu_tanh(x):
    c = math.sqrt(2.0 / math.pi)
    return x * (0.5 * (1.0 + jnp.tanh(c * (x + 0.044715 * (x * x * x)))))


def _nmm_kernel(h_ref, g_ref, w_ref, o_ref, xn_scr):
    @pl.when(pl.program_id(1) == 0)
    def _():
        xn_scr[...] = _rms(h_ref[...], g_ref[...]).astype(BF)

    o_ref[...] = _dot_nt(xn_scr[...], w_ref[...]).astype(o_ref.dtype)


def _norm_matmul(h, g, w_t, *, out_dtype=F32, tm=1024, tn=1536, name="in_proj"):
    M, D = h.shape
    N = w_t.shape[0]
    return pl.pallas_call(
        _nmm_kernel,
        grid=(M // tm, N // tn),
        in_specs=[
            pl.BlockSpec((tm, D), lambda i, j: (i, 0)),
            pl.BlockSpec((1, D), lambda i, j: (0, 0)),
            pl.BlockSpec((tn, D), lambda i, j: (j, 0)),
        ],
        out_specs=pl.BlockSpec((tm, tn), lambda i, j: (i, j)),
        out_shape=jax.ShapeDtypeStruct((M, N), out_dtype),
        scratch_shapes=[pltpu.VMEM((tm, D), BF)],
        compiler_params=_cparams("parallel", "arbitrary"),
        name=name,
    )(h, g, w_t)


def _prep_kernel(*refs, seq, has_vres):
    if has_vres:
        (ps_ref, prev_ref, mu_ref, w0_ref, w2_ref, a0_ref, a2_ref, g2_ref, kk_ref, ka_ref,
         vf_ref, v0_ref, vw2_ref,
         r_o, k_o, v_o, lw_o, kk_o, a_o, g_o) = refs
    else:
        (ps_ref, prev_ref, mu_ref, w0_ref, w2_ref, a0_ref, a2_ref, g2_ref, kk_ref, ka_ref,
         r_o, k_o, v_o, lw_o, kk_o, a_o, g_o) = refs
    tm = ps_ref.shape[0]
    W = RWKV_WIDTH
    x = ps_ref[...]
    at_seq_start = (pl.program_id(0) * tm) % seq == 0
    last_prev = jnp.where(at_seq_start, 0.0, prev_ref[7:8, :])
    row = lax.broadcasted_iota(jnp.int32, x.shape, 0)
    prev = jnp.where(row == 0, last_prev, pltpu.roll(x, 1, axis=0))
    xs = x + (prev - x) * mu_ref[...]

    xr = xs[:, SEG_R:SEG_R + W]
    xk = xs[:, SEG_K:SEG_K + W]
    xv = xs[:, SEG_V:SEG_V + W]
    xwa = xs[:, SEG_WA:SEG_WA + 128]
    xg = xs[:, SEG_G:SEG_G + 256]

    z = w0_ref[...] + _dot(jnp.tanh(xwa).astype(BF), w2_ref[...])
    w_log = -jax.nn.softplus(-z) - 0.5
    lw_o[...] = -jnp.exp(w_log)
    a = jax.nn.sigmoid(a0_ref[...] + _dot(xwa.astype(BF), a2_ref[...]))
    a_o[...] = a
    g_o[...] = _dot(jax.nn.sigmoid(xg).astype(BF), g2_ref[...])
    kk_o[...] = xk * kk_ref[...]
    k_o[...] = xk * (1.0 + (a - 1.0) * ka_ref[...])
    r_o[...] = xr
    if has_vres:
        pv = xs[:, SEG_P:SEG_P + 128]
        mix = jax.nn.sigmoid(v0_ref[...] + _dot(pv.astype(BF), vw2_ref[...]))
        v_o[...] = xv + (vf_ref[...] - xv) * mix
    else:
        v_o[...] = xv


def _rwkv_prep(ps, mu, w0, w2, a0, a2, g2, k_k, k_a, vres, *, seq, tm=256):
    M = ps.shape[0]
    NA = COL_QKV
    W = RWKV_WIDTH
    row = lambda n: pl.BlockSpec((1, n), lambda i: (0, 0))
    full = lambda a: pl.BlockSpec(a.shape, lambda i: (0, 0))
    tok = pl.BlockSpec((tm, W), lambda i: (i, 0))
    in_specs = [
        pl.BlockSpec((tm, NA), lambda i: (i, 0)),
        pl.BlockSpec((8, NA), lambda i: (jnp.maximum(i * (tm // 8) - 1, 0), 0)),
        row(NA), row(W), full(w2), row(W), full(a2), full(g2), row(W), row(W),
    ]
    args = [ps, ps, mu, w0, w2, a0, a2, g2, k_k, k_a]
    if vres is not None:
        v_first, v0, vw2 = vres
        in_specs += [tok, row(W), full(vw2)]
        args += [v_first, v0, vw2]
    return pl.pallas_call(
        functools.partial(_prep_kernel, seq=seq, has_vres=vres is not None),
        grid=(M // tm,),
        in_specs=in_specs,
        out_specs=[tok] * 7,
        out_shape=[jax.ShapeDtypeStruct((M, W), F32)] * 7,
        compiler_params=_cparams("parallel"),
        name="rwkv_prep",
    )(*args)


def _wkv_kernel(r_ref, k_ref, v_ref, lw_ref, kk_ref, a_ref, g_ref, rk_ref, lnw_ref, lnb_ref,
                y_ref, p_scr):
    C = WKV_CHUNK
    C2 = 2 * C

    @pl.when(pl.program_id(1) == 0)
    def _():
        p_scr[...] = jnp.zeros_like(p_scr)

    lane = lax.broadcasted_iota(jnp.int32, (C, LANES), 1)
    lo = lane < RWKV_HEAD_DIM
    ti = lax.broadcasted_iota(jnp.int32, (C, C), 0)
    tj = lax.broadcasted_iota(jnp.int32, (C, C), 1)
    tri = jnp.where(ti >= tj, 1.0, 0.0).astype(BF)
    tq = lax.broadcasted_iota(jnp.int32, (C, LANES), 0)
    sq = lane % C
    strict = tq > sq
    incl = tq >= sq
    eye_c = jnp.where(tq == sq, 1.0, 0.0)
    si = lax.broadcasted_iota(jnp.int32, (C2, LANES), 0)
    sj = lax.broadcasted_iota(jnp.int32, (C2, LANES), 1)
    eye = si == sj

    def seg_sum(x):
        s_lo = jnp.sum(jnp.where(lo, x, 0.0), axis=1, keepdims=True)
        s_hi = jnp.sum(jnp.where(lo, 0.0, x), axis=1, keepdims=True)
        return jnp.where(lo, s_lo, s_hi)

    def stack(x):
        return jnp.concatenate([jnp.where(lo, x, 0.0), jnp.where(lo, 0.0, x)], axis=0)

    def bstack(x):
        return stack(x).astype(BF)

    n_pairs = r_ref.shape[1] // LANES
    n_chunks = r_ref.shape[0] // C
    items = [(c, p) for c in range(n_chunks) for p in range(n_pairs)]
    rows = [slice(c * C, (c + 1) * C) for c, _ in items]
    cols = [slice(p * LANES, (p + 1) * LANES) for _, p in items]
    each = lambda f, *xs: [f(*t) for t in zip(*xs)]
    bf = lambda xs: [x.astype(BF) for x in xs]
    load = lambda ref: [ref[rs, cs] for rs, cs in zip(rows, cols)]

    r, k, v, lw, kkr, a = (load(ref) for ref in (r_ref, k_ref, v_ref, lw_ref, kk_ref, a_ref))

    def cumsum(x):
        hi = x.astype(BF)
        both = _dot(tri, jnp.concatenate([hi, (x - hi.astype(F32)).astype(BF)], axis=1))
        return both[:, :LANES] + both[:, LANES:]

    cum = each(cumsum, lw)
    tot = [c[C - 1:C, :] for c in cum]
    kk = each(lambda x: x / jnp.maximum(jnp.sqrt(seg_sum(x * x)), 1e-12), kkr)
    kka = each(lambda x, y: x * y, kk, a)
    A_c = each(lambda x, c, w: -x * jnp.exp(c - w), kk, cum, lw)
    R_c = bf(each(lambda x, c: x * jnp.exp(c), r, cum))
    AR = each(lambda a_, r_: jnp.concatenate([a_.astype(BF), r_], axis=0), A_c, R_c)
    BK_st = each(lambda x, y, c: jnp.concatenate([bstack(x * jnp.exp(-c)), bstack(y * jnp.exp(-c))], axis=0),
                 kka, k, cum)
    V_st = each(bstack, v)
    BKpT = bf(each(lambda x, y, c, t: jnp.concatenate([stack(x * jnp.exp(t - c)).T,
                                                       stack(y * jnp.exp(t - c)).T], axis=1),
                   kka, k, cum, tot))
    gcol = [jnp.exp(jnp.sum(jnp.where(eye, jnp.broadcast_to(t, (C2, LANES)), 0.0), axis=1, keepdims=True))
            for t in tot]

    G = each(_dot_nt, AR, BK_st)
    Aab = [jnp.where(strict, x[:C, :LANES], 0.0) for x in G]
    Aak = bf([jnp.where(strict, x[:C, LANES:], 0.0) for x in G])
    Mrb = bf([jnp.where(incl, x[C:, :LANES], 0.0) for x in G])
    Mrk = bf([jnp.where(incl, x[C:, LANES:], 0.0) for x in G])
    AkV = each(_dot, Aak, V_st)

    S = [eye_c + x for x in Aab]
    Pw = each(lambda x: _dot(x.astype(BF), bstack(x)), Aab)
    n = 2
    while 2 * n < C:
        prod = each(lambda p_, s_: _dot(p_.astype(BF), jnp.concatenate([bstack(p_), bstack(s_)], axis=1)),
                    Pw, S)
        S = each(lambda s_, pr: s_ + pr[:, LANES:], S, prod)
        Pw = [pr[:, :LANES] for pr in prod]
        n *= 2
    S = each(lambda s_, p_: s_ + _dot(p_.astype(BF), bstack(s_)), S, Pw)
    WU = each(lambda s_, a_, u0: _dot(s_.astype(BF), jnp.concatenate([bstack(a_), bstack(u0)], axis=1)),
              S, A_c, AkV)

    def finish(y_, r_, k_, v_, rs, cs):
        mu = seg_sum(y_) * (1.0 / RWKV_HEAD_DIM)
        yc = y_ - mu
        var = seg_sum(yc * yc) * (1.0 / RWKV_HEAD_DIM)
        yn = yc * lax.rsqrt(var + LNX_EPS) * lnw_ref[:, cs] + lnb_ref[:, cs]
        bonus = seg_sum(r_ * k_ * rk_ref[:, cs]) * v_
        return ((yn + bonus) * g_ref[rs, cs]).astype(y_ref.dtype)

    P = [p_scr[p] for p in range(n_pairs)]
    for c in range(n_chunks):
        sel = lambda xs: xs[c * n_pairs:(c + 1) * n_pairs]
        Pb = bf(P)
        U = each(lambda wu, pb: _dot(wu[:, :LANES].astype(BF), pb) + wu[:, LANES:], sel(WU), Pb)
        U_st = each(bstack, U)
        y = each(lambda r_, mb, mk, pb, u, vs: _dot(jnp.concatenate([r_, mb, mk], axis=1),
                                                    jnp.concatenate([pb, u, vs], axis=0)),
                 sel(R_c), sel(Mrb), sel(Mrk), Pb, U_st, sel(V_st))
        P = each(lambda gc, p_, m, u, vs: gc * p_ + _dot(m, jnp.concatenate([u, vs], axis=0)),
                 sel(gcol), P, sel(BKpT), U_st, sel(V_st))
        out = each(finish, y, sel(r), sel(k), sel(v), sel(rows), sel(cols))
        for o, rs, cs in zip(out, sel(rows), sel(cols)):
            y_ref[rs, cs] = o
    for p in range(n_pairs):
        p_scr[p] = P[p]


def _wkv(r, k, v, lw, kkr, a, g, r_k, lnx_w, lnx_b, *, batch, seq, chunks_per_step=2):
    M, W = r.shape
    rows = chunks_per_step * WKV_CHUNK
    steps = seq // rows
    tok = pl.BlockSpec((rows, W), lambda b, c: (b * steps + c, 0))
    par = pl.BlockSpec((1, W), lambda b, c: (0, 0))
    return pl.pallas_call(
        _wkv_kernel,
        grid=(batch, steps),
        in_specs=[tok] * 7 + [par] * 3,
        out_specs=tok,
        out_shape=jax.ShapeDtypeStruct((M, W), BF),
        scratch_shapes=[pltpu.VMEM((W // LANES, LANES, LANES), F32)],
        compiler_params=_cparams("parallel", "arbitrary"),
        name="wkv",
    )(r, k, v, lw, kkr, a, g, r_k, lnx_w, lnx_b)


def _attn_kernel(slopes_ref, q_ref, k_ref, v_ref, o_ref, og_scr, lse_scr, *, seq):
    slot = pl.program_id(1)
    grp = pl.program_id(2)
    blk = ATTN_BLK
    scale = ATTN_HEAD_DIM ** -0.5
    qi = lax.broadcasted_iota(jnp.int32, (blk, 2 * blk), 0)
    kj = lax.broadcasted_iota(jnp.int32, (blk, 2 * blk), 1)
    rel = (qi + blk - kj).astype(F32)
    in_band = (kj >= qi) & (kj <= qi + blk)
    is_prev = kj < blk
    nb = ATTN_BLOCKS_PER_ITER
    each = lambda f, *xs: [f(*t) for t in zip(*xs)]

    def rows(start, d):
        if d == 1:
            return pl.ds(start, blk)
        return pl.ds(start, blk, stride=d)

    for gi, (window, d) in enumerate(ATTN_GROUPS):
        assert window // d == blk and seq % window == 0 and (seq // blk) % nb == 0

        @pl.when(grp == gi)
        def _(gi=gi, d=d):
            slope = slopes_ref[gi * ATTN_HEADS_PER_GROUP + slot]
            bias = jnp.where(in_band, -slope * float(d) * rel, NEG_BIG)
            span = blk * d

            def body(it, carry):
                ids = [it * nb + j for j in range(nb)]
                ns = [i // d for i in ids]
                cur = [rows(n * span + i % d, d) for n, i in zip(ns, ids)]
                prv = [rows(jnp.maximum(n - 1, 0) * span + i % d, d) for n, i in zip(ns, ids)]
                q = [q_ref[c, :].astype(BF) for c in cur]
                kcat = [jnp.concatenate([k_ref[p_, :], k_ref[c, :]], axis=0).astype(BF) for p_, c in zip(prv, cur)]
                vcat = [jnp.concatenate([v_ref[p_, :], v_ref[c, :]], axis=0).astype(BF) for p_, c in zip(prv, cur)]
                s = each(lambda q_, k_: _dot_nt(q_, k_) * scale + bias, q, kcat)
                s = each(lambda s_, n: jnp.where(is_prev & (n == 0), NEG_BIG, s_), s, ns)
                m = [jnp.max(s_, axis=1, keepdims=True) for s_ in s]
                p = each(lambda s_, m_: jnp.exp(s_ - m_), s, m)
                l = [jnp.sum(p_, axis=1, keepdims=True) for p_ in p]
                o = each(lambda p_, v_, l_: _dot(p_.astype(BF), v_) / l_, p, vcat, l)
                lse = each(lambda m_, l_: m_ + jnp.log(l_), m, l)
                for c, o_, e_ in zip(cur, o, lse):
                    og_scr[gi, c, :] = o_
                    lse_scr[gi, c, :] = jnp.broadcast_to(e_, (blk, LANES))
                return carry

            lax.fori_loop(0, seq // blk // nb, body, 0)

    @pl.when(grp == len(ATTN_GROUPS) - 1)
    def _():
        tile = 8 * blk

        def merge(t, carry):
            rs = pl.ds(pl.multiple_of(t * tile, tile), tile)
            lses = [lse_scr[g, rs, :] for g in range(len(ATTN_GROUPS))]
            top = functools.reduce(jnp.maximum, lses)
            w = [jnp.exp(e - top) for e in lses]
            num = sum(w_ * og_scr[g, rs, :] for g, w_ in enumerate(w))
            o_ref[rs, :] = (num / sum(w)).astype(o_ref.dtype)
            return carry

        lax.fori_loop(0, seq // tile, merge, 0)


def _attention(qkv, slopes, *, batch, seq):
    M = qkv.shape[0]
    E = ATTN_HEAD_DIM
    hpg = ATTN_HEADS_PER_GROUP

    def spec(which):
        first = COL_QKV // E + which * ATTN_HEADS
        return pl.BlockSpec((seq, E), lambda b, s, g: (b, first + g * hpg + s))

    return pl.pallas_call(
        functools.partial(_attn_kernel, seq=seq),
        grid=(batch, hpg, len(ATTN_GROUPS)),
        in_specs=[pl.BlockSpec(memory_space=pltpu.SMEM), spec(0), spec(1), spec(2)],
        out_specs=pl.BlockSpec((seq, E), lambda b, s, g: (b, s)),
        out_shape=jax.ShapeDtypeStruct((M, hpg * E), BF),
        scratch_shapes=[pltpu.VMEM((len(ATTN_GROUPS), seq, LANES), F32)] * 2,
        compiler_params=_cparams("parallel", "parallel", "arbitrary"),
        name="dilated_attn",
    )(slopes, qkv, qkv, qkv)


def _sgu_kernel(x_ref, lg_ref, lb_ref, ws_ref, bt_ref, o_ref):
    tm = x_ref.shape[0]
    CH = SGU_CHUNK
    GD = SGU_WIDTH // SGU_GROUPS
    u = _gelu_tanh(x_ref[:, :SGU_WIDTH])
    v = _gelu_tanh(x_ref[:, SGU_WIDTH:])
    mu = jnp.mean(v, axis=-1, keepdims=True)
    vc = v - mu
    var = jnp.mean(vc * vc, axis=-1, keepdims=True)
    vn = (vc * lax.rsqrt(var + LN_EPS) * lg_ref[...] + lb_ref[...]).astype(BF)
    ti = lax.broadcasted_iota(jnp.int32, (CH, CH), 0)
    si = lax.broadcasted_iota(jnp.int32, (CH, CH), 1)
    causal = ti >= si
    for gidx in range(SGU_GROUPS):
        w = jnp.where(causal, ws_ref[gidx], 0.0).astype(BF)
        bias = bt_ref[:, gidx:gidx + 1]
        cs = slice(gidx * GD, (gidx + 1) * GD)
        for c in range(tm // CH):
            rs = slice(c * CH, (c + 1) * CH)
            mixed = _dot(w, vn[rs, cs]) + bias
            o_ref[rs, cs] = (u[rs, cs] * mixed).astype(o_ref.dtype)


def _sgu(proj, ln_g, ln_b, w_s, b_t, *, tm=512):
    M = proj.shape[0]
    return pl.pallas_call(
        _sgu_kernel,
        grid=(M // tm,),
        in_specs=[
            pl.BlockSpec((tm, N_SGU), lambda i: (i, COL_SGU // N_SGU)),
            pl.BlockSpec((1, SGU_WIDTH), lambda i: (0, 0)),
            pl.BlockSpec((1, SGU_WIDTH), lambda i: (0, 0)),
            pl.BlockSpec(w_s.shape, lambda i: (0, 0, 0)),
            pl.BlockSpec(b_t.shape, lambda i: (0, 0)),
        ],
        out_specs=pl.BlockSpec((tm, SGU_WIDTH), lambda i: (i, 0)),
        out_shape=jax.ShapeDtypeStruct((M, SGU_WIDTH), BF),
        compiler_params=_cparams("parallel"),
        name="sgu",
    )(proj, ln_g, ln_b, w_s, b_t)


def _merge_kernel(h_ref, yr_ref, ya_ref, ys_ref, g0_ref, g1_ref, g2_ref, wr_ref, wa_ref, ws_ref,
                  wo_ref, pg_ref, o_ref, acc_scr):
    j = pl.program_id(1)

    @pl.when(j == 0)
    def _():
        acc_scr[...] = jnp.zeros_like(acc_scr)

    merged = (jax.nn.sigmoid(g0_ref[...]) * _dot(yr_ref[...], wr_ref[...])
              + jax.nn.sigmoid(g1_ref[...]) * _dot(ya_ref[...], wa_ref[...])
              + jax.nn.sigmoid(g2_ref[...]) * _dot(ys_ref[...], ws_ref[...]))
    acc_scr[...] += _dot(merged.astype(BF), wo_ref[...])

    @pl.when(j == pl.num_programs(1) - 1)
    def _():
        o_ref[...] = h_ref[...] + _rms(acc_scr[...], pg_ref[...])


def _merge(h, yr, ya, ys, gates, wr, wa, ws, wo, post_g, *, tm=512, tn=512):
    M, D = h.shape
    nj = D // tn
    act = lambda a: pl.BlockSpec((tm, a.shape[1]), lambda i, j: (i, 0))
    gate = lambda br: pl.BlockSpec((tm, tn), lambda i, j: (i, COL_GATE // tn + br * nj + j))
    wcol = lambda a: pl.BlockSpec((a.shape[0], tn), lambda i, j: (0, j))
    return pl.pallas_call(
        _merge_kernel,
        grid=(M // tm, nj),
        in_specs=[
            pl.BlockSpec((tm, D), lambda i, j: (i, 0)),
            act(yr), act(ya), act(ys), gate(0), gate(1), gate(2),
            wcol(wr), wcol(wa), wcol(ws),
            pl.BlockSpec((tn, D), lambda i, j: (j, 0)),
            pl.BlockSpec((1, D), lambda i, j: (0, 0)),
        ],
        out_specs=pl.BlockSpec((tm, D), lambda i, j: (i, 0)),
        out_shape=jax.ShapeDtypeStruct((M, D), F32),
        scratch_shapes=[pltpu.VMEM((tm, D), F32)],
        compiler_params=_cparams("parallel", "arbitrary"),
        name="merge_out",
    )(h, yr, ya, ys, gates, gates, gates, wr, wa, ws, wo, post_g)


def _alibi_slopes(n_heads):
    def geometric(n):
        start = 2.0 ** (-8.0 / n)
        return [start ** (i + 1) for i in range(n)]
    closest = 2 ** int(math.floor(math.log2(n_heads)))
    slopes = geometric(closest)
    if closest < n_heads:
        slopes += geometric(2 * closest)[0::2][: n_heads - closest]
    return np.array(sorted(slopes, reverse=True), dtype=np.float32)


def _pad_rows(x, before, total):
    return jnp.pad(x, ((before, total - before - x.shape[0]), (0, 0)))


def _cast_kernel(src_ref, o_ref, *, axis, n_valid):
    valid = pl.program_id(axis) < n_valid

    @pl.when(valid)
    def _():
        o_ref[...] = src_ref[...].astype(BF)

    @pl.when(jnp.logical_not(valid))
    def _():
        o_ref[...] = jnp.zeros_like(o_ref)


def _cast_rows(w, l, *, rows, n_out):
    _, R, C = w.shape
    n_src = R // rows
    assert n_src * rows == R and n_out >= n_src
    return pl.pallas_call(
        functools.partial(_cast_kernel, axis=0, n_valid=n_src),
        grid=(n_out,),
        in_specs=[pl.BlockSpec((None, rows, C), lambda j: (l, jnp.minimum(j, n_src - 1), 0))],
        out_specs=pl.BlockSpec((rows, C), lambda j: (j, 0)),
        out_shape=jax.ShapeDtypeStruct((n_out * rows, C), BF),
        compiler_params=_cparams("parallel"),
        name="cast_rows",
    )(w)


def _gate_up_kernel(src_ref, o_ref):
    F = src_ref.shape[1] // 2
    Fp = o_ref.shape[1] // 2
    for half in range(2):
        o_ref[:, half * Fp:half * Fp + F] = src_ref[:, half * F:(half + 1) * F].astype(BF)
        if Fp > F:
            o_ref[:, half * Fp + F:(half + 1) * Fp] = jnp.zeros((o_ref.shape[0], Fp - F), BF)


def _cast_gate_up(w_gu, l, *, n_out, rows=128):
    _, D, F2 = w_gu.shape
    assert F2 % (2 * LANES) == 0 and n_out * LANES >= F2 // 2 and D % rows == 0
    return pl.pallas_call(
        _gate_up_kernel,
        grid=(D // rows,),
        in_specs=[pl.BlockSpec((None, rows, F2), lambda i: (l, i, 0))],
        out_specs=pl.BlockSpec((rows, 2 * n_out * LANES), lambda i: (i, 0)),
        out_shape=jax.ShapeDtypeStruct((D, 2 * n_out * LANES), BF),
        compiler_params=_cparams("parallel"),
        name="cast_gate_up",
    )(w_gu)


def _win_kernel(src_ref, x_ref, o_ref, *, rows):
    o_ref[...] = src_ref[...].astype(BF)

    @pl.when(pl.program_id(0) == SEG_P // rows)
    def _():
        o_ref[pl.ds(SEG_P % rows, LANES), :] = x_ref[...].astype(BF)


def _cast_w_in_t(w_in_t, extra_t, l, *, rows=512):
    _, n_in, D = w_in_t.shape
    assert n_in - N_SHIFT == N_PROJ - COL_QKV
    assert COL_QKV % rows == 0 and N_PROJ % rows == 0 and SEG_P % rows + LANES <= rows
    n_head = COL_QKV // rows
    sub = 8
    assert rows % sub == 0 and (COL_QKV - N_SHIFT) % sub == 0

    def src_idx(j):
        back = jnp.where(j < n_head, 0, (COL_QKV - N_SHIFT) // sub)
        return (l, (j * (rows // sub) - back) * sub, 0)

    return pl.pallas_call(
        functools.partial(_win_kernel, rows=rows),
        grid=(N_PROJ // rows,),
        in_specs=[pl.BlockSpec((None, pl.Element(rows), pl.Element(D)), src_idx),
                  pl.BlockSpec((LANES, D), lambda j: (0, 0))],
        out_specs=pl.BlockSpec((rows, D), lambda j: (j, 0)),
        out_shape=jax.ShapeDtypeStruct((N_PROJ, D), BF),
        compiler_params=_cparams("parallel"),
        name="cast_w_in",
    )(w_in_t, extra_t)


def kernel(x, ffn1_pre_g, ffn1_w_gu, ffn1_w_down, ffn1_post_g, mix_pre_g, w_in, shift_mu, decay_w0, decay_w2, iclr_a0, iclr_a2, gate_g2, k_k, k_a, r_k, lnx_w, lnx_b, vres_w1, vres_mu, vres_v0, vres_w2, sgu_ln_g, sgu_ln_b, sgu_w_s, sgu_b, w_b_rwkv, w_b_attn, w_b_sgu, w_out, mix_post_g, ffn2_pre_g, ffn2_w_gu, ffn2_w_down, ffn2_post_g):
    B, T, D = x.shape
    depth = w_in.shape[0]
    F = ffn1_w_down.shape[1]
    f_blocks = -(-F // FFN_TF) * FFN_TF // LANES
    slopes = jnp.asarray(_alibi_slopes(ATTN_HEADS))
    row = lambda p: p.reshape(1, -1)

    def ffn(h, pre_g, w_gu, w_down, post_g, l):
        wgu = _cast_gate_up(w_gu, l, n_out=f_blocks)
        wd = _cast_rows(w_down, l, rows=LANES, n_out=f_blocks)
        return _ffn(h, row(pre_g[l]), wgu, wd, row(post_g[l]))

    w_in_t = jnp.swapaxes(w_in, 1, 2)
    h = x.reshape(B * T, D)
    v_first = None
    for l in range(depth):
        h = ffn(h, ffn1_pre_g, ffn1_w_gu, ffn1_w_down, ffn1_post_g, l)

        if l == 0:
            extra_t = jnp.zeros((LANES, D), F32)
            extra_mu = jnp.zeros((LANES,), F32)
        else:
            extra_t = _pad_rows(vres_w1[l - 1].T, 0, LANES)
            extra_mu = jnp.pad(vres_mu[l - 1], (0, LANES - VRES_LORA))
        mu = jnp.concatenate([jnp.pad(shift_mu[l], (0, SEG_P - N_SHIFT)), extra_mu]).reshape(1, -1)
        proj = _norm_matmul(h, row(mix_pre_g[l]), _cast_w_in_t(w_in_t, extra_t, l))

        vres = None
        if l > 0:
            vres = (v_first, row(vres_v0[l - 1]), _pad_rows(vres_w2[l - 1], 0, LANES).astype(BF))
        r, k, v, lw, kkr, a, g = _rwkv_prep(
            proj, mu, row(decay_w0[l]), _pad_rows(decay_w2[l], 0, LANES).astype(BF),
            row(iclr_a0[l]), _pad_rows(iclr_a2[l], DECAY_LORA, LANES).astype(BF),
            _pad_rows(gate_g2[l], 0, 2 * LANES).astype(BF), row(k_k[l]), row(k_a[l]), vres, seq=T)
        if l == 0:
            v_first = v
        y_rwkv = _wkv(r, k, v, lw, kkr, a, g, row(r_k[l]), row(lnx_w[l]), row(lnx_b[l]),
                      batch=B, seq=T)

        y_attn = _attention(proj, slopes, batch=B, seq=T)
        y_sgu = _sgu(proj, row(sgu_ln_g[l]), row(sgu_ln_b[l]), sgu_w_s[l], sgu_b[l].T)

        cast = lambda w: _cast_rows(w, l, rows=2 * LANES, n_out=w.shape[1] // (2 * LANES))
        h = _merge(h, y_rwkv, y_attn, y_sgu, proj, cast(w_b_rwkv), cast(w_b_attn), cast(w_b_sgu),
                   cast(w_out), row(mix_post_g[l]))

        h = ffn(h, ffn2_pre_g, ffn2_w_gu, ffn2_w_down, ffn2_post_g, l)
    return h.reshape(B, T, D)
```

```python
import functools
import math
from typing import Callable, NamedTuple, Optional

import numpy as np
import jax
import jax.numpy as jnp
from jax import lax
from jax.experimental import pallas as pl
from jax.experimental.pallas import tpu as pltpu

BF = jnp.bfloat16
F32 = jnp.float32

D_MODEL = 2048
RWKV_HEAD_DIM = 64
RWKV_WIDTH = D_MODEL // 2
DECAY_LORA = 64
ICLR_LORA = 64
VRES_LORA = 32
GATE_LORA = 160
LNX_EPS = 64e-5
ATTN_GROUPS = ((128, 1), (512, 4), (2048, 16))
ATTN_HEADS_PER_GROUP = 4
ATTN_HEAD_DIM = 128
ATTN_HEADS = len(ATTN_GROUPS) * ATTN_HEADS_PER_GROUP
ATTN_BLK = 128
ATTN_BLOCKS_PER_ITER = 4
SGU_CHUNK = 128
SGU_GROUPS = 4
SGU_WIDTH = D_MODEL // 4
RMS_EPS = 1e-6
LN_EPS = 1e-5
N_BRANCHES = 3

LANES = 128
SUBLANES = 8
WKV_CHUNK = 64
FFN_TF = 512
NEG_BIG = -1e30

N_SHIFT = 3 * RWKV_WIDTH + DECAY_LORA + ICLR_LORA + GATE_LORA
SEG_R, SEG_K, SEG_V = 0, RWKV_WIDTH, 2 * RWKV_WIDTH
SEG_WA = 3 * RWKV_WIDTH
SEG_G = SEG_WA + 128
SEG_P = SEG_G + 256
N_QKV = 3 * ATTN_HEADS * ATTN_HEAD_DIM
N_SGU = 2 * SGU_WIDTH
N_GATE = N_BRANCHES * D_MODEL
COL_SHIFT = 0
COL_QKV = SEG_P + LANES
COL_SGU = COL_QKV + N_QKV
COL_GATE = COL_SGU + N_SGU
N_PROJ = COL_GATE + N_GATE

VMEM_LIMIT = 56 * 1024 * 1024


def _cparams(*sem):
    return pltpu.CompilerParams(dimension_semantics=sem, vmem_limit_bytes=VMEM_LIMIT)


def _dot(a, b):
    return jnp.dot(a, b, preferred_element_type=F32)


def _dot_nt(a, b):
    return lax.dot_general(a, b, (((1,), (1,)), ((), ())), preferred_element_type=F32)


def _rms(x, g):
    return x * lax.rsqrt(jnp.mean(x * x, axis=-1, keepdims=True) + RMS_EPS) * g


class _CastJob(NamedTuple):
    src: jax.Array
    layer: int
    rows: int
    n_blocks: int
    offset: Callable
    patch: Optional[jax.Array] = None
    patch_block: int = 0
    patch_row: int = 0


class _CastMeta(NamedTuple):
    n_blocks: int
    has_patch: bool
    patch_block: int
    patch_row: int


def _plain_job(src, layer, rows):
    R = src.shape[1]
    assert R % rows == 0 and rows % (2 * SUBLANES) == 0
    return _CastJob(src, layer, rows, R // rows, lambda s: s * (rows // SUBLANES))


def _job_plumbing(jobs, step_of):
    in_specs, args, out_specs, out_shapes, metas = [], [], [], [], []
    for job in jobs:
        C = job.src.shape[2]
        blk = lambda *g, job=job: jnp.minimum(step_of(*g), job.n_blocks - 1)
        in_specs.append(pl.BlockSpec(
            (None, pl.Element(job.rows), pl.Element(C)),
            lambda *g, job=job, blk=blk: (job.layer, job.offset(blk(*g)) * SUBLANES, 0)))
        args.append(job.src)
        if job.patch is not None:
            in_specs.append(pl.BlockSpec(job.patch.shape, lambda *g: (0, 0)))
            args.append(job.patch)
        out_specs.append(pl.BlockSpec((job.rows, C), lambda *g, blk=blk: (blk(*g), 0)))
        out_shapes.append(jax.ShapeDtypeStruct((job.n_blocks * job.rows, C), BF))
        metas.append(_CastMeta(job.n_blocks, job.patch is not None, job.patch_block, job.patch_row))
    return in_specs, args, out_specs, out_shapes, tuple(metas)


def _run_cast_jobs(step, metas, in_refs, out_refs):
    in_refs = iter(in_refs)
    for meta, dst in zip(metas, out_refs):
        src = next(in_refs)
        patch = next(in_refs) if meta.has_patch else None

        @pl.when(step < meta.n_blocks)
        def _(meta=meta, src=src, patch=patch, dst=dst):
            dst[...] = src[...].astype(BF)
            if patch is not None:
                @pl.when(step == meta.patch_block)
                def _():
                    dst[pl.ds(meta.patch_row, LANES), :] = patch[...].astype(BF)


def _n_job_inputs(metas):
    return sum(2 if m.has_patch else 1 for m in metas)


def _cast_kernel(*refs, metas):
    n_in = _n_job_inputs(metas)
    _run_cast_jobs(pl.program_id(0), metas, refs[:n_in], refs[n_in:])


def _cast(job):
    in_specs, args, out_specs, out_shapes, metas = _job_plumbing([job], lambda s: s)
    return pl.pallas_call(
        functools.partial(_cast_kernel, metas=metas),
        grid=(job.n_blocks,),
        in_specs=in_specs,
        out_specs=out_specs,
        out_shape=out_shapes,
        compiler_params=_cparams("arbitrary"),
        name="cast",
    )(*args)[0]


def _ffn_kernel(*refs, tail, metas):
    n_in = _n_job_inputs(metas)
    h_ref, pre_g_ref, wg_ref, wu_ref, wd_ref, post_g_ref = refs[:6]
    job_in = refs[6:6 + n_in]
    o_ref = refs[6 + n_in]
    job_out = refs[7 + n_in:7 + n_in + len(metas)]
    xn_scr, acc_scr = refs[7 + n_in + len(metas):]
    i, f = pl.program_id(0), pl.program_id(1)
    last = pl.num_programs(1) - 1
    tf = wd_ref.shape[0]

    @pl.when(f == 0)
    def _():
        xn_scr[...] = _rms(h_ref[...], pre_g_ref[...]).astype(BF)
        acc_scr[...] = jnp.zeros_like(acc_scr)

    def accumulate(lo):
        xn = xn_scr[...]
        g = _dot(xn, wg_ref[:, lo:])
        u = _dot(xn, wu_ref[:, lo:])
        a = (g * jax.nn.sigmoid(g) * u).astype(BF)
        acc_scr[...] += _dot(a, wd_ref[lo:, :])

    @pl.when(f < last)
    def _():
        accumulate(0)

    @pl.when(f == last)
    def _():
        accumulate(tf - tail)
        o_ref[...] = h_ref[...] + 0.5 * _rms(acc_scr[...], post_g_ref[...])

    _run_cast_jobs(i * pl.num_programs(1) + f, metas, job_in, job_out)


def _ffn(h, pre_g, wgu, wd, post_g, jobs=(), *, tm=512, tf=FFN_TF):
    M, D = h.shape
    F = wd.shape[0]
    nf = -(-F // tf)
    tail = F - (nf - 1) * tf
    assert wgu.shape == (D, 2 * F) and F % LANES == 0 and tf % LANES == 0 and tail % LANES == 0
    start = lambda f, base=0: (base // LANES + jnp.minimum(f * (tf // LANES), (F - tf) // LANES)) * LANES
    j_in, j_args, j_out, j_shapes, metas = _job_plumbing(jobs, lambda i, f: i * nf + f)
    return pl.pallas_call(
        functools.partial(_ffn_kernel, tail=tail, metas=metas),
        grid=(M // tm, nf),
        in_specs=[
            pl.BlockSpec((tm, D), lambda i, f: (i, 0)),
            pl.BlockSpec((1, D), lambda i, f: (0, 0)),
            pl.BlockSpec((pl.Element(D), pl.Element(tf)), lambda i, f: (0, start(f))),
            pl.BlockSpec((pl.Element(D), pl.Element(tf)), lambda i, f: (0, start(f, F))),
            pl.BlockSpec((pl.Element(tf), pl.Element(D)), lambda i, f: (start(f), 0)),
            pl.BlockSpec((1, D), lambda i, f: (0, 0)),
        ] + j_in,
        out_specs=[pl.BlockSpec((tm, D), lambda i, f: (i, 0))] + j_out,
        out_shape=[jax.ShapeDtypeStruct((M, D), F32)] + j_shapes,
        scratch_shapes=[pltpu.VMEM((tm, D), BF), pltpu.VMEM((tm, D), F32)],
        compiler_params=_cparams("arbitrary", "arbitrary"),
        name="ffn",
    )(h, pre_g, wgu, wgu, wd, post_g, *j_args)


def _gelu_tanh(x):
    c = math.sqrt(2.0 / math.pi)
    return x * (0.5 * (1.0 + jnp.tanh(c * (x + 0.044715 * (x * x * x)))))


def _nmm_kernel(*refs, metas):
    n_in = _n_job_inputs(metas)
    h_ref, g_ref, w_ref = refs[:3]
    job_in = refs[3:3 + n_in]
    o_ref = refs[3 + n_in]
    job_out = refs[4 + n_in:4 + n_in + len(metas)]
    xn_scr = refs[-1]
    i, j = pl.program_id(0), pl.program_id(1)

    @pl.when(j == 0)
    def _():
        xn_scr[...] = _rms(h_ref[...], g_ref[...]).astype(BF)

    o_ref[...] = _dot_nt(xn_scr[...], w_ref[...]).astype(o_ref.dtype)
    _run_cast_jobs(i * pl.num_programs(1) + j, metas, job_in, job_out)


def _norm_matmul(h, g, w_t, jobs=(), *, out_dtype=F32, tm=1024, tn=1536, name="in_proj"):
    M, D = h.shape
    N = w_t.shape[0]
    nj = N // tn
    j_in, j_args, j_out, j_shapes, metas = _job_plumbing(jobs, lambda i, j: i * nj + j)
    return pl.pallas_call(
        functools.partial(_nmm_kernel, metas=metas),
        grid=(M // tm, nj),
        in_specs=[
            pl.BlockSpec((tm, D), lambda i, j: (i, 0)),
            pl.BlockSpec((1, D), lambda i, j: (0, 0)),
            pl.BlockSpec((tn, D), lambda i, j: (j, 0)),
        ] + j_in,
        out_specs=[pl.BlockSpec((tm, tn), lambda i, j: (i, j))] + j_out,
        out_shape=[jax.ShapeDtypeStruct((M, N), out_dtype)] + j_shapes,
        scratch_shapes=[pltpu.VMEM((tm, D), BF)],
        compiler_params=_cparams("arbitrary", "arbitrary"),
        name=name,
    )(h, g, w_t, *j_args)


def _prep_kernel(*refs, seq, has_vres):
    if has_vres:
        (ps_ref, prev_ref, mu_ref, w0_ref, w2_ref, a0_ref, a2_ref, g2_ref, kk_ref, ka_ref,
         vf_ref, v0_ref, vw2_ref,
         r_o, k_o, v_o, lw_o, kk_o, a_o, g_o) = refs
    else:
        (ps_ref, prev_ref, mu_ref, w0_ref, w2_ref, a0_ref, a2_ref, g2_ref, kk_ref, ka_ref,
         r_o, k_o, v_o, lw_o, kk_o, a_o, g_o) = refs
    tm = ps_ref.shape[0]
    W = RWKV_WIDTH
    x = ps_ref[...]
    at_seq_start = (pl.program_id(0) * tm) % seq == 0
    last_prev = jnp.where(at_seq_start, 0.0, prev_ref[7:8, :])
    row = lax.broadcasted_iota(jnp.int32, x.shape, 0)
    prev = jnp.where(row == 0, last_prev, pltpu.roll(x, 1, axis=0))
    xs = x + (prev - x) * mu_ref[...]

    xr = xs[:, SEG_R:SEG_R + W]
    xk = xs[:, SEG_K:SEG_K + W]
    xv = xs[:, SEG_V:SEG_V + W]
    xwa = xs[:, SEG_WA:SEG_WA + 128]
    xg = xs[:, SEG_G:SEG_G + 256]

    z = w0_ref[...] + _dot(jnp.tanh(xwa).astype(BF), w2_ref[...])
    w_log = -jax.nn.softplus(-z) - 0.5
    lw_o[...] = -jnp.exp(w_log)
    a = jax.nn.sigmoid(a0_ref[...] + _dot(xwa.astype(BF), a2_ref[...]))
    a_o[...] = a
    g_o[...] = _dot(jax.nn.sigmoid(xg).astype(BF), g2_ref[...])
    kk_o[...] = xk * kk_ref[...]
    k_o[...] = xk * (1.0 + (a - 1.0) * ka_ref[...])
    r_o[...] = xr
    if has_vres:
        pv = xs[:, SEG_P:SEG_P + 128]
        mix = jax.nn.sigmoid(v0_ref[...] + _dot(pv.astype(BF), vw2_ref[...]))
        v_o[...] = xv + (vf_ref[...] - xv) * mix
    else:
        v_o[...] = xv


def _rwkv_prep(ps, mu, w0, w2, a0, a2, g2, k_k, k_a, vres, *, seq, tm=256):
    M = ps.shape[0]
    NA = COL_QKV
    W = RWKV_WIDTH
    row = lambda n: pl.BlockSpec((1, n), lambda i: (0, 0))
    full = lambda a: pl.BlockSpec(a.shape, lambda i: (0, 0))
    tok = pl.BlockSpec((tm, W), lambda i: (i, 0))
    in_specs = [
        pl.BlockSpec((tm, NA), lambda i: (i, 0)),
        pl.BlockSpec((8, NA), lambda i: (jnp.maximum(i * (tm // 8) - 1, 0), 0)),
        row(NA), row(W), full(w2), row(W), full(a2), full(g2), row(W), row(W),
    ]
    args = [ps, ps, mu, w0, w2, a0, a2, g2, k_k, k_a]
    if vres is not None:
        v_first, v0, vw2 = vres
        in_specs += [tok, row(W), full(vw2)]
        args += [v_first, v0, vw2]
    return pl.pallas_call(
        functools.partial(_prep_kernel, seq=seq, has_vres=vres is not None),
        grid=(M // tm,),
        in_specs=in_specs,
        out_specs=[tok] * 7,
        out_shape=[jax.ShapeDtypeStruct((M, W), F32)] * 7,
        compiler_params=_cparams("parallel"),
        name="rwkv_prep",
    )(*args)


def _wkv_kernel(r_ref, k_ref, v_ref, lw_ref, kk_ref, a_ref, g_ref, rk_ref, lnw_ref, lnb_ref,
                y_ref, p_scr):
    C = WKV_CHUNK
    C2 = 2 * C

    @pl.when(pl.program_id(1) == 0)
    def _():
        p_scr[...] = jnp.zeros_like(p_scr)

    lane = lax.broadcasted_iota(jnp.int32, (C, LANES), 1)
    lo = lane < RWKV_HEAD_DIM
    ti = lax.broadcasted_iota(jnp.int32, (C, C), 0)
    tj = lax.broadcasted_iota(jnp.int32, (C, C), 1)
    tri = jnp.where(ti >= tj, 1.0, 0.0).astype(BF)
    tq = lax.broadcasted_iota(jnp.int32, (C, LANES), 0)
    sq = lane % C
    strict = tq > sq
    incl = tq >= sq
    eye_c = jnp.where(tq == sq, 1.0, 0.0)
    si = lax.broadcasted_iota(jnp.int32, (C2, LANES), 0)
    sj = lax.broadcasted_iota(jnp.int32, (C2, LANES), 1)
    eye = si == sj

    def seg_sum(x):
        s_lo = jnp.sum(jnp.where(lo, x, 0.0), axis=1, keepdims=True)
        s_hi = jnp.sum(jnp.where(lo, 0.0, x), axis=1, keepdims=True)
        return jnp.where(lo, s_lo, s_hi)

    def stack(x):
        return jnp.concatenate([jnp.where(lo, x, 0.0), jnp.where(lo, 0.0, x)], axis=0)

    def bstack(x):
        return stack(x).astype(BF)

    n_pairs = r_ref.shape[1] // LANES
    n_chunks = r_ref.shape[0] // C
    items = [(c, p) for c in range(n_chunks) for p in range(n_pairs)]
    rows = [slice(c * C, (c + 1) * C) for c, _ in items]
    cols = [slice(p * LANES, (p + 1) * LANES) for _, p in items]
    each = lambda f, *xs: [f(*t) for t in zip(*xs)]
    bf = lambda xs: [x.astype(BF) for x in xs]
    load = lambda ref: [ref[rs, cs] for rs, cs in zip(rows, cols)]

    r, k, v, lw, kkr, a = (load(ref) for ref in (r_ref, k_ref, v_ref, lw_ref, kk_ref, a_ref))

    def cumsum(x):
        hi = x.astype(BF)
        both = _dot(tri, jnp.concatenate([hi, (x - hi.astype(F32)).astype(BF)], axis=1))
        return both[:, :LANES] + both[:, LANES:]

    cum = each(cumsum, lw)
    tot = [c[C - 1:C, :] for c in cum]
    kk = each(lambda x: x / jnp.maximum(jnp.sqrt(seg_sum(x * x)), 1e-12), kkr)
    kka = each(lambda x, y: x * y, kk, a)
    A_c = each(lambda x, c, w: -x * jnp.exp(c - w), kk, cum, lw)
    R_c = bf(each(lambda x, c: x * jnp.exp(c), r, cum))
    AR = each(lambda a_, r_: jnp.concatenate([a_.astype(BF), r_], axis=0), A_c, R_c)
    BK_st = each(lambda x, y, c: jnp.concatenate([bstack(x * jnp.exp(-c)), bstack(y * jnp.exp(-c))], axis=0),
                 kka, k, cum)
    V_st = each(bstack, v)
    BKpT = bf(each(lambda x, y, c, t: jnp.concatenate([stack(x * jnp.exp(t - c)).T,
                                                       stack(y * jnp.exp(t - c)).T], axis=1),
                   kka, k, cum, tot))
    gcol = [jnp.exp(jnp.sum(jnp.where(eye, jnp.broadcast_to(t, (C2, LANES)), 0.0), axis=1, keepdims=True))
            for t in tot]

    G = each(_dot_nt, AR, BK_st)
    Aab = [jnp.where(strict, x[:C, :LANES], 0.0) for x in G]
    Aak = bf([jnp.where(strict, x[:C, LANES:], 0.0) for x in G])
    Mrb = bf([jnp.where(incl, x[C:, :LANES], 0.0) for x in G])
    Mrk = bf([jnp.where(incl, x[C:, LANES:], 0.0) for x in G])
    AkV = each(_dot, Aak, V_st)

    S = [eye_c + x for x in Aab]
    Pw = each(lambda x: _dot(x.astype(BF), bstack(x)), Aab)
    n = 2
    while 2 * n < C:
        prod = each(lambda p_, s_: _dot(p_.astype(BF), jnp.concatenate([bstack(p_), bstack(s_)], axis=1)),
                    Pw, S)
        S = each(lambda s_, pr: s_ + pr[:, LANES:], S, prod)
        Pw = [pr[:, :LANES] for pr in prod]
        n *= 2
    S = each(lambda s_, p_: s_ + _dot(p_.astype(BF), bstack(s_)), S, Pw)
    WU = each(lambda s_, a_, u0: _dot(s_.astype(BF), jnp.concatenate([bstack(a_), bstack(u0)], axis=1)),
              S, A_c, AkV)

    def finish(y_, r_, k_, v_, rs, cs):
        mu = seg_sum(y_) * (1.0 / RWKV_HEAD_DIM)
        yc = y_ - mu
        var = seg_sum(yc * yc) * (1.0 / RWKV_HEAD_DIM)
        yn = yc * lax.rsqrt(var + LNX_EPS) * lnw_ref[:, cs] + lnb_ref[:, cs]
        bonus = seg_sum(r_ * k_ * rk_ref[:, cs]) * v_
        return ((yn + bonus) * g_ref[rs, cs]).astype(y_ref.dtype)

    P = [p_scr[p] for p in range(n_pairs)]
    for c in range(n_chunks):
        sel = lambda xs: xs[c * n_pairs:(c + 1) * n_pairs]
        Pb = bf(P)
        U = each(lambda wu, pb: _dot(wu[:, :LANES].astype(BF), pb) + wu[:, LANES:], sel(WU), Pb)
        U_st = each(bstack, U)
        y = each(lambda r_, mb, mk, pb, u, vs: _dot(jnp.concatenate([r_, mb, mk], axis=1),
                                                    jnp.concatenate([pb, u, vs], axis=0)),
                 sel(R_c), sel(Mrb), sel(Mrk), Pb, U_st, sel(V_st))
        P = each(lambda gc, p_, m, u, vs: gc * p_ + _dot(m, jnp.concatenate([u, vs], axis=0)),
                 sel(gcol), P, sel(BKpT), U_st, sel(V_st))
        out = each(finish, y, sel(r), sel(k), sel(v), sel(rows), sel(cols))
        for o, rs, cs in zip(out, sel(rows), sel(cols)):
            y_ref[rs, cs] = o
    for p in range(n_pairs):
        p_scr[p] = P[p]


def _wkv(r, k, v, lw, kkr, a, g, r_k, lnx_w, lnx_b, *, batch, seq, chunks_per_step=2):
    M, W = r.shape
    rows = chunks_per_step * WKV_CHUNK
    steps = seq // rows
    tok = pl.BlockSpec((rows, W), lambda b, c: (b * steps + c, 0))
    par = pl.BlockSpec((1, W), lambda b, c: (0, 0))
    return pl.pallas_call(
        _wkv_kernel,
        grid=(batch, steps),
        in_specs=[tok] * 7 + [par] * 3,
        out_specs=tok,
        out_shape=jax.ShapeDtypeStruct((M, W), BF),
        scratch_shapes=[pltpu.VMEM((W // LANES, LANES, LANES), F32)],
        compiler_params=_cparams("parallel", "arbitrary"),
        name="wkv",
    )(r, k, v, lw, kkr, a, g, r_k, lnx_w, lnx_b)


def _attn_kernel(slopes_ref, q_ref, k_ref, v_ref, o_ref, og_scr, lse_scr, *, seq):
    slot = pl.program_id(1)
    grp = pl.program_id(2)
    blk = ATTN_BLK
    scale = ATTN_HEAD_DIM ** -0.5
    qi = lax.broadcasted_iota(jnp.int32, (blk, 2 * blk), 0)
    kj = lax.broadcasted_iota(jnp.int32, (blk, 2 * blk), 1)
    rel = (qi + blk - kj).astype(F32)
    in_band = (kj >= qi) & (kj <= qi + blk)
    is_prev = kj < blk
    nb = ATTN_BLOCKS_PER_ITER
    each = lambda f, *xs: [f(*t) for t in zip(*xs)]

    def rows(start, d):
        if d == 1:
            return pl.ds(start, blk)
        return pl.ds(start, blk, stride=d)

    for gi, (window, d) in enumerate(ATTN_GROUPS):
        assert window // d == blk and seq % window == 0 and (seq // blk) % nb == 0

        @pl.when(grp == gi)
        def _(gi=gi, d=d):
            slope = slopes_ref[gi * ATTN_HEADS_PER_GROUP + slot]
            bias = jnp.where(in_band, -slope * float(d) * rel, NEG_BIG)
            span = blk * d

            def body(it, carry):
                ids = [it * nb + j for j in range(nb)]
                ns = [i // d for i in ids]
                cur = [rows(n * span + i % d, d) for n, i in zip(ns, ids)]
                prv = [rows(jnp.maximum(n - 1, 0) * span + i % d, d) for n, i in zip(ns, ids)]
                q = [q_ref[c, :].astype(BF) for c in cur]
                kcat = [jnp.concatenate([k_ref[p_, :], k_ref[c, :]], axis=0).astype(BF) for p_, c in zip(prv, cur)]
                vcat = [jnp.concatenate([v_ref[p_, :], v_ref[c, :]], axis=0).astype(BF) for p_, c in zip(prv, cur)]
                s = each(lambda q_, k_: _dot_nt(q_, k_) * scale + bias, q, kcat)
                s = each(lambda s_, n: jnp.where(is_prev & (n == 0), NEG_BIG, s_), s, ns)
                m = [jnp.max(s_, axis=1, keepdims=True) for s_ in s]
                p = each(lambda s_, m_: jnp.exp(s_ - m_), s, m)
                l = [jnp.sum(p_, axis=1, keepdims=True) for p_ in p]
                o = each(lambda p_, v_, l_: _dot(p_.astype(BF), v_) / l_, p, vcat, l)
                lse = each(lambda m_, l_: m_ + jnp.log(l_), m, l)
                for c, o_, e_ in zip(cur, o, lse):
                    og_scr[gi, c, :] = o_
                    lse_scr[gi, c, :] = jnp.broadcast_to(e_, (blk, LANES))
                return carry

            lax.fori_loop(0, seq // blk // nb, body, 0)

    @pl.when(grp == len(ATTN_GROUPS) - 1)
    def _():
        tile = 8 * blk

        def merge(t, carry):
            rs = pl.ds(pl.multiple_of(t * tile, tile), tile)
            lses = [lse_scr[g, rs, :] for g in range(len(ATTN_GROUPS))]
            top = functools.reduce(jnp.maximum, lses)
            w = [jnp.exp(e - top) for e in lses]
            num = sum(w_ * og_scr[g, rs, :] for g, w_ in enumerate(w))
            o_ref[rs, :] = (num / sum(w)).astype(o_ref.dtype)
            return carry

        lax.fori_loop(0, seq // tile, merge, 0)


def _attention(qkv, slopes, *, batch, seq):
    M = qkv.shape[0]
    E = ATTN_HEAD_DIM
    hpg = ATTN_HEADS_PER_GROUP

    def spec(which):
        first = COL_QKV // E + which * ATTN_HEADS
        return pl.BlockSpec((seq, E), lambda b, s, g: (b, first + g * hpg + s))

    return pl.pallas_call(
        functools.partial(_attn_kernel, seq=seq),
        grid=(batch, hpg, len(ATTN_GROUPS)),
        in_specs=[pl.BlockSpec(memory_space=pltpu.SMEM), spec(0), spec(1), spec(2)],
        out_specs=pl.BlockSpec((seq, E), lambda b, s, g: (b, s)),
        out_shape=jax.ShapeDtypeStruct((M, hpg * E), BF),
        scratch_shapes=[pltpu.VMEM((len(ATTN_GROUPS), seq, LANES), F32)] * 2,
        compiler_params=_cparams("parallel", "parallel", "arbitrary"),
        name="dilated_attn",
    )(slopes, qkv, qkv, qkv)


def _sgu_kernel(x_ref, lg_ref, lb_ref, ws_ref, bt_ref, o_ref):
    tm = x_ref.shape[0]
    CH = SGU_CHUNK
    GD = SGU_WIDTH // SGU_GROUPS
    u = _gelu_tanh(x_ref[:, :SGU_WIDTH])
    v = _gelu_tanh(x_ref[:, SGU_WIDTH:])
    mu = jnp.mean(v, axis=-1, keepdims=True)
    vc = v - mu
    var = jnp.mean(vc * vc, axis=-1, keepdims=True)
    vn = (vc * lax.rsqrt(var + LN_EPS) * lg_ref[...] + lb_ref[...]).astype(BF)
    ti = lax.broadcasted_iota(jnp.int32, (CH, CH), 0)
    si = lax.broadcasted_iota(jnp.int32, (CH, CH), 1)
    causal = ti >= si
    for gidx in range(SGU_GROUPS):
        w = jnp.where(causal, ws_ref[gidx], 0.0).astype(BF)
        bias = bt_ref[:, gidx:gidx + 1]
        cs = slice(gidx * GD, (gidx + 1) * GD)
        for c in range(tm // CH):
            rs = slice(c * CH, (c + 1) * CH)
            mixed = _dot(w, vn[rs, cs]) + bias
            o_ref[rs, cs] = (u[rs, cs] * mixed).astype(o_ref.dtype)


def _sgu(proj, ln_g, ln_b, w_s, b_t, *, tm=512):
    M = proj.shape[0]
    return pl.pallas_call(
        _sgu_kernel,
        grid=(M // tm,),
        in_specs=[
            pl.BlockSpec((tm, N_SGU), lambda i: (i, COL_SGU // N_SGU)),
            pl.BlockSpec((1, SGU_WIDTH), lambda i: (0, 0)),
            pl.BlockSpec((1, SGU_WIDTH), lambda i: (0, 0)),
            pl.BlockSpec(w_s.shape, lambda i: (0, 0, 0)),
            pl.BlockSpec(b_t.shape, lambda i: (0, 0)),
        ],
        out_specs=pl.BlockSpec((tm, SGU_WIDTH), lambda i: (i, 0)),
        out_shape=jax.ShapeDtypeStruct((M, SGU_WIDTH), BF),
        compiler_params=_cparams("parallel"),
        name="sgu",
    )(proj, ln_g, ln_b, w_s, b_t)


def _merge_kernel(h_ref, yr_ref, ya_ref, ys_ref, g0_ref, g1_ref, g2_ref, wr_ref, wa_ref, ws_ref,
                  wo_ref, pg_ref, o_ref, acc_scr):
    j = pl.program_id(1)

    @pl.when(j == 0)
    def _():
        acc_scr[...] = jnp.zeros_like(acc_scr)

    merged = (jax.nn.sigmoid(g0_ref[...]) * _dot(yr_ref[...], wr_ref[...])
              + jax.nn.sigmoid(g1_ref[...]) * _dot(ya_ref[...], wa_ref[...])
              + jax.nn.sigmoid(g2_ref[...]) * _dot(ys_ref[...], ws_ref[...]))
    acc_scr[...] += _dot(merged.astype(BF), wo_ref[...])

    @pl.when(j == pl.num_programs(1) - 1)
    def _():
        o_ref[...] = h_ref[...] + _rms(acc_scr[...], pg_ref[...])


def _merge(h, yr, ya, ys, gates, wr, wa, ws, wo, post_g, *, tm=512, tn=512):
    M, D = h.shape
    nj = D // tn
    act = lambda a: pl.BlockSpec((tm, a.shape[1]), lambda i, j: (i, 0))
    gate = lambda br: pl.BlockSpec((tm, tn), lambda i, j: (i, COL_GATE // tn + br * nj + j))
    wcol = lambda a: pl.BlockSpec((a.shape[0], tn), lambda i, j: (0, j))
    return pl.pallas_call(
        _merge_kernel,
        grid=(M // tm, nj),
        in_specs=[
            pl.BlockSpec((tm, D), lambda i, j: (i, 0)),
            act(yr), act(ya), act(ys), gate(0), gate(1), gate(2),
            wcol(wr), wcol(wa), wcol(ws),
            pl.BlockSpec((tn, D), lambda i, j: (j, 0)),
            pl.BlockSpec((1, D), lambda i, j: (0, 0)),
        ],
        out_specs=pl.BlockSpec((tm, D), lambda i, j: (i, 0)),
        out_shape=jax.ShapeDtypeStruct((M, D), F32),
        scratch_shapes=[pltpu.VMEM((tm, D), F32)],
        compiler_params=_cparams("parallel", "arbitrary"),
        name="merge_out",
    )(h, yr, ya, ys, gates, gates, gates, wr, wa, ws, wo, post_g)


def _alibi_slopes(n_heads):
    def geometric(n):
        start = 2.0 ** (-8.0 / n)
        return [start ** (i + 1) for i in range(n)]
    closest = 2 ** int(math.floor(math.log2(n_heads)))
    slopes = geometric(closest)
    if closest < n_heads:
        slopes += geometric(2 * closest)[0::2][: n_heads - closest]
    return np.array(sorted(slopes, reverse=True), dtype=np.float32)


def _pad_rows(x, before, total):
    return jnp.pad(x, ((before, total - before - x.shape[0]), (0, 0)))


def _w_in_job(w_in_t, extra_t, l, *, rows=256):
    n_in = w_in_t.shape[1]
    assert n_in - N_SHIFT == N_PROJ - COL_QKV and (COL_QKV - N_SHIFT) % SUBLANES == 0
    assert COL_QKV % rows == 0 and N_PROJ % rows == 0 and SEG_P % rows + LANES <= rows
    n_head = COL_QKV // rows

    def offset(s):
        return s * (rows // SUBLANES) - jnp.where(s < n_head, 0, (COL_QKV - N_SHIFT) // SUBLANES)

    return _CastJob(w_in_t, l, rows, N_PROJ // rows, offset, extra_t, SEG_P // rows, SEG_P % rows)


def kernel(x, ffn1_pre_g, ffn1_w_gu, ffn1_w_down, ffn1_post_g, mix_pre_g, w_in, shift_mu, decay_w0, decay_w2, iclr_a0, iclr_a2, gate_g2, k_k, k_a, r_k, lnx_w, lnx_b, vres_w1, vres_mu, vres_v0, vres_w2, sgu_ln_g, sgu_ln_b, sgu_w_s, sgu_b, w_b_rwkv, w_b_attn, w_b_sgu, w_out, mix_post_g, ffn2_pre_g, ffn2_w_gu, ffn2_w_down, ffn2_post_g):
    B, T, D = x.shape
    depth = w_in.shape[0]
    slopes = jnp.asarray(_alibi_slopes(ATTN_HEADS))
    row = lambda p: p.reshape(1, -1)
    ffn_jobs = lambda w_gu, w_down, l: [_plain_job(w_gu, l, 32), _plain_job(w_down, l, 128)]

    w_in_t = jnp.swapaxes(w_in, 1, 2)
    h = x.reshape(B * T, D)
    v_first = None
    wgu, wd = (_cast(job) for job in ffn_jobs(ffn1_w_gu, ffn1_w_down, 0))
    for l in range(depth):
        if l == 0:
            extra_t = jnp.zeros((LANES, D), F32)
            extra_mu = jnp.zeros((LANES,), F32)
        else:
            extra_t = _pad_rows(vres_w1[l - 1].T, 0, LANES)
            extra_mu = jnp.pad(vres_mu[l - 1], (0, LANES - VRES_LORA))
        jobs = [_w_in_job(w_in_t, extra_t, l)] + [_plain_job(w, l, 64)
                                                    for w in (w_b_rwkv, w_b_attn, w_b_sgu, w_out)]
        h, w_proj, wr, wa, ws, wo = _ffn(h, row(ffn1_pre_g[l]), wgu, wd, row(ffn1_post_g[l]), jobs)

        mu = jnp.concatenate([jnp.pad(shift_mu[l], (0, SEG_P - N_SHIFT)), extra_mu]).reshape(1, -1)
        proj, wgu, wd = _norm_matmul(h, row(mix_pre_g[l]), w_proj, ffn_jobs(ffn2_w_gu, ffn2_w_down, l))

        vres = None
        if l > 0:
            vres = (v_first, row(vres_v0[l - 1]), _pad_rows(vres_w2[l - 1], 0, LANES).astype(BF))
        r, k, v, lw, kkr, a, g = _rwkv_prep(
            proj, mu, row(decay_w0[l]), _pad_rows(decay_w2[l], 0, LANES).astype(BF),
            row(iclr_a0[l]), _pad_rows(iclr_a2[l], DECAY_LORA, LANES).astype(BF),
            _pad_rows(gate_g2[l], 0, 2 * LANES).astype(BF), row(k_k[l]), row(k_a[l]), vres, seq=T)
        if l == 0:
            v_first = v
        y_rwkv = _wkv(r, k, v, lw, kkr, a, g, row(r_k[l]), row(lnx_w[l]), row(lnx_b[l]),
                      batch=B, seq=T)

        y_attn = _attention(proj, slopes, batch=B, seq=T)
        y_sgu = _sgu(proj, row(sgu_ln_g[l]), row(sgu_ln_b[l]), sgu_w_s[l], sgu_b[l].T)

        h = _merge(h, y_rwkv, y_attn, y_sgu, proj, wr, wa, ws, wo, row(mix_post_g[l]))

        jobs = ffn_jobs(ffn1_w_gu, ffn1_w_down, l + 1) if l + 1 < depth else []
        h, *cast = _ffn(h, row(ffn2_pre_g[l]), wgu, wd, row(ffn2_post_g[l]), jobs)
        if cast:
            wgu, wd = cast
    return h.reshape(B, T, D)
```

```python
import functools
import math
from typing import Callable, NamedTuple, Optional

import numpy as np
import jax
import jax.numpy as jnp
from jax import lax
from jax.experimental import pallas as pl
from jax.experimental.pallas import tpu as pltpu

BF = jnp.bfloat16
F32 = jnp.float32

D_MODEL = 2048
RWKV_HEAD_DIM = 64
RWKV_WIDTH = D_MODEL // 2
DECAY_LORA = 64
ICLR_LORA = 64
VRES_LORA = 32
GATE_LORA = 160
LNX_EPS = 64e-5
ATTN_GROUPS = ((128, 1), (512, 4), (2048, 16))
ATTN_HEADS_PER_GROUP = 4
ATTN_HEAD_DIM = 128
ATTN_HEADS = len(ATTN_GROUPS) * ATTN_HEADS_PER_GROUP
ATTN_BLK = 128
ATTN_BLOCKS_PER_ITER = 4
SGU_CHUNK = 128
SGU_GROUPS = 4
SGU_WIDTH = D_MODEL // 4
RMS_EPS = 1e-6
LN_EPS = 1e-5
N_BRANCHES = 3

LANES = 128
SUBLANES = 8
WKV_CHUNK = 64
FFN_TF = 1024
FFN_TF_CARRIER = 512
NEG_BIG = -1e30

N_SHIFT = 3 * RWKV_WIDTH + DECAY_LORA + ICLR_LORA + GATE_LORA
SEG_R, SEG_K, SEG_V = 0, RWKV_WIDTH, 2 * RWKV_WIDTH
SEG_WA = 3 * RWKV_WIDTH
SEG_G = SEG_WA + 128
SEG_P = SEG_G + 256
N_QKV = 3 * ATTN_HEADS * ATTN_HEAD_DIM
N_SGU = 2 * SGU_WIDTH
N_GATE = N_BRANCHES * D_MODEL
COL_SHIFT = 0
COL_QKV = SEG_P + LANES
COL_SGU = COL_QKV + N_QKV
COL_GATE = COL_SGU + N_SGU
N_PROJ = COL_GATE + N_GATE

VMEM_LIMIT = 60 * 1024 * 1024


def _cparams(*sem):
    return pltpu.CompilerParams(dimension_semantics=sem, vmem_limit_bytes=VMEM_LIMIT)


def _dot(a, b):
    return jnp.dot(a, b, preferred_element_type=F32)


def _dot_nt(a, b):
    return lax.dot_general(a, b, (((1,), (1,)), ((), ())), preferred_element_type=F32)


def _rms(x, g):
    return x * lax.rsqrt(jnp.mean(x * x, axis=-1, keepdims=True) + RMS_EPS) * g


class _CastJob(NamedTuple):
    src: jax.Array
    layer: int
    rows: int
    n_blocks: int
    offset: Callable
    patch: Optional[jax.Array] = None
    patch_block: int = 0
    patch_row: int = 0


class _CastMeta(NamedTuple):
    n_blocks: int
    has_patch: bool
    patch_block: int
    patch_row: int


def _plain_job(src, layer, rows):
    R = src.shape[1]
    assert R % rows == 0 and rows % (2 * SUBLANES) == 0
    return _CastJob(src, layer, rows, R // rows, lambda s: s * (rows // SUBLANES))


def _job_plumbing(jobs, step_of, n_steps):
    in_specs, args, out_specs, out_shapes, metas = [], [], [], [], []
    for job in jobs:
        assert job.n_blocks <= n_steps, "carrier kernel has too few grid steps for this cast job"
        C = job.src.shape[2]
        blk = lambda *g, job=job: jnp.minimum(step_of(*g), job.n_blocks - 1)
        in_specs.append(pl.BlockSpec(
            (None, pl.Element(job.rows), pl.Element(C)),
            lambda *g, job=job, blk=blk: (job.layer, job.offset(blk(*g)) * SUBLANES, 0)))
        args.append(job.src)
        if job.patch is not None:
            in_specs.append(pl.BlockSpec(job.patch.shape, lambda *g: (0, 0)))
            args.append(job.patch)
        out_specs.append(pl.BlockSpec((job.rows, C), lambda *g, blk=blk: (blk(*g), 0)))
        out_shapes.append(jax.ShapeDtypeStruct((job.n_blocks * job.rows, C), BF))
        metas.append(_CastMeta(job.n_blocks, job.patch is not None, job.patch_block, job.patch_row))
    return in_specs, args, out_specs, out_shapes, tuple(metas)


def _run_cast_jobs(step, metas, in_refs, out_refs):
    in_refs = iter(in_refs)
    for meta, dst in zip(metas, out_refs):
        src = next(in_refs)
        patch = next(in_refs) if meta.has_patch else None

        @pl.when(step < meta.n_blocks)
        def _(meta=meta, src=src, patch=patch, dst=dst):
            dst[...] = src[...].astype(BF)
            if patch is not None:
                @pl.when(step == meta.patch_block)
                def _():
                    dst[pl.ds(meta.patch_row, LANES), :] = patch[...].astype(BF)


def _n_job_inputs(metas):
    return sum(2 if m.has_patch else 1 for m in metas)


def _cast_kernel(*refs, metas):
    n_in = _n_job_inputs(metas)
    _run_cast_jobs(pl.program_id(0), metas, refs[:n_in], refs[n_in:])


def _cast(job):
    in_specs, args, out_specs, out_shapes, metas = _job_plumbing([job], lambda s: s, job.n_blocks)
    return pl.pallas_call(
        functools.partial(_cast_kernel, metas=metas),
        grid=(job.n_blocks,),
        in_specs=in_specs,
        out_specs=out_specs,
        out_shape=out_shapes,
        compiler_params=_cparams("arbitrary"),
        name="cast",
    )(*args)[0]


def _ffn_kernel(*refs, tail, metas):
    n_in = _n_job_inputs(metas)
    h_ref, pre_g_ref, wg_ref, wu_ref, wd_ref, post_g_ref = refs[:6]
    job_in = refs[6:6 + n_in]
    o_ref = refs[6 + n_in]
    job_out = refs[7 + n_in:7 + n_in + len(metas)]
    xn_scr, acc_scr = refs[7 + n_in + len(metas):]
    i, f = pl.program_id(0), pl.program_id(1)
    last = pl.num_programs(1) - 1
    tf = wd_ref.shape[0]

    @pl.when(f == 0)
    def _():
        xn_scr[...] = _rms(h_ref[...], pre_g_ref[...]).astype(BF)
        acc_scr[...] = jnp.zeros_like(acc_scr)

    def accumulate(lo):
        xn = xn_scr[...]
        g = _dot(xn, wg_ref[:, lo:])
        u = _dot(xn, wu_ref[:, lo:])
        a = (g * jax.nn.sigmoid(g) * u).astype(BF)
        acc_scr[...] += _dot(a, wd_ref[lo:, :])

    @pl.when(f < last)
    def _():
        accumulate(0)

    @pl.when(f == last)
    def _():
        accumulate(tf - tail)
        o_ref[...] = h_ref[...] + 0.5 * _rms(acc_scr[...], post_g_ref[...])

    _run_cast_jobs(i * pl.num_programs(1) + f, metas, job_in, job_out)


def _ffn(h, pre_g, wgu, wd, post_g, jobs=(), *, tm=512, tf=FFN_TF):
    M, D = h.shape
    F = wd.shape[0]
    nf = -(-F // tf)
    tail = F - (nf - 1) * tf
    assert wgu.shape == (D, 2 * F) and F % LANES == 0 and tf % LANES == 0 and tail % LANES == 0
    start = lambda f, base=0: (base // LANES + jnp.minimum(f * (tf // LANES), (F - tf) // LANES)) * LANES
    j_in, j_args, j_out, j_shapes, metas = _job_plumbing(jobs, lambda i, f: i * nf + f, M // tm * nf)
    return pl.pallas_call(
        functools.partial(_ffn_kernel, tail=tail, metas=metas),
        grid=(M // tm, nf),
        in_specs=[
            pl.BlockSpec((tm, D), lambda i, f: (i, 0)),
            pl.BlockSpec((1, D), lambda i, f: (0, 0)),
            pl.BlockSpec((pl.Element(D), pl.Element(tf)), lambda i, f: (0, start(f))),
            pl.BlockSpec((pl.Element(D), pl.Element(tf)), lambda i, f: (0, start(f, F))),
            pl.BlockSpec((pl.Element(tf), pl.Element(D)), lambda i, f: (start(f), 0)),
            pl.BlockSpec((1, D), lambda i, f: (0, 0)),
        ] + j_in,
        out_specs=[pl.BlockSpec((tm, D), lambda i, f: (i, 0))] + j_out,
        out_shape=[jax.ShapeDtypeStruct((M, D), F32)] + j_shapes,
        scratch_shapes=[pltpu.VMEM((tm, D), BF), pltpu.VMEM((tm, D), F32)],
        compiler_params=_cparams("arbitrary", "arbitrary"),
        name="ffn",
    )(h, pre_g, wgu, wgu, wd, post_g, *j_args)


def _gelu_tanh(x):
    c = math.sqrt(2.0 / math.pi)
    return x * (0.5 * (1.0 + jnp.tanh(c * (x + 0.044715 * (x * x * x)))))


def _nmm_kernel(*refs, n_plain, metas):
    n_in = _n_job_inputs(metas)
    h_ref, g_ref, w_ref = refs[:3]
    job_in = refs[3:3 + n_in]
    o_ref, gate_ref = refs[3 + n_in:5 + n_in]
    job_out = refs[5 + n_in:5 + n_in + len(metas)]
    xn_scr = refs[-1]
    i, j = pl.program_id(0), pl.program_id(1)

    @pl.when(j == 0)
    def _():
        xn_scr[...] = _rms(h_ref[...], g_ref[...]).astype(BF)

    y = _dot_nt(xn_scr[...], w_ref[...])

    @pl.when(j < n_plain)
    def _():
        o_ref[...] = y

    @pl.when(j >= n_plain)
    def _():
        gate_ref[...] = jax.nn.sigmoid(y).astype(gate_ref.dtype)

    _run_cast_jobs(i * pl.num_programs(1) + j, metas, job_in, job_out)


def _norm_matmul(h, g, w_t, jobs=(), *, tm=1024, tn=1024):
    M, D = h.shape
    N = w_t.shape[0]
    nj = N // tn
    n_plain = COL_GATE // tn
    assert N == N_PROJ and nj * tn == N and n_plain * tn == COL_GATE
    j_in, j_args, j_out, j_shapes, metas = _job_plumbing(jobs, lambda i, j: i * nj + j, M // tm * nj)
    return pl.pallas_call(
        functools.partial(_nmm_kernel, n_plain=n_plain, metas=metas),
        grid=(M // tm, nj),
        in_specs=[
            pl.BlockSpec((tm, D), lambda i, j: (i, 0)),
            pl.BlockSpec((1, D), lambda i, j: (0, 0)),
            pl.BlockSpec((tn, D), lambda i, j: (j, 0)),
        ] + j_in,
        out_specs=[pl.BlockSpec((tm, tn), lambda i, j: (i, jnp.minimum(j, n_plain - 1))),
                   pl.BlockSpec((tm, tn), lambda i, j: (i, jnp.maximum(j - n_plain, 0)))] + j_out,
        out_shape=[jax.ShapeDtypeStruct((M, COL_GATE), F32),
                   jax.ShapeDtypeStruct((M, N - COL_GATE), BF)] + j_shapes,
        scratch_shapes=[pltpu.VMEM((tm, D), BF)],
        compiler_params=_cparams("arbitrary", "arbitrary"),
        name="in_proj",
    )(h, g, w_t, *j_args)


def _prep_kernel(*refs, seq, has_vres):
    if has_vres:
        (ps_ref, prev_ref, mu_ref, w0_ref, w2_ref, a0_ref, a2_ref, g2_ref, kk_ref, ka_ref,
         vf_ref, v0_ref, vw2_ref,
         r_o, k_o, v_o, lw_o, kk_o, a_o, g_o) = refs
    else:
        (ps_ref, prev_ref, mu_ref, w0_ref, w2_ref, a0_ref, a2_ref, g2_ref, kk_ref, ka_ref,
         r_o, k_o, v_o, lw_o, kk_o, a_o, g_o) = refs
    tm = ps_ref.shape[0]
    W = RWKV_WIDTH
    x = ps_ref[...]
    at_seq_start = (pl.program_id(0) * tm) % seq == 0
    last_prev = jnp.where(at_seq_start, 0.0, prev_ref[7:8, :])
    row = lax.broadcasted_iota(jnp.int32, x.shape, 0)
    prev = jnp.where(row == 0, last_prev, pltpu.roll(x, 1, axis=0))
    xs = x + (prev - x) * mu_ref[...]

    xr = xs[:, SEG_R:SEG_R + W]
    xk = xs[:, SEG_K:SEG_K + W]
    xv = xs[:, SEG_V:SEG_V + W]
    xwa = xs[:, SEG_WA:SEG_WA + 128]
    xg = xs[:, SEG_G:SEG_G + 256]

    z = w0_ref[...] + _dot(jnp.tanh(xwa).astype(BF), w2_ref[...])
    w_log = -jax.nn.softplus(-z) - 0.5
    lw_o[...] = -jnp.exp(w_log)
    a = jax.nn.sigmoid(a0_ref[...] + _dot(xwa.astype(BF), a2_ref[...]))
    a_o[...] = a
    g_o[...] = _dot(jax.nn.sigmoid(xg).astype(BF), g2_ref[...])
    kk_o[...] = xk * kk_ref[...]
    k_o[...] = xk * (1.0 + (a - 1.0) * ka_ref[...])
    r_o[...] = xr
    if has_vres:
        pv = xs[:, SEG_P:SEG_P + 128]
        mix = jax.nn.sigmoid(v0_ref[...] + _dot(pv.astype(BF), vw2_ref[...]))
        v_o[...] = xv + (vf_ref[...] - xv) * mix
    else:
        v_o[...] = xv


def _rwkv_prep(ps, mu, w0, w2, a0, a2, g2, k_k, k_a, vres, *, seq, tm=256):
    M = ps.shape[0]
    NA = COL_QKV
    W = RWKV_WIDTH
    row = lambda n: pl.BlockSpec((1, n), lambda i: (0, 0))
    full = lambda a: pl.BlockSpec(a.shape, lambda i: (0, 0))
    tok = pl.BlockSpec((tm, W), lambda i: (i, 0))
    in_specs = [
        pl.BlockSpec((tm, NA), lambda i: (i, 0)),
        pl.BlockSpec((8, NA), lambda i: (jnp.maximum(i * (tm // 8) - 1, 0), 0)),
        row(NA), row(W), full(w2), row(W), full(a2), full(g2), row(W), row(W),
    ]
    args = [ps, ps, mu, w0, w2, a0, a2, g2, k_k, k_a]
    if vres is not None:
        v_first, v0, vw2 = vres
        in_specs += [tok, row(W), full(vw2)]
        args += [v_first, v0, vw2]
    return pl.pallas_call(
        functools.partial(_prep_kernel, seq=seq, has_vres=vres is not None),
        grid=(M // tm,),
        in_specs=in_specs,
        out_specs=[tok] * 7,
        out_shape=[jax.ShapeDtypeStruct((M, W), F32)] * 7,
        compiler_params=_cparams("parallel"),
        name="rwkv_prep",
    )(*args)


def _wkv_kernel(r_ref, k_ref, v_ref, lw_ref, kk_ref, a_ref, g_ref, rk_ref, lnw_ref, lnb_ref,
                y_ref, p_scr):
    C = WKV_CHUNK
    C2 = 2 * C

    @pl.when(pl.program_id(1) == 0)
    def _():
        p_scr[...] = jnp.zeros_like(p_scr)

    lane = lax.broadcasted_iota(jnp.int32, (C, LANES), 1)
    lo = lane < RWKV_HEAD_DIM
    ti = lax.broadcasted_iota(jnp.int32, (C, C), 0)
    tj = lax.broadcasted_iota(jnp.int32, (C, C), 1)
    tri = jnp.where(ti >= tj, 1.0, 0.0).astype(BF)
    tq = lax.broadcasted_iota(jnp.int32, (C, LANES), 0)
    sq = lane % C
    strict = tq > sq
    incl = tq >= sq
    eye_c = jnp.where(tq == sq, 1.0, 0.0)
    si = lax.broadcasted_iota(jnp.int32, (C2, LANES), 0)
    sj = lax.broadcasted_iota(jnp.int32, (C2, LANES), 1)
    eye = si == sj

    def seg_sum(x):
        s_lo = jnp.sum(jnp.where(lo, x, 0.0), axis=1, keepdims=True)
        s_hi = jnp.sum(jnp.where(lo, 0.0, x), axis=1, keepdims=True)
        return jnp.where(lo, s_lo, s_hi)

    def stack(x):
        return jnp.concatenate([jnp.where(lo, x, 0.0), jnp.where(lo, 0.0, x)], axis=0)

    def bstack(x):
        return stack(x).astype(BF)

    n_pairs = r_ref.shape[1] // LANES
    n_chunks = r_ref.shape[0] // C
    items = [(c, p) for c in range(n_chunks) for p in range(n_pairs)]
    rows = [slice(c * C, (c + 1) * C) for c, _ in items]
    cols = [slice(p * LANES, (p + 1) * LANES) for _, p in items]
    each = lambda f, *xs: [f(*t) for t in zip(*xs)]
    bf = lambda xs: [x.astype(BF) for x in xs]
    load = lambda ref: [ref[rs, cs] for rs, cs in zip(rows, cols)]

    r, k, v, lw, kkr, a = (load(ref) for ref in (r_ref, k_ref, v_ref, lw_ref, kk_ref, a_ref))

    def cumsum(x):
        hi = x.astype(BF)
        both = _dot(tri, jnp.concatenate([hi, (x - hi.astype(F32)).astype(BF)], axis=1))
        return both[:, :LANES] + both[:, LANES:]

    cum = each(cumsum, lw)
    tot = [c[C - 1:C, :] for c in cum]
    kk = each(lambda x: x / jnp.maximum(jnp.sqrt(seg_sum(x * x)), 1e-12), kkr)
    kka = each(lambda x, y: x * y, kk, a)
    A_c = each(lambda x, c, w: -x * jnp.exp(c - w), kk, cum, lw)
    R_c = bf(each(lambda x, c: x * jnp.exp(c), r, cum))
    AR = each(lambda a_, r_: jnp.concatenate([a_.astype(BF), r_], axis=0), A_c, R_c)
    BK_st = each(lambda x, y, c: jnp.concatenate([bstack(x * jnp.exp(-c)), bstack(y * jnp.exp(-c))], axis=0),
                 kka, k, cum)
    V_st = each(bstack, v)
    BKpT = bf(each(lambda x, y, c, t: jnp.concatenate([stack(x * jnp.exp(t - c)).T,
                                                       stack(y * jnp.exp(t - c)).T], axis=1),
                   kka, k, cum, tot))
    gcol = [jnp.exp(jnp.sum(jnp.where(eye, jnp.broadcast_to(t, (C2, LANES)), 0.0), axis=1, keepdims=True))
            for t in tot]

    G = each(_dot_nt, AR, BK_st)
    Aab = [jnp.where(strict, x[:C, :LANES], 0.0) for x in G]
    Aak = bf([jnp.where(strict, x[:C, LANES:], 0.0) for x in G])
    Mrb = bf([jnp.where(incl, x[C:, :LANES], 0.0) for x in G])
    Mrk = bf([jnp.where(incl, x[C:, LANES:], 0.0) for x in G])
    AkV = each(_dot, Aak, V_st)

    S = [eye_c + x for x in Aab]
    Pw = each(lambda x: _dot(x.astype(BF), bstack(x)), Aab)
    n = 2
    while 2 * n < C:
        prod = each(lambda p_, s_: _dot(p_.astype(BF), jnp.concatenate([bstack(p_), bstack(s_)], axis=1)),
                    Pw, S)
        S = each(lambda s_, pr: s_ + pr[:, LANES:], S, prod)
        Pw = [pr[:, :LANES] for pr in prod]
        n *= 2
    S = each(lambda s_, p_: s_ + _dot(p_.astype(BF), bstack(s_)), S, Pw)
    WU = each(lambda s_, a_, u0: _dot(s_.astype(BF), jnp.concatenate([bstack(a_), bstack(u0)], axis=1)),
              S, A_c, AkV)

    def finish(y_, r_, k_, v_, rs, cs):
        mu = seg_sum(y_) * (1.0 / RWKV_HEAD_DIM)
        yc = y_ - mu
        var = seg_sum(yc * yc) * (1.0 / RWKV_HEAD_DIM)
        yn = yc * lax.rsqrt(var + LNX_EPS) * lnw_ref[:, cs] + lnb_ref[:, cs]
        bonus = seg_sum(r_ * k_ * rk_ref[:, cs]) * v_
        return ((yn + bonus) * g_ref[rs, cs]).astype(y_ref.dtype)

    P = [p_scr[p] for p in range(n_pairs)]
    for c in range(n_chunks):
        sel = lambda xs: xs[c * n_pairs:(c + 1) * n_pairs]
        Pb = bf(P)
        U = each(lambda wu, pb: _dot(wu[:, :LANES].astype(BF), pb) + wu[:, LANES:], sel(WU), Pb)
        U_st = each(bstack, U)
        y = each(lambda r_, mb, mk, pb, u, vs: _dot(jnp.concatenate([r_, mb, mk], axis=1),
                                                    jnp.concatenate([pb, u, vs], axis=0)),
                 sel(R_c), sel(Mrb), sel(Mrk), Pb, U_st, sel(V_st))
        P = each(lambda gc, p_, m, u, vs: gc * p_ + _dot(m, jnp.concatenate([u, vs], axis=0)),
                 sel(gcol), P, sel(BKpT), U_st, sel(V_st))
        out = each(finish, y, sel(r), sel(k), sel(v), sel(rows), sel(cols))
        for o, rs, cs in zip(out, sel(rows), sel(cols)):
            y_ref[rs, cs] = o
    for p in range(n_pairs):
        p_scr[p] = P[p]


def _wkv(r, k, v, lw, kkr, a, g, r_k, lnx_w, lnx_b, *, batch, seq, chunks_per_step=2):
    M, W = r.shape
    rows = chunks_per_step * WKV_CHUNK
    steps = seq // rows
    tok = pl.BlockSpec((rows, W), lambda b, c: (b * steps + c, 0))
    par = pl.BlockSpec((1, W), lambda b, c: (0, 0))
    return pl.pallas_call(
        _wkv_kernel,
        grid=(batch, steps),
        in_specs=[tok] * 7 + [par] * 3,
        out_specs=tok,
        out_shape=jax.ShapeDtypeStruct((M, W), BF),
        scratch_shapes=[pltpu.VMEM((W // LANES, LANES, LANES), F32)],
        compiler_params=_cparams("parallel", "arbitrary"),
        name="wkv",
    )(r, k, v, lw, kkr, a, g, r_k, lnx_w, lnx_b)


def _attn_kernel(slopes_ref, q_ref, k_ref, v_ref, o_ref, og_scr, lse_scr, *, seq):
    slot = pl.program_id(1)
    grp = pl.program_id(2)
    blk = ATTN_BLK
    scale = ATTN_HEAD_DIM ** -0.5
    qi = lax.broadcasted_iota(jnp.int32, (blk, 2 * blk), 0)
    kj = lax.broadcasted_iota(jnp.int32, (blk, 2 * blk), 1)
    rel = (qi + blk - kj).astype(F32)
    in_band = (kj >= qi) & (kj <= qi + blk)
    is_prev = kj < blk
    nb = ATTN_BLOCKS_PER_ITER
    each = lambda f, *xs: [f(*t) for t in zip(*xs)]

    def rows(start, d):
        if d == 1:
            return pl.ds(start, blk)
        return pl.ds(start, blk, stride=d)

    for gi, (window, d) in enumerate(ATTN_GROUPS):
        assert window // d == blk and seq % window == 0 and (seq // blk) % nb == 0

        @pl.when(grp == gi)
        def _(gi=gi, d=d):
            slope = slopes_ref[gi * ATTN_HEADS_PER_GROUP + slot]
            bias = jnp.where(in_band, -slope * float(d) * rel, NEG_BIG)
            span = blk * d

            def body(it, carry):
                ids = [it * nb + j for j in range(nb)]
                ns = [i // d for i in ids]
                cur = [rows(n * span + i % d, d) for n, i in zip(ns, ids)]
                prv = [rows(jnp.maximum(n - 1, 0) * span + i % d, d) for n, i in zip(ns, ids)]
                q = [q_ref[c, :].astype(BF) for c in cur]
                kcat = [jnp.concatenate([k_ref[p_, :], k_ref[c, :]], axis=0).astype(BF) for p_, c in zip(prv, cur)]
                vcat = [jnp.concatenate([v_ref[p_, :], v_ref[c, :]], axis=0).astype(BF) for p_, c in zip(prv, cur)]
                s = each(lambda q_, k_: _dot_nt(q_, k_) * scale + bias, q, kcat)
                s = each(lambda s_, n: jnp.where(is_prev & (n == 0), NEG_BIG, s_), s, ns)
                m = [jnp.max(s_, axis=1, keepdims=True) for s_ in s]
                p = each(lambda s_, m_: jnp.exp(s_ - m_), s, m)
                l = [jnp.sum(p_, axis=1, keepdims=True) for p_ in p]
                o = each(lambda p_, v_, l_: _dot(p_.astype(BF), v_) / l_, p, vcat, l)
                lse = each(lambda m_, l_: m_ + jnp.log(l_), m, l)
                for c, o_, e_ in zip(cur, o, lse):
                    og_scr[gi, c, :] = o_
                    lse_scr[gi, c, :] = jnp.broadcast_to(e_, (blk, LANES))
                return carry

            lax.fori_loop(0, seq // blk // nb, body, 0)

    @pl.when(grp == len(ATTN_GROUPS) - 1)
    def _():
        tile = 8 * blk

        def merge(t, carry):
            rs = pl.ds(pl.multiple_of(t * tile, tile), tile)
            lses = [lse_scr[g, rs, :] for g in range(len(ATTN_GROUPS))]
            top = functools.reduce(jnp.maximum, lses)
            w = [jnp.exp(e - top) for e in lses]
            num = sum(w_ * og_scr[g, rs, :] for g, w_ in enumerate(w))
            o_ref[rs, :] = (num / sum(w)).astype(o_ref.dtype)
            return carry

        lax.fori_loop(0, seq // tile, merge, 0)


def _attention(qkv, slopes, *, batch, seq):
    M = qkv.shape[0]
    E = ATTN_HEAD_DIM
    hpg = ATTN_HEADS_PER_GROUP

    def spec(which):
        first = COL_QKV // E + which * ATTN_HEADS
        return pl.BlockSpec((seq, E), lambda b, s, g: (b, first + g * hpg + s))

    return pl.pallas_call(
        functools.partial(_attn_kernel, seq=seq),
        grid=(batch, hpg, len(ATTN_GROUPS)),
        in_specs=[pl.BlockSpec(memory_space=pltpu.SMEM), spec(0), spec(1), spec(2)],
        out_specs=pl.BlockSpec((seq, E), lambda b, s, g: (b, s)),
        out_shape=jax.ShapeDtypeStruct((M, hpg * E), BF),
        scratch_shapes=[pltpu.VMEM((len(ATTN_GROUPS), seq, LANES), F32)] * 2,
        compiler_params=_cparams("parallel", "parallel", "arbitrary"),
        name="dilated_attn",
    )(slopes, qkv, qkv, qkv)


def _sgu_kernel(x_ref, lg_ref, lb_ref, ws_ref, bt_ref, o_ref):
    tm = x_ref.shape[0]
    CH = SGU_CHUNK
    GD = SGU_WIDTH // SGU_GROUPS
    u = _gelu_tanh(x_ref[:, :SGU_WIDTH])
    v = _gelu_tanh(x_ref[:, SGU_WIDTH:])
    mu = jnp.mean(v, axis=-1, keepdims=True)
    vc = v - mu
    var = jnp.mean(vc * vc, axis=-1, keepdims=True)
    vn = (vc * lax.rsqrt(var + LN_EPS) * lg_ref[...] + lb_ref[...]).astype(BF)
    ti = lax.broadcasted_iota(jnp.int32, (CH, CH), 0)
    si = lax.broadcasted_iota(jnp.int32, (CH, CH), 1)
    causal = ti >= si
    for gidx in range(SGU_GROUPS):
        w = jnp.where(causal, ws_ref[gidx], 0.0).astype(BF)
        bias = bt_ref[:, gidx:gidx + 1]
        cs = slice(gidx * GD, (gidx + 1) * GD)
        for c in range(tm // CH):
            rs = slice(c * CH, (c + 1) * CH)
            mixed = _dot(w, vn[rs, cs]) + bias
            o_ref[rs, cs] = (u[rs, cs] * mixed).astype(o_ref.dtype)


def _sgu(proj, ln_g, ln_b, w_s, b_t, *, tm=512):
    M = proj.shape[0]
    return pl.pallas_call(
        _sgu_kernel,
        grid=(M // tm,),
        in_specs=[
            pl.BlockSpec((tm, N_SGU), lambda i: (i, COL_SGU // N_SGU)),
            pl.BlockSpec((1, SGU_WIDTH), lambda i: (0, 0)),
            pl.BlockSpec((1, SGU_WIDTH), lambda i: (0, 0)),
            pl.BlockSpec(w_s.shape, lambda i: (0, 0, 0)),
            pl.BlockSpec(b_t.shape, lambda i: (0, 0)),
        ],
        out_specs=pl.BlockSpec((tm, SGU_WIDTH), lambda i: (i, 0)),
        out_shape=jax.ShapeDtypeStruct((M, SGU_WIDTH), BF),
        compiler_params=_cparams("parallel"),
        name="sgu",
    )(proj, ln_g, ln_b, w_s, b_t)


def _merge_kernel(h_ref, yr_ref, ya_ref, ys_ref, g0_ref, g1_ref, g2_ref, wr_ref, wa_ref, ws_ref,
                  wo_ref, pg_ref, o_ref, acc_scr):
    j = pl.program_id(1)

    @pl.when(j == 0)
    def _():
        acc_scr[...] = jnp.zeros_like(acc_scr)

    merged = (g0_ref[...] * _dot(yr_ref[...], wr_ref[...])
              + g1_ref[...] * _dot(ya_ref[...], wa_ref[...])
              + g2_ref[...] * _dot(ys_ref[...], ws_ref[...]))
    acc_scr[...] += _dot(merged.astype(BF), wo_ref[...])

    @pl.when(j == pl.num_programs(1) - 1)
    def _():
        o_ref[...] = h_ref[...] + _rms(acc_scr[...], pg_ref[...])


def _merge(h, yr, ya, ys, gates, wr, wa, ws, wo, post_g, *, tm=512, tn=1024):
    M, D = h.shape
    nj = D // tn
    act = lambda a: pl.BlockSpec((tm, a.shape[1]), lambda i, j: (i, 0))
    gate = lambda br: pl.BlockSpec((tm, tn), lambda i, j: (i, br * nj + j))
    wcol = lambda a: pl.BlockSpec((a.shape[0], tn), lambda i, j: (0, j))
    return pl.pallas_call(
        _merge_kernel,
        grid=(M // tm, nj),
        in_specs=[
            pl.BlockSpec((tm, D), lambda i, j: (i, 0)),
            act(yr), act(ya), act(ys), gate(0), gate(1), gate(2),
            wcol(wr), wcol(wa), wcol(ws),
            pl.BlockSpec((tn, D), lambda i, j: (j, 0)),
            pl.BlockSpec((1, D), lambda i, j: (0, 0)),
        ],
        out_specs=pl.BlockSpec((tm, D), lambda i, j: (i, 0)),
        out_shape=jax.ShapeDtypeStruct((M, D), F32),
        scratch_shapes=[pltpu.VMEM((tm, D), F32)],
        compiler_params=_cparams("parallel", "arbitrary"),
        name="merge_out",
    )(h, yr, ya, ys, gates, gates, gates, wr, wa, ws, wo, post_g)


def _alibi_slopes(n_heads):
    def geometric(n):
        start = 2.0 ** (-8.0 / n)
        return [start ** (i + 1) for i in range(n)]
    closest = 2 ** int(math.floor(math.log2(n_heads)))
    slopes = geometric(closest)
    if closest < n_heads:
        slopes += geometric(2 * closest)[0::2][: n_heads - closest]
    return np.array(sorted(slopes, reverse=True), dtype=np.float32)


def _pad_rows(x, before, total):
    return jnp.pad(x, ((before, total - before - x.shape[0]), (0, 0)))


def _w_in_job(w_in_t, extra_t, l, *, rows=256):
    n_in = w_in_t.shape[1]
    assert n_in - N_SHIFT == N_PROJ - COL_QKV and (COL_QKV - N_SHIFT) % SUBLANES == 0
    assert COL_QKV % rows == 0 and N_PROJ % rows == 0 and SEG_P % rows + LANES <= rows
    n_head = COL_QKV // rows

    def offset(s):
        return s * (rows // SUBLANES) - jnp.where(s < n_head, 0, (COL_QKV - N_SHIFT) // SUBLANES)

    return _CastJob(w_in_t, l, rows, N_PROJ // rows, offset, extra_t, SEG_P // rows, SEG_P % rows)


def kernel(x, ffn1_pre_g, ffn1_w_gu, ffn1_w_down, ffn1_post_g, mix_pre_g, w_in, shift_mu, decay_w0, decay_w2, iclr_a0, iclr_a2, gate_g2, k_k, k_a, r_k, lnx_w, lnx_b, vres_w1, vres_mu, vres_v0, vres_w2, sgu_ln_g, sgu_ln_b, sgu_w_s, sgu_b, w_b_rwkv, w_b_attn, w_b_sgu, w_out, mix_post_g, ffn2_pre_g, ffn2_w_gu, ffn2_w_down, ffn2_post_g):
    B, T, D = x.shape
    depth = w_in.shape[0]
    slopes = jnp.asarray(_alibi_slopes(ATTN_HEADS))
    row = lambda p: p.reshape(1, -1)
    ffn_jobs = lambda w_gu, w_down, l: [_plain_job(w_gu, l, 32), _plain_job(w_down, l, 128)]

    w_in_t = jnp.swapaxes(w_in, 1, 2)
    h = x.reshape(B * T, D)
    v_first = None
    wgu, wd = (_cast(job) for job in ffn_jobs(ffn1_w_gu, ffn1_w_down, 0))
    for l in range(depth):
        if l == 0:
            extra_t = jnp.zeros((LANES, D), F32)
            extra_mu = jnp.zeros((LANES,), F32)
        else:
            extra_t = _pad_rows(vres_w1[l - 1].T, 0, LANES)
            extra_mu = jnp.pad(vres_mu[l - 1], (0, LANES - VRES_LORA))
        jobs = [_w_in_job(w_in_t, extra_t, l)] + [_plain_job(w, l, 64)
                                                    for w in (w_b_rwkv, w_b_attn, w_b_sgu, w_out)]
        h, w_proj, wr, wa, ws, wo = _ffn(h, row(ffn1_pre_g[l]), wgu, wd, row(ffn1_post_g[l]), jobs,
                                         tf=FFN_TF_CARRIER)

        mu = jnp.concatenate([jnp.pad(shift_mu[l], (0, SEG_P - N_SHIFT)), extra_mu]).reshape(1, -1)
        proj, gates, wgu, wd = _norm_matmul(h, row(mix_pre_g[l]), w_proj,
                                            ffn_jobs(ffn2_w_gu, ffn2_w_down, l))

        vres = None
        if l > 0:
            vres = (v_first, row(vres_v0[l - 1]), _pad_rows(vres_w2[l - 1], 0, LANES).astype(BF))
        r, k, v, lw, kkr, a, g = _rwkv_prep(
            proj, mu, row(decay_w0[l]), _pad_rows(decay_w2[l], 0, LANES).astype(BF),
            row(iclr_a0[l]), _pad_rows(iclr_a2[l], DECAY_LORA, LANES).astype(BF),
            _pad_rows(gate_g2[l], 0, 2 * LANES).astype(BF), row(k_k[l]), row(k_a[l]), vres, seq=T)
        if l == 0:
            v_first = v
        y_rwkv = _wkv(r, k, v, lw, kkr, a, g, row(r_k[l]), row(lnx_w[l]), row(lnx_b[l]),
                      batch=B, seq=T)

        y_attn = _attention(proj, slopes, batch=B, seq=T)
        y_sgu = _sgu(proj, row(sgu_ln_g[l]), row(sgu_ln_b[l]), sgu_w_s[l], sgu_b[l].T)

        h = _merge(h, y_rwkv, y_attn, y_sgu, gates, wr, wa, ws, wo, row(mix_post_g[l]))

        jobs = ffn_jobs(ffn1_w_gu, ffn1_w_down, l + 1) if l + 1 < depth else []
        h, *cast = _ffn(h, row(ffn2_pre_g[l]), wgu, wd, row(ffn2_post_g[l]), jobs)
        if cast:
            wgu, wd = cast
    return h.reshape(B, T, D)
```

```python
import functools
import math
from typing import Callable, NamedTuple, Optional

import numpy as np
import jax
import jax.numpy as jnp
from jax import lax
from jax.experimental import pallas as pl
from jax.experimental.pallas import tpu as pltpu

BF = jnp.bfloat16
F32 = jnp.float32

D_MODEL = 2048
RWKV_HEAD_DIM = 64
RWKV_WIDTH = D_MODEL // 2
DECAY_LORA = 64
ICLR_LORA = 64
VRES_LORA = 32
GATE_LORA = 160
LNX_EPS = 64e-5
ATTN_GROUPS = ((128, 1), (512, 4), (2048, 16))
ATTN_HEADS_PER_GROUP = 4
ATTN_HEAD_DIM = 128
ATTN_HEADS = len(ATTN_GROUPS) * ATTN_HEADS_PER_GROUP
ATTN_BLK = 128
ATTN_BLOCKS_PER_ITER = 4
SGU_CHUNK = 128
SGU_GROUPS = 4
SGU_WIDTH = D_MODEL // 4
RMS_EPS = 1e-6
LN_EPS = 1e-5
N_BRANCHES = 3

LANES = 128
SUBLANES = 8
WKV_CHUNK = 64
FFN_TF = 512
NEG_BIG = -1e30

N_SHIFT = 3 * RWKV_WIDTH + DECAY_LORA + ICLR_LORA + GATE_LORA
SEG_R, SEG_K, SEG_V = 0, RWKV_WIDTH, 2 * RWKV_WIDTH
SEG_WA = 3 * RWKV_WIDTH
SEG_G = SEG_WA + 128
SEG_P = SEG_G + 256
N_QKV = 3 * ATTN_HEADS * ATTN_HEAD_DIM
N_SGU = 2 * SGU_WIDTH
N_GATE = N_BRANCHES * D_MODEL
COL_SHIFT = 0
COL_QKV = SEG_P + LANES
COL_SGU = COL_QKV + N_QKV
COL_GATE = COL_SGU + N_SGU
N_PROJ = COL_GATE + N_GATE

VMEM_LIMIT = 60 * 1024 * 1024


def _cparams(*sem):
    return pltpu.CompilerParams(dimension_semantics=sem, vmem_limit_bytes=VMEM_LIMIT)


def _dot(a, b):
    return jnp.dot(a, b, preferred_element_type=F32)


def _dot_nt(a, b):
    return lax.dot_general(a, b, (((1,), (1,)), ((), ())), preferred_element_type=F32)


def _rms(x, g):
    return x * lax.rsqrt(jnp.mean(x * x, axis=-1, keepdims=True) + RMS_EPS) * g


class _CastJob(NamedTuple):
    src: jax.Array
    layer: int
    rows: int
    n_blocks: int
    offset: Callable
    patch: Optional[jax.Array] = None
    patch_block: int = 0
    patch_row: int = 0


class _CastMeta(NamedTuple):
    n_blocks: int
    has_patch: bool
    patch_block: int
    patch_row: int


def _plain_job(src, layer, rows):
    R = src.shape[1]
    assert R % rows == 0 and rows % (2 * SUBLANES) == 0
    return _CastJob(src, layer, rows, R // rows, lambda s: s * (rows // SUBLANES))


def _job_plumbing(jobs, step_of, n_steps):
    in_specs, args, out_specs, out_shapes, metas = [], [], [], [], []
    for job in jobs:
        assert job.n_blocks <= n_steps, "carrier kernel has too few grid steps for this cast job"
        C = job.src.shape[2]
        blk = lambda *g, job=job: jnp.minimum(step_of(*g), job.n_blocks - 1)
        in_specs.append(pl.BlockSpec(
            (None, pl.Element(job.rows), pl.Element(C)),
            lambda *g, job=job, blk=blk: (job.layer, job.offset(blk(*g)) * SUBLANES, 0)))
        args.append(job.src)
        if job.patch is not None:
            in_specs.append(pl.BlockSpec(job.patch.shape, lambda *g: (0, 0)))
            args.append(job.patch)
        out_specs.append(pl.BlockSpec((job.rows, C), lambda *g, blk=blk: (blk(*g), 0)))
        out_shapes.append(jax.ShapeDtypeStruct((job.n_blocks * job.rows, C), BF))
        metas.append(_CastMeta(job.n_blocks, job.patch is not None, job.patch_block, job.patch_row))
    return in_specs, args, out_specs, out_shapes, tuple(metas)


def _run_cast_jobs(step, metas, in_refs, out_refs):
    in_refs = iter(in_refs)
    for meta, dst in zip(metas, out_refs):
        src = next(in_refs)
        patch = next(in_refs) if meta.has_patch else None

        @pl.when(step < meta.n_blocks)
        def _(meta=meta, src=src, patch=patch, dst=dst):
            dst[...] = src[...].astype(BF)
            if patch is not None:
                @pl.when(step == meta.patch_block)
                def _():
                    dst[pl.ds(meta.patch_row, LANES), :] = patch[...].astype(BF)


def _n_job_inputs(metas):
    return sum(2 if m.has_patch else 1 for m in metas)


def _cast_kernel(*refs, metas):
    n_in = _n_job_inputs(metas)
    _run_cast_jobs(pl.program_id(0), metas, refs[:n_in], refs[n_in:])


def _cast(job):
    in_specs, args, out_specs, out_shapes, metas = _job_plumbing([job], lambda s: s, job.n_blocks)
    return pl.pallas_call(
        functools.partial(_cast_kernel, metas=metas),
        grid=(job.n_blocks,),
        in_specs=in_specs,
        out_specs=out_specs,
        out_shape=out_shapes,
        compiler_params=_cparams("arbitrary"),
        name="cast",
    )(*args)[0]


def _ffn_kernel(*refs, tail, metas):
    n_in = _n_job_inputs(metas)
    h_ref, pre_g_ref, wg_ref, wu_ref, wd_ref, post_g_ref = refs[:6]
    job_in = refs[6:6 + n_in]
    o_ref = refs[6 + n_in]
    job_out = refs[7 + n_in:7 + n_in + len(metas)]
    xn_scr, acc_scr = refs[7 + n_in + len(metas):]
    i, f = pl.program_id(0), pl.program_id(1)
    last = pl.num_programs(1) - 1
    tf = wd_ref.shape[0]

    @pl.when(f == 0)
    def _():
        xn_scr[...] = _rms(h_ref[...], pre_g_ref[...]).astype(BF)
        acc_scr[...] = jnp.zeros_like(acc_scr)

    def accumulate(lo):
        xn = xn_scr[...]
        g = _dot(xn, wg_ref[:, lo:])
        u = _dot(xn, wu_ref[:, lo:])
        a = (g * jax.nn.sigmoid(g) * u).astype(BF)
        acc_scr[...] += _dot(a, wd_ref[lo:, :])

    @pl.when(f < last)
    def _():
        accumulate(0)

    @pl.when(f == last)
    def _():
        accumulate(tf - tail)
        o_ref[...] = h_ref[...] + 0.5 * _rms(acc_scr[...], post_g_ref[...])

    _run_cast_jobs(i * pl.num_programs(1) + f, metas, job_in, job_out)


def _ffn(h, pre_g, wgu, wd, post_g, jobs=(), *, tm=512, tf=FFN_TF):
    M, D = h.shape
    F = wd.shape[0]
    nf = -(-F // tf)
    tail = F - (nf - 1) * tf
    assert wgu.shape == (D, 2 * F) and F % LANES == 0 and tf % LANES == 0 and tail % LANES == 0
    start = lambda f, base=0: (base // LANES + jnp.minimum(f * (tf // LANES), (F - tf) // LANES)) * LANES
    j_in, j_args, j_out, j_shapes, metas = _job_plumbing(jobs, lambda i, f: i * nf + f, M // tm * nf)
    return pl.pallas_call(
        functools.partial(_ffn_kernel, tail=tail, metas=metas),
        grid=(M // tm, nf),
        in_specs=[
            pl.BlockSpec((tm, D), lambda i, f: (i, 0)),
            pl.BlockSpec((1, D), lambda i, f: (0, 0)),
            pl.BlockSpec((pl.Element(D), pl.Element(tf)), lambda i, f: (0, start(f))),
            pl.BlockSpec((pl.Element(D), pl.Element(tf)), lambda i, f: (0, start(f, F))),
            pl.BlockSpec((pl.Element(tf), pl.Element(D)), lambda i, f: (start(f), 0)),
            pl.BlockSpec((1, D), lambda i, f: (0, 0)),
        ] + j_in,
        out_specs=[pl.BlockSpec((tm, D), lambda i, f: (i, 0))] + j_out,
        out_shape=[jax.ShapeDtypeStruct((M, D), F32)] + j_shapes,
        scratch_shapes=[pltpu.VMEM((tm, D), BF), pltpu.VMEM((tm, D), F32)],
        compiler_params=_cparams("arbitrary", "arbitrary"),
        name="ffn",
    )(h, pre_g, wgu, wgu, wd, post_g, *j_args)


def _gelu_tanh(x):
    c = math.sqrt(2.0 / math.pi)
    return x * (0.5 * (1.0 + jnp.tanh(c * (x + 0.044715 * (x * x * x)))))


def _nmm_kernel(*refs, n_plain, metas):
    n_in = _n_job_inputs(metas)
    h_ref, g_ref, w_ref = refs[:3]
    job_in = refs[3:3 + n_in]
    o_ref, gate_ref = refs[3 + n_in:5 + n_in]
    job_out = refs[5 + n_in:5 + n_in + len(metas)]
    xn_scr = refs[-1]
    i, j = pl.program_id(0), pl.program_id(1)

    @pl.when(j == 0)
    def _():
        xn_scr[...] = _rms(h_ref[...], g_ref[...]).astype(BF)

    @pl.when(j < n_plain)
    def _():
        o_ref[...] = _dot_nt(xn_scr[...], w_ref[...])

    @pl.when(j >= n_plain)
    def _():
        gate_ref[...] = jax.nn.sigmoid(_dot_nt(xn_scr[...], w_ref[...])).astype(gate_ref.dtype)

    _run_cast_jobs(i * pl.num_programs(1) + j, metas, job_in, job_out)


def _norm_matmul(h, g, w_t, jobs=(), *, tm=1024, tn=1024):
    M, D = h.shape
    N = w_t.shape[0]
    nj = N // tn
    n_plain = COL_GATE // tn
    assert N == N_PROJ and nj * tn == N and n_plain * tn == COL_GATE
    j_in, j_args, j_out, j_shapes, metas = _job_plumbing(jobs, lambda i, j: i * nj + j, M // tm * nj)
    return pl.pallas_call(
        functools.partial(_nmm_kernel, n_plain=n_plain, metas=metas),
        grid=(M // tm, nj),
        in_specs=[
            pl.BlockSpec((tm, D), lambda i, j: (i, 0)),
            pl.BlockSpec((1, D), lambda i, j: (0, 0)),
            pl.BlockSpec((tn, D), lambda i, j: (j, 0)),
        ] + j_in,
        out_specs=[pl.BlockSpec((tm, tn), lambda i, j: (i, jnp.minimum(j, n_plain - 1))),
                   pl.BlockSpec((tm, tn), lambda i, j: (i, jnp.maximum(j - n_plain, 0)))] + j_out,
        out_shape=[jax.ShapeDtypeStruct((M, COL_GATE), F32),
                   jax.ShapeDtypeStruct((M, N - COL_GATE), BF)] + j_shapes,
        scratch_shapes=[pltpu.VMEM((tm, D), BF)],
        compiler_params=_cparams("arbitrary", "arbitrary"),
        name="in_proj",
    )(h, g, w_t, *j_args)


def _prep_kernel(*refs, seq, has_vres):
    if has_vres:
        (ps_ref, prev_ref, mu_ref, w0_ref, w2_ref, a0_ref, a2_ref, g2_ref, kk_ref, ka_ref,
         vf_ref, v0_ref, vw2_ref,
         r_o, k_o, v_o, lw_o, kk_o, a_o, g_o) = refs
    else:
        (ps_ref, prev_ref, mu_ref, w0_ref, w2_ref, a0_ref, a2_ref, g2_ref, kk_ref, ka_ref,
         r_o, k_o, v_o, lw_o, kk_o, a_o, g_o) = refs
    tm = ps_ref.shape[0]
    W = RWKV_WIDTH
    x = ps_ref[...]
    at_seq_start = (pl.program_id(0) * tm) % seq == 0
    last_prev = jnp.where(at_seq_start, 0.0, prev_ref[7:8, :])
    row = lax.broadcasted_iota(jnp.int32, x.shape, 0)
    prev = jnp.where(row == 0, last_prev, pltpu.roll(x, 1, axis=0))
    xs = x + (prev - x) * mu_ref[...]

    xr = xs[:, SEG_R:SEG_R + W]
    xk = xs[:, SEG_K:SEG_K + W]
    xv = xs[:, SEG_V:SEG_V + W]
    xwa = xs[:, SEG_WA:SEG_WA + 128]
    xg = xs[:, SEG_G:SEG_G + 256]

    z = w0_ref[...] + _dot(jnp.tanh(xwa).astype(BF), w2_ref[...])
    w_log = -jax.nn.softplus(-z) - 0.5
    lw_o[...] = -jnp.exp(w_log)
    a = jax.nn.sigmoid(a0_ref[...] + _dot(xwa.astype(BF), a2_ref[...]))
    a_o[...] = a
    g_o[...] = _dot(jax.nn.sigmoid(xg).astype(BF), g2_ref[...])
    kk_o[...] = xk * kk_ref[...]
    k_o[...] = xk * (1.0 + (a - 1.0) * ka_ref[...])
    r_o[...] = xr
    if has_vres:
        pv = xs[:, SEG_P:SEG_P + 128]
        mix = jax.nn.sigmoid(v0_ref[...] + _dot(pv.astype(BF), vw2_ref[...]))
        v_o[...] = xv + (vf_ref[...] - xv) * mix
    else:
        v_o[...] = xv


def _rwkv_prep(ps, mu, w0, w2, a0, a2, g2, k_k, k_a, vres, *, seq, tm=256):
    M = ps.shape[0]
    NA = COL_QKV
    W = RWKV_WIDTH
    row = lambda n: pl.BlockSpec((1, n), lambda i: (0, 0))
    full = lambda a: pl.BlockSpec(a.shape, lambda i: (0, 0))
    tok = pl.BlockSpec((tm, W), lambda i: (i, 0))
    in_specs = [
        pl.BlockSpec((tm, NA), lambda i: (i, 0)),
        pl.BlockSpec((8, NA), lambda i: (jnp.maximum(i * (tm // 8) - 1, 0), 0)),
        row(NA), row(W), full(w2), row(W), full(a2), full(g2), row(W), row(W),
    ]
    args = [ps, ps, mu, w0, w2, a0, a2, g2, k_k, k_a]
    if vres is not None:
        v_first, v0, vw2 = vres
        in_specs += [tok, row(W), full(vw2)]
        args += [v_first, v0, vw2]
    return pl.pallas_call(
        functools.partial(_prep_kernel, seq=seq, has_vres=vres is not None),
        grid=(M // tm,),
        in_specs=in_specs,
        out_specs=[tok] * 7,
        out_shape=[jax.ShapeDtypeStruct((M, W), F32)] * 7,
        compiler_params=_cparams("parallel"),
        name="rwkv_prep",
    )(*args)


def _wkv_kernel(r_ref, k_ref, v_ref, lw_ref, kk_ref, a_ref, g_ref, rk_ref, lnw_ref, lnb_ref,
                y_ref, p_scr):
    C = WKV_CHUNK
    C2 = 2 * C

    @pl.when(pl.program_id(1) == 0)
    def _():
        p_scr[...] = jnp.zeros_like(p_scr)

    lane = lax.broadcasted_iota(jnp.int32, (C, LANES), 1)
    lo = lane < RWKV_HEAD_DIM
    ti = lax.broadcasted_iota(jnp.int32, (C, C), 0)
    tj = lax.broadcasted_iota(jnp.int32, (C, C), 1)
    tri = jnp.where(ti >= tj, 1.0, 0.0).astype(BF)
    tq = lax.broadcasted_iota(jnp.int32, (C, LANES), 0)
    sq = lane % C
    strict = tq > sq
    incl = tq >= sq
    eye_c = jnp.where(tq == sq, 1.0, 0.0)
    si = lax.broadcasted_iota(jnp.int32, (C2, LANES), 0)
    sj = lax.broadcasted_iota(jnp.int32, (C2, LANES), 1)
    eye = si == sj

    def seg_sum(x):
        s_lo = jnp.sum(jnp.where(lo, x, 0.0), axis=1, keepdims=True)
        s_hi = jnp.sum(jnp.where(lo, 0.0, x), axis=1, keepdims=True)
        return jnp.where(lo, s_lo, s_hi)

    def stack(x):
        return jnp.concatenate([jnp.where(lo, x, 0.0), jnp.where(lo, 0.0, x)], axis=0)

    def bstack(x):
        return stack(x).astype(BF)

    n_pairs = r_ref.shape[1] // LANES
    n_chunks = r_ref.shape[0] // C
    items = [(c, p) for c in range(n_chunks) for p in range(n_pairs)]
    rows = [slice(c * C, (c + 1) * C) for c, _ in items]
    cols = [slice(p * LANES, (p + 1) * LANES) for _, p in items]
    each = lambda f, *xs: [f(*t) for t in zip(*xs)]
    bf = lambda xs: [x.astype(BF) for x in xs]
    load = lambda ref: [ref[rs, cs] for rs, cs in zip(rows, cols)]

    r, k, v, lw, kkr, a = (load(ref) for ref in (r_ref, k_ref, v_ref, lw_ref, kk_ref, a_ref))

    def cumsum(x):
        hi = x.astype(BF)
        both = _dot(tri, jnp.concatenate([hi, (x - hi.astype(F32)).astype(BF)], axis=1))
        return both[:, :LANES] + both[:, LANES:]

    cum = each(cumsum, lw)
    tot = [c[C - 1:C, :] for c in cum]
    kk = each(lambda x: x / jnp.maximum(jnp.sqrt(seg_sum(x * x)), 1e-12), kkr)
    kka = each(lambda x, y: x * y, kk, a)
    A_c = each(lambda x, c, w: -x * jnp.exp(c - w), kk, cum, lw)
    R_c = bf(each(lambda x, c: x * jnp.exp(c), r, cum))
    AR = each(lambda a_, r_: jnp.concatenate([a_.astype(BF), r_], axis=0), A_c, R_c)
    BK_st = each(lambda x, y, c: jnp.concatenate([bstack(x * jnp.exp(-c)), bstack(y * jnp.exp(-c))], axis=0),
                 kka, k, cum)
    V_st = each(bstack, v)
    BKpT = bf(each(lambda x, y, c, t: jnp.concatenate([stack(x * jnp.exp(t - c)).T,
                                                       stack(y * jnp.exp(t - c)).T], axis=1),
                   kka, k, cum, tot))
    gcol = [jnp.exp(jnp.sum(jnp.where(eye, jnp.broadcast_to(t, (C2, LANES)), 0.0), axis=1, keepdims=True))
            for t in tot]

    G = each(_dot_nt, AR, BK_st)
    Aab = [jnp.where(strict, x[:C, :LANES], 0.0) for x in G]
    Aak = bf([jnp.where(strict, x[:C, LANES:], 0.0) for x in G])
    Mrb = bf([jnp.where(incl, x[C:, :LANES], 0.0) for x in G])
    Mrk = bf([jnp.where(incl, x[C:, LANES:], 0.0) for x in G])
    AkV = each(_dot, Aak, V_st)

    S = [eye_c + x for x in Aab]
    Pw = each(lambda x: _dot(x.astype(BF), bstack(x)), Aab)
    n = 2
    while 2 * n < C:
        prod = each(lambda p_, s_: _dot(p_.astype(BF), jnp.concatenate([bstack(p_), bstack(s_)], axis=1)),
                    Pw, S)
        S = each(lambda s_, pr: s_ + pr[:, LANES:], S, prod)
        Pw = [pr[:, :LANES] for pr in prod]
        n *= 2
    S = each(lambda s_, p_: s_ + _dot(p_.astype(BF), bstack(s_)), S, Pw)
    WU = each(lambda s_, a_, u0: _dot(s_.astype(BF), jnp.concatenate([bstack(a_), bstack(u0)], axis=1)),
              S, A_c, AkV)

    def finish(y_, r_, k_, v_, rs, cs):
        mu = seg_sum(y_) * (1.0 / RWKV_HEAD_DIM)
        yc = y_ - mu
        var = seg_sum(yc * yc) * (1.0 / RWKV_HEAD_DIM)
        yn = yc * lax.rsqrt(var + LNX_EPS) * lnw_ref[:, cs] + lnb_ref[:, cs]
        bonus = seg_sum(r_ * k_ * rk_ref[:, cs]) * v_
        return ((yn + bonus) * g_ref[rs, cs]).astype(y_ref.dtype)

    P = [p_scr[p] for p in range(n_pairs)]
    for c in range(n_chunks):
        sel = lambda xs: xs[c * n_pairs:(c + 1) * n_pairs]
        Pb = bf(P)
        U = each(lambda wu, pb: _dot(wu[:, :LANES].astype(BF), pb) + wu[:, LANES:], sel(WU), Pb)
        U_st = each(bstack, U)
        y = each(lambda r_, mb, mk, pb, u, vs: _dot(jnp.concatenate([r_, mb, mk], axis=1),
                                                    jnp.concatenate([pb, u, vs], axis=0)),
                 sel(R_c), sel(Mrb), sel(Mrk), Pb, U_st, sel(V_st))
        P = each(lambda gc, p_, m, u, vs: gc * p_ + _dot(m, jnp.concatenate([u, vs], axis=0)),
                 sel(gcol), P, sel(BKpT), U_st, sel(V_st))
        out = each(finish, y, sel(r), sel(k), sel(v), sel(rows), sel(cols))
        for o, rs, cs in zip(out, sel(rows), sel(cols)):
            y_ref[rs, cs] = o
    for p in range(n_pairs):
        p_scr[p] = P[p]


def _wkv(r, k, v, lw, kkr, a, g, r_k, lnx_w, lnx_b, *, batch, seq, chunks_per_step=2):
    M, W = r.shape
    rows = chunks_per_step * WKV_CHUNK
    steps = seq // rows
    tok = pl.BlockSpec((rows, W), lambda b, c: (b * steps + c, 0))
    par = pl.BlockSpec((1, W), lambda b, c: (0, 0))
    return pl.pallas_call(
        _wkv_kernel,
        grid=(batch, steps),
        in_specs=[tok] * 7 + [par] * 3,
        out_specs=tok,
        out_shape=jax.ShapeDtypeStruct((M, W), BF),
        scratch_shapes=[pltpu.VMEM((W // LANES, LANES, LANES), F32)],
        compiler_params=_cparams("parallel", "arbitrary"),
        name="wkv",
    )(r, k, v, lw, kkr, a, g, r_k, lnx_w, lnx_b)


def _attn_kernel(slopes_ref, q_ref, k_ref, v_ref, o_ref, og_scr, lse_scr, *, seq):
    slot = pl.program_id(1)
    grp = pl.program_id(2)
    blk = ATTN_BLK
    scale = ATTN_HEAD_DIM ** -0.5
    qi = lax.broadcasted_iota(jnp.int32, (blk, 2 * blk), 0)
    kj = lax.broadcasted_iota(jnp.int32, (blk, 2 * blk), 1)
    rel = (qi + blk - kj).astype(F32)
    in_band = (kj >= qi) & (kj <= qi + blk)
    is_prev = kj < blk
    nb = ATTN_BLOCKS_PER_ITER
    each = lambda f, *xs: [f(*t) for t in zip(*xs)]

    def rows(start, d):
        if d == 1:
            return pl.ds(start, blk)
        return pl.ds(start, blk, stride=d)

    for gi, (window, d) in enumerate(ATTN_GROUPS):
        assert window // d == blk and seq % window == 0 and (seq // blk) % nb == 0

        @pl.when(grp == gi)
        def _(gi=gi, d=d):
            slope = slopes_ref[gi * ATTN_HEADS_PER_GROUP + slot]
            bias = jnp.where(in_band, -slope * float(d) * rel, NEG_BIG)
            span = blk * d

            def body(it, carry):
                ids = [it * nb + j for j in range(nb)]
                ns = [i // d for i in ids]
                cur = [rows(n * span + i % d, d) for n, i in zip(ns, ids)]
                prv = [rows(jnp.maximum(n - 1, 0) * span + i % d, d) for n, i in zip(ns, ids)]
                q = [q_ref[c, :].astype(BF) for c in cur]
                kcat = [jnp.concatenate([k_ref[p_, :], k_ref[c, :]], axis=0).astype(BF) for p_, c in zip(prv, cur)]
                vcat = [jnp.concatenate([v_ref[p_, :], v_ref[c, :]], axis=0).astype(BF) for p_, c in zip(prv, cur)]
                s = each(lambda q_, k_: _dot_nt(q_, k_) * scale + bias, q, kcat)
                s = each(lambda s_, n: jnp.where(is_prev & (n == 0), NEG_BIG, s_), s, ns)
                m = [jnp.max(s_, axis=1, keepdims=True) for s_ in s]
                p = each(lambda s_, m_: jnp.exp(s_ - m_), s, m)
                l = [jnp.sum(p_, axis=1, keepdims=True) for p_ in p]
                o = each(lambda p_, v_, l_: _dot(p_.astype(BF), v_) / l_, p, vcat, l)
                lse = each(lambda m_, l_: m_ + jnp.log(l_), m, l)
                for c, o_, e_ in zip(cur, o, lse):
                    og_scr[gi, c, :] = o_
                    lse_scr[gi, c, :] = jnp.broadcast_to(e_, (blk, LANES))
                return carry

            lax.fori_loop(0, seq // blk // nb, body, 0)

    @pl.when(grp == len(ATTN_GROUPS) - 1)
    def _():
        tile = 8 * blk

        def merge(t, carry):
            rs = pl.ds(pl.multiple_of(t * tile, tile), tile)
            lses = [lse_scr[g, rs, :] for g in range(len(ATTN_GROUPS))]
            top = functools.reduce(jnp.maximum, lses)
            w = [jnp.exp(e - top) for e in lses]
            num = sum(w_ * og_scr[g, rs, :] for g, w_ in enumerate(w))
            o_ref[rs, :] = (num / sum(w)).astype(o_ref.dtype)
            return carry

        lax.fori_loop(0, seq // tile, merge, 0)


def _attention(qkv, slopes, *, batch, seq):
    M = qkv.shape[0]
    E = ATTN_HEAD_DIM
    hpg = ATTN_HEADS_PER_GROUP

    def spec(which):
        first = COL_QKV // E + which * ATTN_HEADS
        return pl.BlockSpec((seq, E), lambda b, s, g: (b, first + g * hpg + s))

    return pl.pallas_call(
        functools.partial(_attn_kernel, seq=seq),
        grid=(batch, hpg, len(ATTN_GROUPS)),
        in_specs=[pl.BlockSpec(memory_space=pltpu.SMEM), spec(0), spec(1), spec(2)],
        out_specs=pl.BlockSpec((seq, E), lambda b, s, g: (b, s)),
        out_shape=jax.ShapeDtypeStruct((M, hpg * E), BF),
        scratch_shapes=[pltpu.VMEM((len(ATTN_GROUPS), seq, LANES), F32)] * 2,
        compiler_params=_cparams("parallel", "parallel", "arbitrary"),
        name="dilated_attn",
    )(slopes, qkv, qkv, qkv)


def _sgu_kernel(x_ref, lg_ref, lb_ref, ws_ref, bt_ref, o_ref):
    tm = x_ref.shape[0]
    CH = SGU_CHUNK
    GD = SGU_WIDTH // SGU_GROUPS
    u = _gelu_tanh(x_ref[:, :SGU_WIDTH])
    v = _gelu_tanh(x_ref[:, SGU_WIDTH:])
    mu = jnp.mean(v, axis=-1, keepdims=True)
    vc = v - mu
    var = jnp.mean(vc * vc, axis=-1, keepdims=True)
    vn = (vc * lax.rsqrt(var + LN_EPS) * lg_ref[...] + lb_ref[...]).astype(BF)
    ti = lax.broadcasted_iota(jnp.int32, (CH, CH), 0)
    si = lax.broadcasted_iota(jnp.int32, (CH, CH), 1)
    causal = ti >= si
    for gidx in range(SGU_GROUPS):
        w = jnp.where(causal, ws_ref[gidx], 0.0).astype(BF)
        bias = bt_ref[:, gidx:gidx + 1]
        cs = slice(gidx * GD, (gidx + 1) * GD)
        for c in range(tm // CH):
            rs = slice(c * CH, (c + 1) * CH)
            mixed = _dot(w, vn[rs, cs]) + bias
            o_ref[rs, cs] = (u[rs, cs] * mixed).astype(o_ref.dtype)


def _sgu(proj, ln_g, ln_b, w_s, b_t, *, tm=512):
    M = proj.shape[0]
    return pl.pallas_call(
        _sgu_kernel,
        grid=(M // tm,),
        in_specs=[
            pl.BlockSpec((tm, N_SGU), lambda i: (i, COL_SGU // N_SGU)),
            pl.BlockSpec((1, SGU_WIDTH), lambda i: (0, 0)),
            pl.BlockSpec((1, SGU_WIDTH), lambda i: (0, 0)),
            pl.BlockSpec(w_s.shape, lambda i: (0, 0, 0)),
            pl.BlockSpec(b_t.shape, lambda i: (0, 0)),
        ],
        out_specs=pl.BlockSpec((tm, SGU_WIDTH), lambda i: (i, 0)),
        out_shape=jax.ShapeDtypeStruct((M, SGU_WIDTH), BF),
        compiler_params=_cparams("parallel"),
        name="sgu",
    )(proj, ln_g, ln_b, w_s, b_t)


def _merge_kernel(h_ref, yr_ref, ya_ref, ys_ref, g0_ref, g1_ref, g2_ref, wr_ref, wa_ref, ws_ref,
                  wo_ref, pg_ref, o_ref, acc_scr):
    j = pl.program_id(1)

    @pl.when(j == 0)
    def _():
        acc_scr[...] = jnp.zeros_like(acc_scr)

    merged = (g0_ref[...] * _dot(yr_ref[...], wr_ref[...])
              + g1_ref[...] * _dot(ya_ref[...], wa_ref[...])
              + g2_ref[...] * _dot(ys_ref[...], ws_ref[...]))
    acc_scr[...] += _dot(merged.astype(BF), wo_ref[...])

    @pl.when(j == pl.num_programs(1) - 1)
    def _():
        o_ref[...] = h_ref[...] + _rms(acc_scr[...], pg_ref[...])


def _merge(h, yr, ya, ys, gates, wr, wa, ws, wo, post_g, *, tm=512, tn=1024):
    M, D = h.shape
    nj = D // tn
    act = lambda a: pl.BlockSpec((tm, a.shape[1]), lambda i, j: (i, 0))
    gate = lambda br: pl.BlockSpec((tm, tn), lambda i, j: (i, br * nj + j))
    wcol = lambda a: pl.BlockSpec((a.shape[0], tn), lambda i, j: (0, j))
    return pl.pallas_call(
        _merge_kernel,
        grid=(M // tm, nj),
        in_specs=[
            pl.BlockSpec((tm, D), lambda i, j: (i, 0)),
            act(yr), act(ya), act(ys), gate(0), gate(1), gate(2),
            wcol(wr), wcol(wa), wcol(ws),
            pl.BlockSpec((tn, D), lambda i, j: (j, 0)),
            pl.BlockSpec((1, D), lambda i, j: (0, 0)),
        ],
        out_specs=pl.BlockSpec((tm, D), lambda i, j: (i, 0)),
        out_shape=jax.ShapeDtypeStruct((M, D), F32),
        scratch_shapes=[pltpu.VMEM((tm, D), F32)],
        compiler_params=_cparams("parallel", "arbitrary"),
        name="merge_out",
    )(h, yr, ya, ys, gates, gates, gates, wr, wa, ws, wo, post_g)


def _alibi_slopes(n_heads):
    def geometric(n):
        start = 2.0 ** (-8.0 / n)
        return [start ** (i + 1) for i in range(n)]
    closest = 2 ** int(math.floor(math.log2(n_heads)))
    slopes = geometric(closest)
    if closest < n_heads:
        slopes += geometric(2 * closest)[0::2][: n_heads - closest]
    return np.array(sorted(slopes, reverse=True), dtype=np.float32)


def _pad_rows(x, before, total):
    return jnp.pad(x, ((before, total - before - x.shape[0]), (0, 0)))


def _w_in_job(w_in_t, extra_t, l, *, rows=128):
    n_in = w_in_t.shape[1]
    assert n_in - N_SHIFT == N_PROJ - COL_QKV and (COL_QKV - N_SHIFT) % SUBLANES == 0
    assert COL_QKV % rows == 0 and N_PROJ % rows == 0 and SEG_P % rows + LANES <= rows
    n_head = COL_QKV // rows

    def offset(s):
        return s * (rows // SUBLANES) - jnp.where(s < n_head, 0, (COL_QKV - N_SHIFT) // SUBLANES)

    return _CastJob(w_in_t, l, rows, N_PROJ // rows, offset, extra_t, SEG_P // rows, SEG_P % rows)


def kernel(x, ffn1_pre_g, ffn1_w_gu, ffn1_w_down, ffn1_post_g, mix_pre_g, w_in, shift_mu, decay_w0, decay_w2, iclr_a0, iclr_a2, gate_g2, k_k, k_a, r_k, lnx_w, lnx_b, vres_w1, vres_mu, vres_v0, vres_w2, sgu_ln_g, sgu_ln_b, sgu_w_s, sgu_b, w_b_rwkv, w_b_attn, w_b_sgu, w_out, mix_post_g, ffn2_pre_g, ffn2_w_gu, ffn2_w_down, ffn2_post_g):
    B, T, D = x.shape
    depth = w_in.shape[0]
    slopes = jnp.asarray(_alibi_slopes(ATTN_HEADS))
    row = lambda p: p.reshape(1, -1)
    ffn_jobs = lambda w_gu, w_down, l, fine=False: [_plain_job(w_gu, l, 16 if fine else 32),
                                                    _plain_job(w_down, l, 64 if fine else 128)]

    w_in_t = jnp.swapaxes(w_in, 1, 2)
    h = x.reshape(B * T, D)
    v_first = None
    wgu, wd = (_cast(job) for job in ffn_jobs(ffn1_w_gu, ffn1_w_down, 0))
    for l in range(depth):
        if l == 0:
            extra_t = jnp.zeros((LANES, D), F32)
            extra_mu = jnp.zeros((LANES,), F32)
        else:
            extra_t = _pad_rows(vres_w1[l - 1].T, 0, LANES)
            extra_mu = jnp.pad(vres_mu[l - 1], (0, LANES - VRES_LORA))
        jobs = [_w_in_job(w_in_t, extra_t, l)] + [_plain_job(w, l, 64)
                                                    for w in (w_b_rwkv, w_b_attn, w_b_sgu, w_out)]
        h, w_proj, wr, wa, ws, wo = _ffn(h, row(ffn1_pre_g[l]), wgu, wd, row(ffn1_post_g[l]), jobs)

        mu = jnp.concatenate([jnp.pad(shift_mu[l], (0, SEG_P - N_SHIFT)), extra_mu]).reshape(1, -1)
        proj, gates, wgu, wd = _norm_matmul(h, row(mix_pre_g[l]), w_proj,
                                            ffn_jobs(ffn2_w_gu, ffn2_w_down, l))

        vres = None
        if l > 0:
            vres = (v_first, row(vres_v0[l - 1]), _pad_rows(vres_w2[l - 1], 0, LANES).astype(BF))
        r, k, v, lw, kkr, a, g = _rwkv_prep(
            proj, mu, row(decay_w0[l]), _pad_rows(decay_w2[l], 0, LANES).astype(BF),
            row(iclr_a0[l]), _pad_rows(iclr_a2[l], DECAY_LORA, LANES).astype(BF),
            _pad_rows(gate_g2[l], 0, 2 * LANES).astype(BF), row(k_k[l]), row(k_a[l]), vres, seq=T)
        if l == 0:
            v_first = v
        y_rwkv = _wkv(r, k, v, lw, kkr, a, g, row(r_k[l]), row(lnx_w[l]), row(lnx_b[l]),
                      batch=B, seq=T)

        y_attn = _attention(proj, slopes, batch=B, seq=T)
        y_sgu = _sgu(proj, row(sgu_ln_g[l]), row(sgu_ln_b[l]), sgu_w_s[l], sgu_b[l].T)

        h = _merge(h, y_rwkv, y_attn, y_sgu, gates, wr, wa, ws, wo, row(mix_post_g[l]))

        jobs = ffn_jobs(ffn1_w_gu, ffn1_w_down, l + 1, fine=True) if l + 1 < depth else []
        h, *cast = _ffn(h, row(ffn2_pre_g[l]), wgu, wd, row(ffn2_post_g[l]), jobs)
        if cast:
            wgu, wd = cast
    return h.reshape(B, T, D)
```

```python
import functools
import math
from typing import Callable, NamedTuple, Optional

import numpy as np
import jax
import jax.numpy as jnp
from jax import lax
from jax.experimental import pallas as pl
from jax.experimental.pallas import tpu as pltpu

BF = jnp.bfloat16
F32 = jnp.float32

D_MODEL = 2048
RWKV_HEAD_DIM = 64
RWKV_WIDTH = D_MODEL // 2
DECAY_LORA = 64
ICLR_LORA = 64
VRES_LORA = 32
GATE_LORA = 160
LNX_EPS = 64e-5
ATTN_GROUPS = ((128, 1), (512, 4), (2048, 16))
ATTN_HEADS_PER_GROUP = 4
ATTN_HEAD_DIM = 128
ATTN_HEADS = len(ATTN_GROUPS) * ATTN_HEADS_PER_GROUP
ATTN_BLK = 128
ATTN_BLOCKS_PER_ITER = 8
SGU_CHUNK = 128
SGU_GROUPS = 4
SGU_WIDTH = D_MODEL // 4
RMS_EPS = 1e-6
LN_EPS = 1e-5
N_BRANCHES = 3

LANES = 128
SUBLANES = 8
WKV_CHUNK = 64
FFN_TF = 512
NEG_BIG = -1e30

N_SHIFT = 3 * RWKV_WIDTH + DECAY_LORA + ICLR_LORA + GATE_LORA
SEG_R, SEG_K, SEG_V = 0, RWKV_WIDTH, 2 * RWKV_WIDTH
SEG_WA = 3 * RWKV_WIDTH
SEG_G = SEG_WA + 128
SEG_P = SEG_G + 256
N_QKV = 3 * ATTN_HEADS * ATTN_HEAD_DIM
N_SGU = 2 * SGU_WIDTH
N_GATE = N_BRANCHES * D_MODEL
COL_SHIFT = 0
COL_QKV = SEG_P + LANES
COL_SGU = COL_QKV + N_QKV
COL_GATE = COL_SGU + N_SGU
N_PROJ = COL_GATE + N_GATE

VMEM_LIMIT = 60 * 1024 * 1024


def _cparams(*sem):
    return pltpu.CompilerParams(dimension_semantics=sem, vmem_limit_bytes=VMEM_LIMIT)


def _dot(a, b):
    return jnp.dot(a, b, preferred_element_type=F32)


def _dot_nt(a, b):
    return lax.dot_general(a, b, (((1,), (1,)), ((), ())), preferred_element_type=F32)


def _rms(x, g):
    return x * lax.rsqrt(jnp.mean(x * x, axis=-1, keepdims=True) + RMS_EPS) * g


class _CastJob(NamedTuple):
    src: jax.Array
    layer: int
    rows: int
    n_blocks: int
    offset: Callable
    patch: Optional[jax.Array] = None
    patch_block: int = 0
    patch_row: int = 0


class _CastMeta(NamedTuple):
    n_blocks: int
    has_patch: bool
    patch_block: int
    patch_row: int


def _plain_job(src, layer, rows):
    R = src.shape[1]
    assert R % rows == 0 and rows % (2 * SUBLANES) == 0
    return _CastJob(src, layer, rows, R // rows, lambda s: s * (rows // SUBLANES))


def _job_plumbing(jobs, step_of, n_steps):
    in_specs, args, out_specs, out_shapes, metas = [], [], [], [], []
    for job in jobs:
        assert job.n_blocks <= n_steps, "carrier kernel has too few grid steps for this cast job"
        C = job.src.shape[2]
        blk = lambda *g, job=job: jnp.minimum(step_of(*g), job.n_blocks - 1)
        in_specs.append(pl.BlockSpec(
            (None, pl.Element(job.rows), pl.Element(C)),
            lambda *g, job=job, blk=blk: (job.layer, job.offset(blk(*g)) * SUBLANES, 0)))
        args.append(job.src)
        if job.patch is not None:
            in_specs.append(pl.BlockSpec(job.patch.shape, lambda *g: (0, 0)))
            args.append(job.patch)
        out_specs.append(pl.BlockSpec((job.rows, C), lambda *g, blk=blk: (blk(*g), 0)))
        out_shapes.append(jax.ShapeDtypeStruct((job.n_blocks * job.rows, C), BF))
        metas.append(_CastMeta(job.n_blocks, job.patch is not None, job.patch_block, job.patch_row))
    return in_specs, args, out_specs, out_shapes, tuple(metas)


def _run_cast_jobs(step, metas, in_refs, out_refs):
    in_refs = iter(in_refs)
    for meta, dst in zip(metas, out_refs):
        src = next(in_refs)
        patch = next(in_refs) if meta.has_patch else None

        @pl.when(step < meta.n_blocks)
        def _(meta=meta, src=src, patch=patch, dst=dst):
            dst[...] = src[...].astype(BF)
            if patch is not None:
                @pl.when(step == meta.patch_block)
                def _():
                    dst[pl.ds(meta.patch_row, LANES), :] = patch[...].astype(BF)


def _n_job_inputs(metas):
    return sum(2 if m.has_patch else 1 for m in metas)


def _cast_kernel(*refs, metas):
    n_in = _n_job_inputs(metas)
    _run_cast_jobs(pl.program_id(0), metas, refs[:n_in], refs[n_in:])


def _cast(job):
    in_specs, args, out_specs, out_shapes, metas = _job_plumbing([job], lambda s: s, job.n_blocks)
    return pl.pallas_call(
        functools.partial(_cast_kernel, metas=metas),
        grid=(job.n_blocks,),
        in_specs=in_specs,
        out_specs=out_specs,
        out_shape=out_shapes,
        compiler_params=_cparams("arbitrary"),
        name="cast",
    )(*args)[0]


def _ffn_kernel(*refs, tail, metas):
    n_in = _n_job_inputs(metas)
    h_ref, pre_g_ref, wg_ref, wu_ref, wd_ref, post_g_ref = refs[:6]
    job_in = refs[6:6 + n_in]
    o_ref = refs[6 + n_in]
    job_out = refs[7 + n_in:7 + n_in + len(metas)]
    xn_scr, acc_scr = refs[7 + n_in + len(metas):]
    i, f = pl.program_id(0), pl.program_id(1)
    last = pl.num_programs(1) - 1
    tf = wd_ref.shape[0]

    @pl.when(f == 0)
    def _():
        xn_scr[...] = _rms(h_ref[...], pre_g_ref[...]).astype(BF)
        acc_scr[...] = jnp.zeros_like(acc_scr)

    def accumulate(lo):
        xn = xn_scr[...]
        g = _dot(xn, wg_ref[:, lo:])
        u = _dot(xn, wu_ref[:, lo:])
        a = (g * jax.nn.sigmoid(g) * u).astype(BF)
        acc_scr[...] += _dot(a, wd_ref[lo:, :])

    @pl.when(f < last)
    def _():
        accumulate(0)

    @pl.when(f == last)
    def _():
        accumulate(tf - tail)
        o_ref[...] = h_ref[...] + 0.5 * _rms(acc_scr[...], post_g_ref[...])

    _run_cast_jobs(i * pl.num_programs(1) + f, metas, job_in, job_out)


def _ffn(h, pre_g, wgu, wd, post_g, jobs=(), *, tm=512, tf=FFN_TF):
    M, D = h.shape
    F = wd.shape[0]
    nf = -(-F // tf)
    tail = F - (nf - 1) * tf
    assert wgu.shape == (D, 2 * F) and F % LANES == 0 and tf % LANES == 0 and tail % LANES == 0
    start = lambda f, base=0: (base // LANES + jnp.minimum(f * (tf // LANES), (F - tf) // LANES)) * LANES
    j_in, j_args, j_out, j_shapes, metas = _job_plumbing(jobs, lambda i, f: i * nf + f, M // tm * nf)
    return pl.pallas_call(
        functools.partial(_ffn_kernel, tail=tail, metas=metas),
        grid=(M // tm, nf),
        in_specs=[
            pl.BlockSpec((tm, D), lambda i, f: (i, 0)),
            pl.BlockSpec((1, D), lambda i, f: (0, 0)),
            pl.BlockSpec((pl.Element(D), pl.Element(tf)), lambda i, f: (0, start(f))),
            pl.BlockSpec((pl.Element(D), pl.Element(tf)), lambda i, f: (0, start(f, F))),
            pl.BlockSpec((pl.Element(tf), pl.Element(D)), lambda i, f: (start(f), 0)),
            pl.BlockSpec((1, D), lambda i, f: (0, 0)),
        ] + j_in,
        out_specs=[pl.BlockSpec((tm, D), lambda i, f: (i, 0))] + j_out,
        out_shape=[jax.ShapeDtypeStruct((M, D), F32)] + j_shapes,
        scratch_shapes=[pltpu.VMEM((tm, D), BF), pltpu.VMEM((tm, D), F32)],
        compiler_params=_cparams("arbitrary", "arbitrary"),
        name="ffn",
    )(h, pre_g, wgu, wgu, wd, post_g, *j_args)


def _gelu_tanh(x):
    c = math.sqrt(2.0 / math.pi)
    return x * (0.5 * (1.0 + jnp.tanh(c * (x + 0.044715 * (x * x * x)))))


def _nmm_kernel(*refs, n_plain, metas):
    n_in = _n_job_inputs(metas)
    h_ref, g_ref, w_ref = refs[:3]
    job_in = refs[3:3 + n_in]
    o_ref, gate_ref = refs[3 + n_in:5 + n_in]
    job_out = refs[5 + n_in:5 + n_in + len(metas)]
    xn_scr = refs[-1]
    i, j = pl.program_id(0), pl.program_id(1)

    @pl.when(j == 0)
    def _():
        xn_scr[...] = _rms(h_ref[...], g_ref[...]).astype(BF)

    @pl.when(j < n_plain)
    def _():
        o_ref[...] = _dot_nt(xn_scr[...], w_ref[...])

    @pl.when(j >= n_plain)
    def _():
        gate_ref[...] = jax.nn.sigmoid(_dot_nt(xn_scr[...], w_ref[...])).astype(gate_ref.dtype)

    _run_cast_jobs(i * pl.num_programs(1) + j, metas, job_in, job_out)


def _norm_matmul(h, g, w_t, jobs=(), *, tm=1024, tn=1024):
    M, D = h.shape
    N = w_t.shape[0]
    nj = N // tn
    n_plain = COL_GATE // tn
    assert N == N_PROJ and nj * tn == N and n_plain * tn == COL_GATE
    j_in, j_args, j_out, j_shapes, metas = _job_plumbing(jobs, lambda i, j: i * nj + j, M // tm * nj)
    return pl.pallas_call(
        functools.partial(_nmm_kernel, n_plain=n_plain, metas=metas),
        grid=(M // tm, nj),
        in_specs=[
            pl.BlockSpec((tm, D), lambda i, j: (i, 0)),
            pl.BlockSpec((1, D), lambda i, j: (0, 0)),
            pl.BlockSpec((tn, D), lambda i, j: (j, 0)),
        ] + j_in,
        out_specs=[pl.BlockSpec((tm, tn), lambda i, j: (i, jnp.minimum(j, n_plain - 1))),
                   pl.BlockSpec((tm, tn), lambda i, j: (i, jnp.maximum(j - n_plain, 0)))] + j_out,
        out_shape=[jax.ShapeDtypeStruct((M, COL_GATE), F32),
                   jax.ShapeDtypeStruct((M, N - COL_GATE), BF)] + j_shapes,
        scratch_shapes=[pltpu.VMEM((tm, D), BF)],
        compiler_params=_cparams("arbitrary", "arbitrary"),
        name="in_proj",
    )(h, g, w_t, *j_args)


def _prep_kernel(*refs, seq, has_vres):
    if has_vres:
        (ps_ref, prev_ref, mu_ref, w0_ref, w2_ref, a0_ref, a2_ref, g2_ref, kk_ref, ka_ref,
         vf_ref, v0_ref, vw2_ref,
         r_o, k_o, v_o, lw_o, kk_o, a_o, g_o) = refs
    else:
        (ps_ref, prev_ref, mu_ref, w0_ref, w2_ref, a0_ref, a2_ref, g2_ref, kk_ref, ka_ref,
         r_o, k_o, v_o, lw_o, kk_o, a_o, g_o) = refs
    tm = ps_ref.shape[0]
    W = RWKV_WIDTH
    x = ps_ref[...]
    at_seq_start = (pl.program_id(0) * tm) % seq == 0
    last_prev = jnp.where(at_seq_start, 0.0, prev_ref[7:8, :])
    row = lax.broadcasted_iota(jnp.int32, x.shape, 0)
    prev = jnp.where(row == 0, last_prev, pltpu.roll(x, 1, axis=0))
    xs = x + (prev - x) * mu_ref[...]

    xr = xs[:, SEG_R:SEG_R + W]
    xk = xs[:, SEG_K:SEG_K + W]
    xv = xs[:, SEG_V:SEG_V + W]
    xwa = xs[:, SEG_WA:SEG_WA + 128]
    xg = xs[:, SEG_G:SEG_G + 256]

    z = w0_ref[...] + _dot(jnp.tanh(xwa).astype(BF), w2_ref[...])
    w_log = -jax.nn.softplus(-z) - 0.5
    lw_o[...] = -jnp.exp(w_log)
    a = jax.nn.sigmoid(a0_ref[...] + _dot(xwa.astype(BF), a2_ref[...]))
    a_o[...] = a
    g_o[...] = _dot(jax.nn.sigmoid(xg).astype(BF), g2_ref[...]).astype(g_o.dtype)
    kk_o[...] = xk * kk_ref[...]
    k_o[...] = xk * (1.0 + (a - 1.0) * ka_ref[...])
    r_o[...] = xr
    if has_vres:
        pv = xs[:, SEG_P:SEG_P + 128]
        mix = jax.nn.sigmoid(v0_ref[...] + _dot(pv.astype(BF), vw2_ref[...]))
        v_o[...] = xv + (vf_ref[...] - xv) * mix
    else:
        v_o[...] = xv


def _rwkv_prep(ps, mu, w0, w2, a0, a2, g2, k_k, k_a, vres, *, seq, tm=256):
    M = ps.shape[0]
    NA = COL_QKV
    W = RWKV_WIDTH
    row = lambda n: pl.BlockSpec((1, n), lambda i: (0, 0))
    full = lambda a: pl.BlockSpec(a.shape, lambda i: (0, 0))
    tok = pl.BlockSpec((tm, W), lambda i: (i, 0))
    in_specs = [
        pl.BlockSpec((tm, NA), lambda i: (i, 0)),
        pl.BlockSpec((8, NA), lambda i: (jnp.maximum(i * (tm // 8) - 1, 0), 0)),
        row(NA), row(W), full(w2), row(W), full(a2), full(g2), row(W), row(W),
    ]
    args = [ps, ps, mu, w0, w2, a0, a2, g2, k_k, k_a]
    if vres is not None:
        v_first, v0, vw2 = vres
        in_specs += [tok, row(W), full(vw2)]
        args += [v_first, v0, vw2]
    return pl.pallas_call(
        functools.partial(_prep_kernel, seq=seq, has_vres=vres is not None),
        grid=(M // tm,),
        in_specs=in_specs,
        out_specs=[tok] * 7,
        out_shape=[jax.ShapeDtypeStruct((M, W), F32)] * 6 + [jax.ShapeDtypeStruct((M, W), BF)],
        compiler_params=_cparams("parallel"),
        name="rwkv_prep",
    )(*args)


def _wkv_kernel(r_ref, k_ref, v_ref, lw_ref, kk_ref, a_ref, g_ref, rk_ref, lnw_ref, lnb_ref,
                y_ref, p_scr):
    C = WKV_CHUNK
    C2 = 2 * C

    @pl.when(pl.program_id(1) == 0)
    def _():
        p_scr[...] = jnp.zeros_like(p_scr)

    lane = lax.broadcasted_iota(jnp.int32, (C, LANES), 1)
    lo = lane < RWKV_HEAD_DIM
    ti = lax.broadcasted_iota(jnp.int32, (C, C), 0)
    tj = lax.broadcasted_iota(jnp.int32, (C, C), 1)
    tri = jnp.where(ti >= tj, 1.0, 0.0).astype(BF)
    tq = lax.broadcasted_iota(jnp.int32, (C, LANES), 0)
    sq = lane % C
    strict = tq > sq
    incl = tq >= sq
    eye_c = jnp.where(tq == sq, 1.0, 0.0)
    si = lax.broadcasted_iota(jnp.int32, (C2, LANES), 0)
    sj = lax.broadcasted_iota(jnp.int32, (C2, LANES), 1)
    eye = si == sj

    def seg_sum(x):
        s_lo = jnp.sum(jnp.where(lo, x, 0.0), axis=1, keepdims=True)
        s_hi = jnp.sum(jnp.where(lo, 0.0, x), axis=1, keepdims=True)
        return jnp.where(lo, s_lo, s_hi)

    def stack(x):
        return jnp.concatenate([jnp.where(lo, x, 0.0), jnp.where(lo, 0.0, x)], axis=0)

    def bstack(x):
        return stack(x).astype(BF)

    n_pairs = r_ref.shape[1] // LANES
    n_chunks = r_ref.shape[0] // C
    items = [(c, p) for c in range(n_chunks) for p in range(n_pairs)]
    rows = [slice(c * C, (c + 1) * C) for c, _ in items]
    cols = [slice(p * LANES, (p + 1) * LANES) for _, p in items]
    each = lambda f, *xs: [f(*t) for t in zip(*xs)]
    bf = lambda xs: [x.astype(BF) for x in xs]
    load = lambda ref: [ref[rs, cs] for rs, cs in zip(rows, cols)]

    r, k, v, lw, kkr, a = (load(ref) for ref in (r_ref, k_ref, v_ref, lw_ref, kk_ref, a_ref))

    def cumsum(x):
        hi = x.astype(BF)
        both = _dot(tri, jnp.concatenate([hi, (x - hi.astype(F32)).astype(BF)], axis=1))
        return both[:, :LANES] + both[:, LANES:]

    cum = each(cumsum, lw)
    tot = [c[C - 1:C, :] for c in cum]
    kk = each(lambda x: x / jnp.maximum(jnp.sqrt(seg_sum(x * x)), 1e-12), kkr)
    kka = each(lambda x, y: x * y, kk, a)
    A_c = each(lambda x, c, w: -x * jnp.exp(c - w), kk, cum, lw)
    R_c = bf(each(lambda x, c: x * jnp.exp(c), r, cum))
    AR = each(lambda a_, r_: jnp.concatenate([a_.astype(BF), r_], axis=0), A_c, R_c)
    BK_st = each(lambda x, y, c: jnp.concatenate([bstack(x * jnp.exp(-c)), bstack(y * jnp.exp(-c))], axis=0),
                 kka, k, cum)
    V_st = each(bstack, v)
    BKpT = bf(each(lambda x, y, c, t: jnp.concatenate([stack(x * jnp.exp(t - c)).T,
                                                       stack(y * jnp.exp(t - c)).T], axis=1),
                   kka, k, cum, tot))
    gcol = [jnp.exp(jnp.sum(jnp.where(eye, jnp.broadcast_to(t, (C2, LANES)), 0.0), axis=1, keepdims=True))
            for t in tot]

    G = each(_dot_nt, AR, BK_st)
    Aab = [jnp.where(strict, x[:C, :LANES], 0.0) for x in G]
    Aak = bf([jnp.where(strict, x[:C, LANES:], 0.0) for x in G])
    Mrb = bf([jnp.where(incl, x[C:, :LANES], 0.0) for x in G])
    Mrk = bf([jnp.where(incl, x[C:, LANES:], 0.0) for x in G])
    AkV = each(_dot, Aak, V_st)

    S = [eye_c + x for x in Aab]
    Pw = each(lambda x: _dot(x.astype(BF), bstack(x)), Aab)
    n = 2
    while 2 * n < C:
        prod = each(lambda p_, s_: _dot(p_.astype(BF), jnp.concatenate([bstack(p_), bstack(s_)], axis=1)),
                    Pw, S)
        S = each(lambda s_, pr: s_ + pr[:, LANES:], S, prod)
        Pw = [pr[:, :LANES] for pr in prod]
        n *= 2
    S = each(lambda s_, p_: s_ + _dot(p_.astype(BF), bstack(s_)), S, Pw)
    WU = each(lambda s_, a_, u0: _dot(s_.astype(BF), jnp.concatenate([bstack(a_), bstack(u0)], axis=1)),
              S, A_c, AkV)

    def finish(y_, r_, k_, v_, rs, cs):
        mu = seg_sum(y_) * (1.0 / RWKV_HEAD_DIM)
        yc = y_ - mu
        var = seg_sum(yc * yc) * (1.0 / RWKV_HEAD_DIM)
        yn = yc * lax.rsqrt(var + LNX_EPS) * lnw_ref[:, cs] + lnb_ref[:, cs]
        bonus = seg_sum(r_ * k_ * rk_ref[:, cs]) * v_
        return ((yn + bonus) * g_ref[rs, cs]).astype(y_ref.dtype)

    P = [p_scr[p] for p in range(n_pairs)]
    for c in range(n_chunks):
        sel = lambda xs: xs[c * n_pairs:(c + 1) * n_pairs]
        Pb = bf(P)
        U = each(lambda wu, pb: _dot(wu[:, :LANES].astype(BF), pb) + wu[:, LANES:], sel(WU), Pb)
        U_st = each(bstack, U)
        y = each(lambda r_, mb, mk, pb, u, vs: _dot(jnp.concatenate([r_, mb, mk], axis=1),
                                                    jnp.concatenate([pb, u, vs], axis=0)),
                 sel(R_c), sel(Mrb), sel(Mrk), Pb, U_st, sel(V_st))
        P = each(lambda gc, p_, m, u, vs: gc * p_ + _dot(m, jnp.concatenate([u, vs], axis=0)),
                 sel(gcol), P, sel(BKpT), U_st, sel(V_st))
        out = each(finish, y, sel(r), sel(k), sel(v), sel(rows), sel(cols))
        for o, rs, cs in zip(out, sel(rows), sel(cols)):
            y_ref[rs, cs] = o
    for p in range(n_pairs):
        p_scr[p] = P[p]


def _wkv(r, k, v, lw, kkr, a, g, r_k, lnx_w, lnx_b, *, batch, seq, chunks_per_step=2):
    M, W = r.shape
    rows = chunks_per_step * WKV_CHUNK
    steps = seq // rows
    tok = pl.BlockSpec((rows, W), lambda b, c: (b * steps + c, 0))
    par = pl.BlockSpec((1, W), lambda b, c: (0, 0))
    return pl.pallas_call(
        _wkv_kernel,
        grid=(batch, steps),
        in_specs=[tok] * 7 + [par] * 3,
        out_specs=tok,
        out_shape=jax.ShapeDtypeStruct((M, W), BF),
        scratch_shapes=[pltpu.VMEM((W // LANES, LANES, LANES), F32)],
        compiler_params=_cparams("parallel", "arbitrary"),
        name="wkv",
    )(r, k, v, lw, kkr, a, g, r_k, lnx_w, lnx_b)


def _attn_kernel(slopes_ref, q_ref, k_ref, v_ref, o_ref, og_scr, lse_scr, *, seq):
    slot = pl.program_id(1)
    grp = pl.program_id(2)
    blk = ATTN_BLK
    scale = ATTN_HEAD_DIM ** -0.5
    qi = lax.broadcasted_iota(jnp.int32, (blk, 2 * blk), 0)
    kj = lax.broadcasted_iota(jnp.int32, (blk, 2 * blk), 1)
    rel = (qi + blk - kj).astype(F32)
    in_band = (kj >= qi) & (kj <= qi + blk)
    is_prev = kj < blk
    nb = ATTN_BLOCKS_PER_ITER
    each = lambda f, *xs: [f(*t) for t in zip(*xs)]

    def rows(start, d):
        if d == 1:
            return pl.ds(start, blk)
        return pl.ds(start, blk, stride=d)

    for gi, (window, d) in enumerate(ATTN_GROUPS):
        assert window // d == blk and seq % window == 0 and (seq // blk) % nb == 0

        @pl.when(grp == gi)
        def _(gi=gi, d=d):
            slope = slopes_ref[gi * ATTN_HEADS_PER_GROUP + slot]
            bias = jnp.where(in_band, -slope * float(d) * rel, NEG_BIG)
            span = blk * d

            def body(it, carry):
                ids = [it * nb + j for j in range(nb)]
                ns = [i // d for i in ids]
                cur = [rows(n * span + i % d, d) for n, i in zip(ns, ids)]
                prv = [rows(jnp.maximum(n - 1, 0) * span + i % d, d) for n, i in zip(ns, ids)]
                q = [q_ref[c, :].astype(BF) for c in cur]
                kcat = [jnp.concatenate([k_ref[p_, :], k_ref[c, :]], axis=0).astype(BF) for p_, c in zip(prv, cur)]
                vcat = [jnp.concatenate([v_ref[p_, :], v_ref[c, :]], axis=0).astype(BF) for p_, c in zip(prv, cur)]
                s = each(lambda q_, k_: _dot_nt(q_, k_) * scale + bias, q, kcat)
                s = each(lambda s_, n: jnp.where(is_prev & (n == 0), NEG_BIG, s_), s, ns)
                m = [jnp.max(s_, axis=1, keepdims=True) for s_ in s]
                p = each(lambda s_, m_: jnp.exp(s_ - m_), s, m)
                l = [jnp.sum(p_, axis=1, keepdims=True) for p_ in p]
                o = each(lambda p_, v_, l_: _dot(p_.astype(BF), v_) / l_, p, vcat, l)
                lse = each(lambda m_, l_: m_ + jnp.log(l_), m, l)
                for c, o_, e_ in zip(cur, o, lse):
                    og_scr[gi, c, :] = o_
                    lse_scr[gi, c, :] = jnp.broadcast_to(e_, (blk, LANES))
                return carry

            lax.fori_loop(0, seq // blk // nb, body, 0)

    @pl.when(grp == len(ATTN_GROUPS) - 1)
    def _():
        tile = 8 * blk

        def merge(t, carry):
            rs = pl.ds(pl.multiple_of(t * tile, tile), tile)
            lses = [lse_scr[g, rs, :] for g in range(len(ATTN_GROUPS))]
            top = functools.reduce(jnp.maximum, lses)
            w = [jnp.exp(e - top) for e in lses]
            num = sum(w_ * og_scr[g, rs, :] for g, w_ in enumerate(w))
            o_ref[rs, :] = (num / sum(w)).astype(o_ref.dtype)
            return carry

        lax.fori_loop(0, seq // tile, merge, 0)


def _attention(qkv, slopes, *, batch, seq):
    M = qkv.shape[0]
    E = ATTN_HEAD_DIM
    hpg = ATTN_HEADS_PER_GROUP

    def spec(which):
        first = COL_QKV // E + which * ATTN_HEADS
        return pl.BlockSpec((seq, E), lambda b, s, g: (b, first + g * hpg + s))

    return pl.pallas_call(
        functools.partial(_attn_kernel, seq=seq),
        grid=(batch, hpg, len(ATTN_GROUPS)),
        in_specs=[pl.BlockSpec(memory_space=pltpu.SMEM), spec(0), spec(1), spec(2)],
        out_specs=pl.BlockSpec((seq, E), lambda b, s, g: (b, s)),
        out_shape=jax.ShapeDtypeStruct((M, hpg * E), BF),
        scratch_shapes=[pltpu.VMEM((len(ATTN_GROUPS), seq, LANES), F32)] * 2,
        compiler_params=_cparams("parallel", "parallel", "arbitrary"),
        name="dilated_attn",
    )(slopes, qkv, qkv, qkv)


def _sgu_kernel(x_ref, lg_ref, lb_ref, ws_ref, bt_ref, o_ref):
    tm = x_ref.shape[0]
    CH = SGU_CHUNK
    GD = SGU_WIDTH // SGU_GROUPS
    u = _gelu_tanh(x_ref[:, :SGU_WIDTH])
    v = _gelu_tanh(x_ref[:, SGU_WIDTH:])
    mu = jnp.mean(v, axis=-1, keepdims=True)
    vc = v - mu
    var = jnp.mean(vc * vc, axis=-1, keepdims=True)
    vn = (vc * lax.rsqrt(var + LN_EPS) * lg_ref[...] + lb_ref[...]).astype(BF)
    ti = lax.broadcasted_iota(jnp.int32, (CH, CH), 0)
    si = lax.broadcasted_iota(jnp.int32, (CH, CH), 1)
    causal = ti >= si
    for gidx in range(SGU_GROUPS):
        w = jnp.where(causal, ws_ref[gidx], 0.0).astype(BF)
        bias = bt_ref[:, gidx:gidx + 1]
        cs = slice(gidx * GD, (gidx + 1) * GD)
        for c in range(tm // CH):
            rs = slice(c * CH, (c + 1) * CH)
            mixed = _dot(w, vn[rs, cs]) + bias
            o_ref[rs, cs] = (u[rs, cs] * mixed).astype(o_ref.dtype)


def _sgu(proj, ln_g, ln_b, w_s, b_t, *, tm=512):
    M = proj.shape[0]
    return pl.pallas_call(
        _sgu_kernel,
        grid=(M // tm,),
        in_specs=[
            pl.BlockSpec((tm, N_SGU), lambda i: (i, COL_SGU // N_SGU)),
            pl.BlockSpec((1, SGU_WIDTH), lambda i: (0, 0)),
            pl.BlockSpec((1, SGU_WIDTH), lambda i: (0, 0)),
            pl.BlockSpec(w_s.shape, lambda i: (0, 0, 0)),
            pl.BlockSpec(b_t.shape, lambda i: (0, 0)),
        ],
        out_specs=pl.BlockSpec((tm, SGU_WIDTH), lambda i: (i, 0)),
        out_shape=jax.ShapeDtypeStruct((M, SGU_WIDTH), BF),
        compiler_params=_cparams("parallel"),
        name="sgu",
    )(proj, ln_g, ln_b, w_s, b_t)


def _merge_kernel(h_ref, yr_ref, ya_ref, ys_ref, g0_ref, g1_ref, g2_ref, wr_ref, wa_ref, ws_ref,
                  wo_ref, pg_ref, o_ref, acc_scr):
    j = pl.program_id(1)

    @pl.when(j == 0)
    def _():
        acc_scr[...] = jnp.zeros_like(acc_scr)

    merged = (g0_ref[...] * _dot(yr_ref[...], wr_ref[...])
              + g1_ref[...] * _dot(ya_ref[...], wa_ref[...])
              + g2_ref[...] * _dot(ys_ref[...], ws_ref[...]))
    acc_scr[...] += _dot(merged.astype(BF), wo_ref[...])

    @pl.when(j == pl.num_programs(1) - 1)
    def _():
        o_ref[...] = h_ref[...] + _rms(acc_scr[...], pg_ref[...])


def _merge(h, yr, ya, ys, gates, wr, wa, ws, wo, post_g, *, tm=512, tn=1024):
    M, D = h.shape
    nj = D // tn
    act = lambda a: pl.BlockSpec((tm, a.shape[1]), lambda i, j: (i, 0))
    gate = lambda br: pl.BlockSpec((tm, tn), lambda i, j: (i, br * nj + j))
    wcol = lambda a: pl.BlockSpec((a.shape[0], tn), lambda i, j: (0, j))
    return pl.pallas_call(
        _merge_kernel,
        grid=(M // tm, nj),
        in_specs=[
            pl.BlockSpec((tm, D), lambda i, j: (i, 0)),
            act(yr), act(ya), act(ys), gate(0), gate(1), gate(2),
            wcol(wr), wcol(wa), wcol(ws),
            pl.BlockSpec((tn, D), lambda i, j: (j, 0)),
            pl.BlockSpec((1, D), lambda i, j: (0, 0)),
        ],
        out_specs=pl.BlockSpec((tm, D), lambda i, j: (i, 0)),
        out_shape=jax.ShapeDtypeStruct((M, D), F32),
        scratch_shapes=[pltpu.VMEM((tm, D), F32)],
        compiler_params=_cparams("parallel", "arbitrary"),
        name="merge_out",
    )(h, yr, ya, ys, gates, gates, gates, wr, wa, ws, wo, post_g)


def _alibi_slopes(n_heads):
    def geometric(n):
        start = 2.0 ** (-8.0 / n)
        return [start ** (i + 1) for i in range(n)]
    closest = 2 ** int(math.floor(math.log2(n_heads)))
    slopes = geometric(closest)
    if closest < n_heads:
        slopes += geometric(2 * closest)[0::2][: n_heads - closest]
    return np.array(sorted(slopes, reverse=True), dtype=np.float32)


def _pad_rows(x, before, total):
    return jnp.pad(x, ((before, total - before - x.shape[0]), (0, 0)))


def _w_in_job(w_in_t, extra_t, l, *, rows=128):
    n_in = w_in_t.shape[1]
    assert n_in - N_SHIFT == N_PROJ - COL_QKV and (COL_QKV - N_SHIFT) % SUBLANES == 0
    assert COL_QKV % rows == 0 and N_PROJ % rows == 0 and SEG_P % rows + LANES <= rows
    n_head = COL_QKV // rows

    def offset(s):
        return s * (rows // SUBLANES) - jnp.where(s < n_head, 0, (COL_QKV - N_SHIFT) // SUBLANES)

    return _CastJob(w_in_t, l, rows, N_PROJ // rows, offset, extra_t, SEG_P // rows, SEG_P % rows)


def kernel(x, ffn1_pre_g, ffn1_w_gu, ffn1_w_down, ffn1_post_g, mix_pre_g, w_in, shift_mu, decay_w0, decay_w2, iclr_a0, iclr_a2, gate_g2, k_k, k_a, r_k, lnx_w, lnx_b, vres_w1, vres_mu, vres_v0, vres_w2, sgu_ln_g, sgu_ln_b, sgu_w_s, sgu_b, w_b_rwkv, w_b_attn, w_b_sgu, w_out, mix_post_g, ffn2_pre_g, ffn2_w_gu, ffn2_w_down, ffn2_post_g):
    B, T, D = x.shape
    depth = w_in.shape[0]
    slopes = jnp.asarray(_alibi_slopes(ATTN_HEADS))
    row = lambda p: p.reshape(1, -1)
    ffn_jobs = lambda w_gu, w_down, l, fine=False: [_plain_job(w_gu, l, 16 if fine else 32),
                                                    _plain_job(w_down, l, 64 if fine else 128)]

    w_in_t = jnp.swapaxes(w_in, 1, 2)
    h = x.reshape(B * T, D)
    v_first = None
    wgu = _cast(_plain_job(ffn1_w_gu, 0, 128))
    wd = _cast(_plain_job(ffn1_w_down, 0, ffn1_w_down.shape[1] // 8))
    for l in range(depth):
        if l == 0:
            extra_t = jnp.zeros((LANES, D), F32)
            extra_mu = jnp.zeros((LANES,), F32)
        else:
            extra_t = _pad_rows(vres_w1[l - 1].T, 0, LANES)
            extra_mu = jnp.pad(vres_mu[l - 1], (0, LANES - VRES_LORA))
        jobs = [_w_in_job(w_in_t, extra_t, l)] + [_plain_job(w, l, 64)
                                                    for w in (w_b_rwkv, w_b_attn, w_b_sgu, w_out)]
        h, w_proj, wr, wa, ws, wo = _ffn(h, row(ffn1_pre_g[l]), wgu, wd, row(ffn1_post_g[l]), jobs)

        mu = jnp.concatenate([jnp.pad(shift_mu[l], (0, SEG_P - N_SHIFT)), extra_mu]).reshape(1, -1)
        proj, gates, wgu, wd = _norm_matmul(h, row(mix_pre_g[l]), w_proj,
                                            ffn_jobs(ffn2_w_gu, ffn2_w_down, l))

        vres = None
        if l > 0:
            vres = (v_first, row(vres_v0[l - 1]), _pad_rows(vres_w2[l - 1], 0, LANES).astype(BF))
        r, k, v, lw, kkr, a, g = _rwkv_prep(
            proj, mu, row(decay_w0[l]), _pad_rows(decay_w2[l], 0, LANES).astype(BF),
            row(iclr_a0[l]), _pad_rows(iclr_a2[l], DECAY_LORA, LANES).astype(BF),
            _pad_rows(gate_g2[l], 0, 2 * LANES).astype(BF), row(k_k[l]), row(k_a[l]), vres, seq=T)
        if l == 0:
            v_first = v
        y_rwkv = _wkv(r, k, v, lw, kkr, a, g, row(r_k[l]), row(lnx_w[l]), row(lnx_b[l]),
                      batch=B, seq=T)

        y_attn = _attention(proj, slopes, batch=B, seq=T)
        y_sgu = _sgu(proj, row(sgu_ln_g[l]), row(sgu_ln_b[l]), sgu_w_s[l], sgu_b[l].T)

        h = _merge(h, y_rwkv, y_attn, y_sgu, gates, wr, wa, ws, wo, row(mix_post_g[l]))

        jobs = ffn_jobs(ffn1_w_gu, ffn1_w_down, l + 1, fine=True) if l + 1 < depth else []
        h, *cast = _ffn(h, row(ffn2_pre_g[l]), wgu, wd, row(ffn2_post_g[l]), jobs)
        if cast:
            wgu, wd = cast
    return h.reshape(B, T, D)
```

```python
import functools
import math
from typing import Callable, NamedTuple, Optional

import numpy as np
import jax
import jax.numpy as jnp
from jax import lax
from jax.experimental import pallas as pl
from jax.experimental.pallas import tpu as pltpu

BF = jnp.bfloat16
F32 = jnp.float32

D_MODEL = 2048
RWKV_HEAD_DIM = 64
RWKV_WIDTH = D_MODEL // 2
DECAY_LORA = 64
ICLR_LORA = 64
VRES_LORA = 32
GATE_LORA = 160
LNX_EPS = 64e-5
ATTN_GROUPS = ((128, 1), (512, 4), (2048, 16))
ATTN_HEADS_PER_GROUP = 4
ATTN_HEAD_DIM = 128
ATTN_HEADS = len(ATTN_GROUPS) * ATTN_HEADS_PER_GROUP
ATTN_BLK = 128
ATTN_BLOCKS_PER_ITER = 8
SGU_CHUNK = 128
SGU_GROUPS = 4
SGU_WIDTH = D_MODEL // 4
RMS_EPS = 1e-6
LN_EPS = 1e-5
N_BRANCHES = 3

LANES = 128
SUBLANES = 8
WKV_CHUNK = 64
FFN_TF = 512
NEG_BIG = -1e30

N_SHIFT = 3 * RWKV_WIDTH + DECAY_LORA + ICLR_LORA + GATE_LORA
SEG_R, SEG_K, SEG_V = 0, RWKV_WIDTH, 2 * RWKV_WIDTH
SEG_WA = 3 * RWKV_WIDTH
SEG_G = SEG_WA + 128
SEG_P = SEG_G + 256
N_QKV = 3 * ATTN_HEADS * ATTN_HEAD_DIM
N_SGU = 2 * SGU_WIDTH
N_GATE = N_BRANCHES * D_MODEL
COL_SHIFT = 0
COL_QKV = SEG_P + LANES
COL_SGU = COL_QKV + N_QKV
COL_GATE = COL_SGU + N_SGU
N_PROJ = COL_GATE + N_GATE

VMEM_LIMIT = 60 * 1024 * 1024


def _cparams(*sem):
    return pltpu.CompilerParams(dimension_semantics=sem, vmem_limit_bytes=VMEM_LIMIT)


def _dot(a, b):
    return jnp.dot(a, b, preferred_element_type=F32)


def _dot_nt(a, b):
    return lax.dot_general(a, b, (((1,), (1,)), ((), ())), preferred_element_type=F32)


def _rms(x, g):
    return x * lax.rsqrt(jnp.mean(x * x, axis=-1, keepdims=True) + RMS_EPS) * g


class _CastJob(NamedTuple):
    src: jax.Array
    layer: int
    rows: int
    n_blocks: int
    offset: Callable
    patch: Optional[jax.Array] = None
    patch_block: int = 0
    patch_row: int = 0


class _CastMeta(NamedTuple):
    n_blocks: int
    has_patch: bool
    patch_block: int
    patch_row: int


def _plain_job(src, layer, rows):
    R = src.shape[1]
    assert R % rows == 0 and rows % (2 * SUBLANES) == 0
    return _CastJob(src, layer, rows, R // rows, lambda s: s * (rows // SUBLANES))


def _job_plumbing(jobs, step_of, n_steps):
    in_specs, args, out_specs, out_shapes, metas = [], [], [], [], []
    for job in jobs:
        assert job.n_blocks <= n_steps, "carrier kernel has too few grid steps for this cast job"
        C = job.src.shape[2]
        blk = lambda *g, job=job: jnp.minimum(step_of(*g), job.n_blocks - 1)
        in_specs.append(pl.BlockSpec(
            (None, pl.Element(job.rows), pl.Element(C)),
            lambda *g, job=job, blk=blk: (job.layer, job.offset(blk(*g)) * SUBLANES, 0)))
        args.append(job.src)
        if job.patch is not None:
            in_specs.append(pl.BlockSpec(job.patch.shape, lambda *g: (0, 0)))
            args.append(job.patch)
        out_specs.append(pl.BlockSpec((job.rows, C), lambda *g, blk=blk: (blk(*g), 0)))
        out_shapes.append(jax.ShapeDtypeStruct((job.n_blocks * job.rows, C), BF))
        metas.append(_CastMeta(job.n_blocks, job.patch is not None, job.patch_block, job.patch_row))
    return in_specs, args, out_specs, out_shapes, tuple(metas)


def _run_cast_jobs(step, metas, in_refs, out_refs):
    in_refs = iter(in_refs)
    for meta, dst in zip(metas, out_refs):
        src = next(in_refs)
        patch = next(in_refs) if meta.has_patch else None

        @pl.when(step < meta.n_blocks)
        def _(meta=meta, src=src, patch=patch, dst=dst):
            dst[...] = src[...].astype(BF)
            if patch is not None:
                @pl.when(step == meta.patch_block)
                def _():
                    dst[pl.ds(meta.patch_row, LANES), :] = patch[...].astype(BF)


def _n_job_inputs(metas):
    return sum(2 if m.has_patch else 1 for m in metas)


def _cast_kernel(*refs, metas):
    n_in = _n_job_inputs(metas)
    _run_cast_jobs(pl.program_id(0), metas, refs[:n_in], refs[n_in:])


def _cast(job):
    in_specs, args, out_specs, out_shapes, metas = _job_plumbing([job], lambda s: s, job.n_blocks)
    return pl.pallas_call(
        functools.partial(_cast_kernel, metas=metas),
        grid=(job.n_blocks,),
        in_specs=in_specs,
        out_specs=out_specs,
        out_shape=out_shapes,
        compiler_params=_cparams("arbitrary"),
        name="cast",
    )(*args)[0]


def _ffn_kernel(*refs, tail, metas):
    n_in = _n_job_inputs(metas)
    h_ref, pre_g_ref, wg_ref, wu_ref, wd_ref, post_g_ref = refs[:6]
    job_in = refs[6:6 + n_in]
    o_ref = refs[6 + n_in]
    job_out = refs[7 + n_in:7 + n_in + len(metas)]
    xn_scr, acc_scr = refs[7 + n_in + len(metas):]
    i, f = pl.program_id(0), pl.program_id(1)
    last = pl.num_programs(1) - 1
    tf = wd_ref.shape[0]

    @pl.when(f == 0)
    def _():
        xn_scr[...] = _rms(h_ref[...], pre_g_ref[...]).astype(BF)
        acc_scr[...] = jnp.zeros_like(acc_scr)

    def accumulate(lo):
        xn = xn_scr[...]
        g = _dot(xn, wg_ref[:, lo:])
        u = _dot(xn, wu_ref[:, lo:])
        a = (g * jax.nn.sigmoid(g) * u).astype(BF)
        acc_scr[...] += _dot(a, wd_ref[lo:, :])

    @pl.when(f < last)
    def _():
        accumulate(0)

    @pl.when(f == last)
    def _():
        accumulate(tf - tail)
        o_ref[...] = h_ref[...] + 0.5 * _rms(acc_scr[...], post_g_ref[...])

    _run_cast_jobs(i * pl.num_programs(1) + f, metas, job_in, job_out)


def _ffn(h, pre_g, wgu, wd, post_g, jobs=(), *, tm=512, tf=FFN_TF):
    M, D = h.shape
    F = wd.shape[0]
    nf = -(-F // tf)
    tail = F - (nf - 1) * tf
    assert wgu.shape == (D, 2 * F) and F % LANES == 0 and tf % LANES == 0 and tail % LANES == 0
    start = lambda f, base=0: (base // LANES + jnp.minimum(f * (tf // LANES), (F - tf) // LANES)) * LANES
    j_in, j_args, j_out, j_shapes, metas = _job_plumbing(jobs, lambda i, f: i * nf + f, M // tm * nf)
    return pl.pallas_call(
        functools.partial(_ffn_kernel, tail=tail, metas=metas),
        grid=(M // tm, nf),
        in_specs=[
            pl.BlockSpec((tm, D), lambda i, f: (i, 0)),
            pl.BlockSpec((1, D), lambda i, f: (0, 0)),
            pl.BlockSpec((pl.Element(D), pl.Element(tf)), lambda i, f: (0, start(f))),
            pl.BlockSpec((pl.Element(D), pl.Element(tf)), lambda i, f: (0, start(f, F))),
            pl.BlockSpec((pl.Element(tf), pl.Element(D)), lambda i, f: (start(f), 0)),
            pl.BlockSpec((1, D), lambda i, f: (0, 0)),
        ] + j_in,
        out_specs=[pl.BlockSpec((tm, D), lambda i, f: (i, 0))] + j_out,
        out_shape=[jax.ShapeDtypeStruct((M, D), F32)] + j_shapes,
        scratch_shapes=[pltpu.VMEM((tm, D), BF), pltpu.VMEM((tm, D), F32)],
        compiler_params=_cparams("arbitrary", "arbitrary"),
        name="ffn",
    )(h, pre_g, wgu, wgu, wd, post_g, *j_args)


def _gelu_tanh(x):
    c = math.sqrt(2.0 / math.pi)
    return x * (0.5 * (1.0 + jnp.tanh(c * (x + 0.044715 * (x * x * x)))))


def _nmm_kernel(*refs, n_plain, metas):
    n_in = _n_job_inputs(metas)
    h_ref, g_ref, w_ref = refs[:3]
    job_in = refs[3:3 + n_in]
    o_ref, gate_ref = refs[3 + n_in:5 + n_in]
    job_out = refs[5 + n_in:5 + n_in + len(metas)]
    xn_scr = refs[-1]
    i, j = pl.program_id(0), pl.program_id(1)

    @pl.when(j == 0)
    def _():
        xn_scr[...] = _rms(h_ref[...], g_ref[...]).astype(BF)

    @pl.when(j < n_plain)
    def _():
        o_ref[...] = _dot_nt(xn_scr[...], w_ref[...])

    @pl.when(j >= n_plain)
    def _():
        gate_ref[...] = jax.nn.sigmoid(_dot_nt(xn_scr[...], w_ref[...])).astype(gate_ref.dtype)

    _run_cast_jobs(i * pl.num_programs(1) + j, metas, job_in, job_out)


def _norm_matmul(h, g, w_t, jobs=(), *, tm=1024, tn=1024):
    M, D = h.shape
    N = w_t.shape[0]
    nj = N // tn
    n_plain = COL_GATE // tn
    assert N == N_PROJ and nj * tn == N and n_plain * tn == COL_GATE
    j_in, j_args, j_out, j_shapes, metas = _job_plumbing(jobs, lambda i, j: i * nj + j, M // tm * nj)
    return pl.pallas_call(
        functools.partial(_nmm_kernel, n_plain=n_plain, metas=metas),
        grid=(M // tm, nj),
        in_specs=[
            pl.BlockSpec((tm, D), lambda i, j: (i, 0)),
            pl.BlockSpec((1, D), lambda i, j: (0, 0)),
            pl.BlockSpec((tn, D), lambda i, j: (j, 0)),
        ] + j_in,
        out_specs=[pl.BlockSpec((tm, tn), lambda i, j: (i, jnp.minimum(j, n_plain - 1))),
                   pl.BlockSpec((tm, tn), lambda i, j: (i, jnp.maximum(j - n_plain, 0)))] + j_out,
        out_shape=[jax.ShapeDtypeStruct((M, COL_GATE), F32),
                   jax.ShapeDtypeStruct((M, N - COL_GATE), BF)] + j_shapes,
        scratch_shapes=[pltpu.VMEM((tm, D), BF)],
        compiler_params=_cparams("arbitrary", "arbitrary"),
        name="in_proj",
    )(h, g, w_t, *j_args)


def _prep_kernel(*refs, seq, has_vres):
    if has_vres:
        (ps_ref, prev_ref, mu_ref, w0_ref, w2_ref, a0_ref, a2_ref, g2_ref, kk_ref, ka_ref,
         vf_ref, v0_ref, vw2_ref,
         r_o, k_o, v_o, lw_o, kk_o, a_o, g_o) = refs
    else:
        (ps_ref, prev_ref, mu_ref, w0_ref, w2_ref, a0_ref, a2_ref, g2_ref, kk_ref, ka_ref,
         r_o, k_o, v_o, lw_o, kk_o, a_o, g_o) = refs
    tm = ps_ref.shape[0]
    W = RWKV_WIDTH
    x = ps_ref[...]
    at_seq_start = (pl.program_id(0) * tm) % seq == 0
    last_prev = jnp.where(at_seq_start, 0.0, prev_ref[7:8, :])
    row = lax.broadcasted_iota(jnp.int32, x.shape, 0)
    prev = jnp.where(row == 0, last_prev, pltpu.roll(x, 1, axis=0))
    xs = x + (prev - x) * mu_ref[...]

    xr = xs[:, SEG_R:SEG_R + W]
    xk = xs[:, SEG_K:SEG_K + W]
    xv = xs[:, SEG_V:SEG_V + W]
    xwa = xs[:, SEG_WA:SEG_WA + 128]
    xg = xs[:, SEG_G:SEG_G + 256]

    z = w0_ref[...] + _dot(jnp.tanh(xwa).astype(BF), w2_ref[...])
    w_log = -jax.nn.softplus(-z) - 0.5
    lw_o[...] = -jnp.exp(w_log)
    a = jax.nn.sigmoid(a0_ref[...] + _dot(xwa.astype(BF), a2_ref[...]))
    a_o[...] = a
    g_o[...] = _dot(jax.nn.sigmoid(xg).astype(BF), g2_ref[...]).astype(g_o.dtype)
    kk_o[...] = xk * kk_ref[...]
    k_o[...] = xk * (1.0 + (a - 1.0) * ka_ref[...])
    r_o[...] = xr
    if has_vres:
        pv = xs[:, SEG_P:SEG_P + 128]
        mix = jax.nn.sigmoid(v0_ref[...] + _dot(pv.astype(BF), vw2_ref[...]))
        v_o[...] = xv + (vf_ref[...] - xv) * mix
    else:
        v_o[...] = xv


def _rwkv_prep(ps, mu, w0, w2, a0, a2, g2, k_k, k_a, vres, *, seq, tm=256):
    M = ps.shape[0]
    NA = COL_QKV
    W = RWKV_WIDTH
    row = lambda n: pl.BlockSpec((1, n), lambda i: (0, 0))
    full = lambda a: pl.BlockSpec(a.shape, lambda i: (0, 0))
    tok = pl.BlockSpec((tm, W), lambda i: (i, 0))
    in_specs = [
        pl.BlockSpec((tm, NA), lambda i: (i, 0)),
        pl.BlockSpec((8, NA), lambda i: (jnp.maximum(i * (tm // 8) - 1, 0), 0)),
        row(NA), row(W), full(w2), row(W), full(a2), full(g2), row(W), row(W),
    ]
    args = [ps, ps, mu, w0, w2, a0, a2, g2, k_k, k_a]
    if vres is not None:
        v_first, v0, vw2 = vres
        in_specs += [tok, row(W), full(vw2)]
        args += [v_first, v0, vw2]
    return pl.pallas_call(
        functools.partial(_prep_kernel, seq=seq, has_vres=vres is not None),
        grid=(M // tm,),
        in_specs=in_specs,
        out_specs=[tok] * 7,
        out_shape=[jax.ShapeDtypeStruct((M, W), F32)] * 6 + [jax.ShapeDtypeStruct((M, W), BF)],
        compiler_params=_cparams("parallel"),
        name="rwkv_prep",
    )(*args)


def _wkv_kernel(r_ref, k_ref, v_ref, lw_ref, kk_ref, a_ref, g_ref, rk_ref, lnw_ref, lnb_ref,
                y_ref, p_scr):
    C = WKV_CHUNK
    C2 = 2 * C

    @pl.when(pl.program_id(1) == 0)
    def _():
        p_scr[...] = jnp.zeros_like(p_scr)

    lane = lax.broadcasted_iota(jnp.int32, (C, LANES), 1)
    lo = lane < RWKV_HEAD_DIM
    ti = lax.broadcasted_iota(jnp.int32, (C, C), 0)
    tj = lax.broadcasted_iota(jnp.int32, (C, C), 1)
    tri = jnp.where(ti >= tj, 1.0, 0.0).astype(BF)
    tq = lax.broadcasted_iota(jnp.int32, (C, LANES), 0)
    sq = lane % C
    strict = tq > sq
    incl = tq >= sq
    eye_c = jnp.where(tq == sq, 1.0, 0.0)
    si = lax.broadcasted_iota(jnp.int32, (C2, LANES), 0)
    sj = lax.broadcasted_iota(jnp.int32, (C2, LANES), 1)
    eye = si == sj

    def seg_sum(x):
        s_lo = jnp.sum(jnp.where(lo, x, 0.0), axis=1, keepdims=True)
        s_hi = jnp.sum(jnp.where(lo, 0.0, x), axis=1, keepdims=True)
        return jnp.where(lo, s_lo, s_hi)

    def stack(x):
        return jnp.concatenate([jnp.where(lo, x, 0.0), jnp.where(lo, 0.0, x)], axis=0)

    def bstack(x):
        return stack(x).astype(BF)

    n_pairs = r_ref.shape[1] // LANES
    n_chunks = r_ref.shape[0] // C
    items = [(c, p) for c in range(n_chunks) for p in range(n_pairs)]
    rows = [slice(c * C, (c + 1) * C) for c, _ in items]
    cols = [slice(p * LANES, (p + 1) * LANES) for _, p in items]
    each = lambda f, *xs: [f(*t) for t in zip(*xs)]
    bf = lambda xs: [x.astype(BF) for x in xs]
    load = lambda ref: [ref[rs, cs] for rs, cs in zip(rows, cols)]

    r, k, v, lw, kkr, a = (load(ref) for ref in (r_ref, k_ref, v_ref, lw_ref, kk_ref, a_ref))

    def cumsum(x):
        hi = x.astype(BF)
        both = _dot(tri, jnp.concatenate([hi, (x - hi.astype(F32)).astype(BF)], axis=1))
        return both[:, :LANES] + both[:, LANES:]

    cum = each(cumsum, lw)
    tot = [c[C - 1:C, :] for c in cum]
    kk = each(lambda x: x / jnp.maximum(jnp.sqrt(seg_sum(x * x)), 1e-12), kkr)
    kka = each(lambda x, y: x * y, kk, a)
    A_c = each(lambda x, c, w: -x * jnp.exp(c - w), kk, cum, lw)
    R_c = bf(each(lambda x, c: x * jnp.exp(c), r, cum))
    AR = each(lambda a_, r_: jnp.concatenate([a_.astype(BF), r_], axis=0), A_c, R_c)
    BK_st = each(lambda x, y, c: jnp.concatenate([bstack(x * jnp.exp(-c)), bstack(y * jnp.exp(-c))], axis=0),
                 kka, k, cum)
    V_st = each(bstack, v)
    BKpT = bf(each(lambda x, y, c, t: jnp.concatenate([stack(x * jnp.exp(t - c)).T,
                                                       stack(y * jnp.exp(t - c)).T], axis=1),
                   kka, k, cum, tot))
    gcol = [jnp.exp(jnp.sum(jnp.where(eye, jnp.broadcast_to(t, (C2, LANES)), 0.0), axis=1, keepdims=True))
            for t in tot]

    G = each(_dot_nt, AR, BK_st)
    Aab = [jnp.where(strict, x[:C, :LANES], 0.0) for x in G]
    Aak = bf([jnp.where(strict, x[:C, LANES:], 0.0) for x in G])
    Mrb = bf([jnp.where(incl, x[C:, :LANES], 0.0) for x in G])
    Mrk = bf([jnp.where(incl, x[C:, LANES:], 0.0) for x in G])
    AkV = each(_dot, Aak, V_st)

    S = [eye_c + x for x in Aab]
    Pw = each(lambda x: _dot(x.astype(BF), bstack(x)), Aab)
    n = 2
    while 2 * n < C:
        prod = each(lambda p_, s_: _dot(p_.astype(BF), jnp.concatenate([bstack(p_), bstack(s_)], axis=1)),
                    Pw, S)
        S = each(lambda s_, pr: s_ + pr[:, LANES:], S, prod)
        Pw = [pr[:, :LANES] for pr in prod]
        n *= 2
    S = each(lambda s_, p_: s_ + _dot(p_.astype(BF), bstack(s_)), S, Pw)
    WU = each(lambda s_, a_, u0: _dot(s_.astype(BF), jnp.concatenate([bstack(a_), bstack(u0)], axis=1)),
              S, A_c, AkV)

    def finish(y_, r_, k_, v_, rs, cs):
        mu = seg_sum(y_) * (1.0 / RWKV_HEAD_DIM)
        yc = y_ - mu
        var = seg_sum(yc * yc) * (1.0 / RWKV_HEAD_DIM)
        yn = yc * lax.rsqrt(var + LNX_EPS) * lnw_ref[:, cs] + lnb_ref[:, cs]
        bonus = seg_sum(r_ * k_ * rk_ref[:, cs]) * v_
        return ((yn + bonus) * g_ref[rs, cs]).astype(y_ref.dtype)

    P = [p_scr[p] for p in range(n_pairs)]
    for c in range(n_chunks):
        sel = lambda xs: xs[c * n_pairs:(c + 1) * n_pairs]
        Pb = bf(P)
        U = each(lambda wu, pb: _dot(wu[:, :LANES].astype(BF), pb) + wu[:, LANES:], sel(WU), Pb)
        U_st = each(bstack, U)
        y = each(lambda r_, mb, mk, pb, u, vs: _dot(jnp.concatenate([r_, mb, mk], axis=1),
                                                    jnp.concatenate([pb, u, vs], axis=0)),
                 sel(R_c), sel(Mrb), sel(Mrk), Pb, U_st, sel(V_st))
        P = each(lambda gc, p_, m, u, vs: gc * p_ + _dot(m, jnp.concatenate([u, vs], axis=0)),
                 sel(gcol), P, sel(BKpT), U_st, sel(V_st))
        out = each(finish, y, sel(r), sel(k), sel(v), sel(rows), sel(cols))
        for o, rs, cs in zip(out, sel(rows), sel(cols)):
            y_ref[rs, cs] = o
    for p in range(n_pairs):
        p_scr[p] = P[p]


def _wkv(r, k, v, lw, kkr, a, g, r_k, lnx_w, lnx_b, *, batch, seq, chunks_per_step=4):
    M, W = r.shape
    rows = chunks_per_step * WKV_CHUNK
    steps = seq // rows
    tok = pl.BlockSpec((rows, W), lambda b, c: (b * steps + c, 0))
    par = pl.BlockSpec((1, W), lambda b, c: (0, 0))
    return pl.pallas_call(
        _wkv_kernel,
        grid=(batch, steps),
        in_specs=[tok] * 7 + [par] * 3,
        out_specs=tok,
        out_shape=jax.ShapeDtypeStruct((M, W), BF),
        scratch_shapes=[pltpu.VMEM((W // LANES, LANES, LANES), F32)],
        compiler_params=_cparams("parallel", "arbitrary"),
        name="wkv",
    )(r, k, v, lw, kkr, a, g, r_k, lnx_w, lnx_b)


def _attn_kernel(slopes_ref, q_ref, k_ref, v_ref, o_ref, og_scr, lse_scr, *, seq):
    slot = pl.program_id(1)
    grp = pl.program_id(2)
    blk = ATTN_BLK
    scale = ATTN_HEAD_DIM ** -0.5
    qi = lax.broadcasted_iota(jnp.int32, (blk, 2 * blk), 0)
    kj = lax.broadcasted_iota(jnp.int32, (blk, 2 * blk), 1)
    rel = (qi + blk - kj).astype(F32)
    in_band = (kj >= qi) & (kj <= qi + blk)
    is_prev = kj < blk
    nb = ATTN_BLOCKS_PER_ITER
    each = lambda f, *xs: [f(*t) for t in zip(*xs)]

    def rows(start, d):
        if d == 1:
            return pl.ds(start, blk)
        return pl.ds(start, blk, stride=d)

    for gi, (window, d) in enumerate(ATTN_GROUPS):
        assert window // d == blk and seq % window == 0 and (seq // blk) % nb == 0

        @pl.when(grp == gi)
        def _(gi=gi, d=d):
            slope = slopes_ref[gi * ATTN_HEADS_PER_GROUP + slot]
            bias = jnp.where(in_band, -slope * float(d) * rel, NEG_BIG)
            span = blk * d

            def body(it, carry):
                ids = [it * nb + j for j in range(nb)]
                ns = [i // d for i in ids]
                cur = [rows(n * span + i % d, d) for n, i in zip(ns, ids)]
                prv = [rows(jnp.maximum(n - 1, 0) * span + i % d, d) for n, i in zip(ns, ids)]
                q = [q_ref[c, :].astype(BF) for c in cur]
                kcat = [jnp.concatenate([k_ref[p_, :], k_ref[c, :]], axis=0).astype(BF) for p_, c in zip(prv, cur)]
                vcat = [jnp.concatenate([v_ref[p_, :], v_ref[c, :]], axis=0).astype(BF) for p_, c in zip(prv, cur)]
                s = each(lambda q_, k_: _dot_nt(q_, k_) * scale + bias, q, kcat)
                s = each(lambda s_, n: jnp.where(is_prev & (n == 0), NEG_BIG, s_), s, ns)
                m = [jnp.max(s_, axis=1, keepdims=True) for s_ in s]
                p = each(lambda s_, m_: jnp.exp(s_ - m_), s, m)
                l = [jnp.sum(p_, axis=1, keepdims=True) for p_ in p]
                o = each(lambda p_, v_, l_: _dot(p_.astype(BF), v_) / l_, p, vcat, l)
                lse = each(lambda m_, l_: m_ + jnp.log(l_), m, l)
                for c, o_, e_ in zip(cur, o, lse):
                    og_scr[gi, c, :] = o_
                    lse_scr[gi, c, :] = jnp.broadcast_to(e_, (blk, LANES))
                return carry

            lax.fori_loop(0, seq // blk // nb, body, 0)

    @pl.when(grp == len(ATTN_GROUPS) - 1)
    def _():
        tile = 8 * blk

        def merge(t, carry):
            rs = pl.ds(pl.multiple_of(t * tile, tile), tile)
            lses = [lse_scr[g, rs, :] for g in range(len(ATTN_GROUPS))]
            top = functools.reduce(jnp.maximum, lses)
            w = [jnp.exp(e - top) for e in lses]
            num = sum(w_ * og_scr[g, rs, :] for g, w_ in enumerate(w))
            o_ref[rs, :] = (num / sum(w)).astype(o_ref.dtype)
            return carry

        lax.fori_loop(0, seq // tile, merge, 0)


def _attention(qkv, slopes, *, batch, seq):
    M = qkv.shape[0]
    E = ATTN_HEAD_DIM
    hpg = ATTN_HEADS_PER_GROUP

    def spec(which):
        first = COL_QKV // E + which * ATTN_HEADS
        return pl.BlockSpec((seq, E), lambda b, s, g: (b, first + g * hpg + s))

    return pl.pallas_call(
        functools.partial(_attn_kernel, seq=seq),
        grid=(batch, hpg, len(ATTN_GROUPS)),
        in_specs=[pl.BlockSpec(memory_space=pltpu.SMEM), spec(0), spec(1), spec(2)],
        out_specs=pl.BlockSpec((seq, E), lambda b, s, g: (b, s)),
        out_shape=jax.ShapeDtypeStruct((M, hpg * E), BF),
        scratch_shapes=[pltpu.VMEM((len(ATTN_GROUPS), seq, LANES), F32)] * 2,
        compiler_params=_cparams("parallel", "parallel", "arbitrary"),
        name="dilated_attn",
    )(slopes, qkv, qkv, qkv)


def _sgu_kernel(x_ref, lg_ref, lb_ref, ws_ref, bt_ref, o_ref):
    tm = x_ref.shape[0]
    CH = SGU_CHUNK
    GD = SGU_WIDTH // SGU_GROUPS
    u = _gelu_tanh(x_ref[:, :SGU_WIDTH])
    v = _gelu_tanh(x_ref[:, SGU_WIDTH:])
    mu = jnp.mean(v, axis=-1, keepdims=True)
    vc = v - mu
    var = jnp.mean(vc * vc, axis=-1, keepdims=True)
    vn = (vc * lax.rsqrt(var + LN_EPS) * lg_ref[...] + lb_ref[...]).astype(BF)
    ti = lax.broadcasted_iota(jnp.int32, (CH, CH), 0)
    si = lax.broadcasted_iota(jnp.int32, (CH, CH), 1)
    causal = ti >= si
    for gidx in range(SGU_GROUPS):
        w = jnp.where(causal, ws_ref[gidx], 0.0).astype(BF)
        bias = bt_ref[:, gidx:gidx + 1]
        cs = slice(gidx * GD, (gidx + 1) * GD)
        for c in range(tm // CH):
            rs = slice(c * CH, (c + 1) * CH)
            mixed = _dot(w, vn[rs, cs]) + bias
            o_ref[rs, cs] = (u[rs, cs] * mixed).astype(o_ref.dtype)


def _sgu(proj, ln_g, ln_b, w_s, b_t, *, tm=512):
    M = proj.shape[0]
    return pl.pallas_call(
        _sgu_kernel,
        grid=(M // tm,),
        in_specs=[
            pl.BlockSpec((tm, N_SGU), lambda i: (i, COL_SGU // N_SGU)),
            pl.BlockSpec((1, SGU_WIDTH), lambda i: (0, 0)),
            pl.BlockSpec((1, SGU_WIDTH), lambda i: (0, 0)),
            pl.BlockSpec(w_s.shape, lambda i: (0, 0, 0)),
            pl.BlockSpec(b_t.shape, lambda i: (0, 0)),
        ],
        out_specs=pl.BlockSpec((tm, SGU_WIDTH), lambda i: (i, 0)),
        out_shape=jax.ShapeDtypeStruct((M, SGU_WIDTH), BF),
        compiler_params=_cparams("parallel"),
        name="sgu",
    )(proj, ln_g, ln_b, w_s, b_t)


def _merge_kernel(h_ref, yr_ref, ya_ref, ys_ref, g0_ref, g1_ref, g2_ref, wr_ref, wa_ref, ws_ref,
                  wo_ref, pg_ref, o_ref, acc_scr):
    j = pl.program_id(1)

    @pl.when(j == 0)
    def _():
        acc_scr[...] = jnp.zeros_like(acc_scr)

    merged = (g0_ref[...] * _dot(yr_ref[...], wr_ref[...])
              + g1_ref[...] * _dot(ya_ref[...], wa_ref[...])
              + g2_ref[...] * _dot(ys_ref[...], ws_ref[...]))
    acc_scr[...] += _dot(merged.astype(BF), wo_ref[...])

    @pl.when(j == pl.num_programs(1) - 1)
    def _():
        o_ref[...] = h_ref[...] + _rms(acc_scr[...], pg_ref[...])


def _merge(h, yr, ya, ys, gates, wr, wa, ws, wo, post_g, *, tm=512, tn=1024):
    M, D = h.shape
    nj = D // tn
    act = lambda a: pl.BlockSpec((tm, a.shape[1]), lambda i, j: (i, 0))
    gate = lambda br: pl.BlockSpec((tm, tn), lambda i, j: (i, br * nj + j))
    wcol = lambda a: pl.BlockSpec((a.shape[0], tn), lambda i, j: (0, j))
    return pl.pallas_call(
        _merge_kernel,
        grid=(M // tm, nj),
        in_specs=[
            pl.BlockSpec((tm, D), lambda i, j: (i, 0)),
            act(yr), act(ya), act(ys), gate(0), gate(1), gate(2),
            wcol(wr), wcol(wa), wcol(ws),
            pl.BlockSpec((tn, D), lambda i, j: (j, 0)),
            pl.BlockSpec((1, D), lambda i, j: (0, 0)),
        ],
        out_specs=pl.BlockSpec((tm, D), lambda i, j: (i, 0)),
        out_shape=jax.ShapeDtypeStruct((M, D), F32),
        scratch_shapes=[pltpu.VMEM((tm, D), F32)],
        compiler_params=_cparams("parallel", "arbitrary"),
        name="merge_out",
    )(h, yr, ya, ys, gates, gates, gates, wr, wa, ws, wo, post_g)


def _alibi_slopes(n_heads):
    def geometric(n):
        start = 2.0 ** (-8.0 / n)
        return [start ** (i + 1) for i in range(n)]
    closest = 2 ** int(math.floor(math.log2(n_heads)))
    slopes = geometric(closest)
    if closest < n_heads:
        slopes += geometric(2 * closest)[0::2][: n_heads - closest]
    return np.array(sorted(slopes, reverse=True), dtype=np.float32)


def _pad_rows(x, before, total):
    return jnp.pad(x, ((before, total - before - x.shape[0]), (0, 0)))


def _w_in_job(w_in_t, extra_t, l, *, rows=128):
    n_in = w_in_t.shape[1]
    assert n_in - N_SHIFT == N_PROJ - COL_QKV and (COL_QKV - N_SHIFT) % SUBLANES == 0
    assert COL_QKV % rows == 0 and N_PROJ % rows == 0 and SEG_P % rows + LANES <= rows
    n_head = COL_QKV // rows

    def offset(s):
        return s * (rows // SUBLANES) - jnp.where(s < n_head, 0, (COL_QKV - N_SHIFT) // SUBLANES)

    return _CastJob(w_in_t, l, rows, N_PROJ // rows, offset, extra_t, SEG_P // rows, SEG_P % rows)


def kernel(x, ffn1_pre_g, ffn1_w_gu, ffn1_w_down, ffn1_post_g, mix_pre_g, w_in, shift_mu, decay_w0, decay_w2, iclr_a0, iclr_a2, gate_g2, k_k, k_a, r_k, lnx_w, lnx_b, vres_w1, vres_mu, vres_v0, vres_w2, sgu_ln_g, sgu_ln_b, sgu_w_s, sgu_b, w_b_rwkv, w_b_attn, w_b_sgu, w_out, mix_post_g, ffn2_pre_g, ffn2_w_gu, ffn2_w_down, ffn2_post_g):
    B, T, D = x.shape
    depth = w_in.shape[0]
    slopes = jnp.asarray(_alibi_slopes(ATTN_HEADS))
    row = lambda p: p.reshape(1, -1)
    ffn_jobs = lambda w_gu, w_down, l, fine=False: [_plain_job(w_gu, l, 16 if fine else 32),
                                                    _plain_job(w_down, l, 64 if fine else 128)]

    w_in_t = jnp.swapaxes(w_in, 1, 2)
    h = x.reshape(B * T, D)
    v_first = None
    wgu = _cast(_plain_job(ffn1_w_gu, 0, 128))
    wd = _cast(_plain_job(ffn1_w_down, 0, ffn1_w_down.shape[1] // 8))
    for l in range(depth):
        if l == 0:
            extra_t = jnp.zeros((LANES, D), F32)
            extra_mu = jnp.zeros((LANES,), F32)
        else:
            extra_t = _pad_rows(vres_w1[l - 1].T, 0, LANES)
            extra_mu = jnp.pad(vres_mu[l - 1], (0, LANES - VRES_LORA))
        jobs = [_w_in_job(w_in_t, extra_t, l)] + [_plain_job(w, l, 64)
                                                    for w in (w_b_rwkv, w_b_attn, w_b_sgu, w_out)]
        h, w_proj, wr, wa, ws, wo = _ffn(h, row(ffn1_pre_g[l]), wgu, wd, row(ffn1_post_g[l]), jobs)

        mu = jnp.concatenate([jnp.pad(shift_mu[l], (0, SEG_P - N_SHIFT)), extra_mu]).reshape(1, -1)
        proj, gates, wgu, wd = _norm_matmul(h, row(mix_pre_g[l]), w_proj,
                                            ffn_jobs(ffn2_w_gu, ffn2_w_down, l))

        vres = None
        if l > 0:
            vres = (v_first, row(vres_v0[l - 1]), _pad_rows(vres_w2[l - 1], 0, LANES).astype(BF))
        r, k, v, lw, kkr, a, g = _rwkv_prep(
            proj, mu, row(decay_w0[l]), _pad_rows(decay_w2[l], 0, LANES).astype(BF),
            row(iclr_a0[l]), _pad_rows(iclr_a2[l], DECAY_LORA, LANES).astype(BF),
            _pad_rows(gate_g2[l], 0, 2 * LANES).astype(BF), row(k_k[l]), row(k_a[l]), vres, seq=T)
        if l == 0:
            v_first = v
        y_rwkv = _wkv(r, k, v, lw, kkr, a, g, row(r_k[l]), row(lnx_w[l]), row(lnx_b[l]),
                      batch=B, seq=T)

        y_attn = _attention(proj, slopes, batch=B, seq=T)
        y_sgu = _sgu(proj, row(sgu_ln_g[l]), row(sgu_ln_b[l]), sgu_w_s[l], sgu_b[l].T)

        h = _merge(h, y_rwkv, y_attn, y_sgu, gates, wr, wa, ws, wo, row(mix_post_g[l]))

        jobs = ffn_jobs(ffn1_w_gu, ffn1_w_down, l + 1, fine=True) if l + 1 < depth else []
        h, *cast = _ffn(h, row(ffn2_pre_g[l]), wgu, wd, row(ffn2_post_g[l]), jobs)
        if cast:
            wgu, wd = cast
    return h.reshape(B, T, D)
```

```python
import functools
import math
from typing import Callable, NamedTuple, Optional

import numpy as np
import jax
import jax.numpy as jnp
from jax import lax
from jax.experimental import pallas as pl
from jax.experimental.pallas import tpu as pltpu

BF = jnp.bfloat16
F32 = jnp.float32

D_MODEL = 2048
RWKV_HEAD_DIM = 64
RWKV_WIDTH = D_MODEL // 2
DECAY_LORA = 64
ICLR_LORA = 64
VRES_LORA = 32
GATE_LORA = 160
LNX_EPS = 64e-5
ATTN_GROUPS = ((128, 1), (512, 4), (2048, 16))
ATTN_HEADS_PER_GROUP = 4
ATTN_HEAD_DIM = 128
ATTN_HEADS = len(ATTN_GROUPS) * ATTN_HEADS_PER_GROUP
ATTN_BLK = 128
ATTN_BLOCKS_PER_ITER = 8
SGU_CHUNK = 128
SGU_GROUPS = 4
SGU_WIDTH = D_MODEL // 4
RMS_EPS = 1e-6
LN_EPS = 1e-5
N_BRANCHES = 3

LANES = 128
SUBLANES = 8
WKV_CHUNK = 64
FFN_TF = 512
NEG_BIG = -1e30

N_SHIFT = 3 * RWKV_WIDTH + DECAY_LORA + ICLR_LORA + GATE_LORA
SEG_R, SEG_K, SEG_V = 0, RWKV_WIDTH, 2 * RWKV_WIDTH
SEG_WA = 3 * RWKV_WIDTH
SEG_G = SEG_WA + 128
SEG_P = SEG_G + 256
N_QKV = 3 * ATTN_HEADS * ATTN_HEAD_DIM
N_SGU = 2 * SGU_WIDTH
N_GATE = N_BRANCHES * D_MODEL
COL_SHIFT = 0
COL_QKV = SEG_P + LANES
COL_SGU = COL_QKV + N_QKV
COL_GATE = COL_SGU + N_SGU
N_PROJ = COL_GATE + N_GATE

VMEM_LIMIT = 60 * 1024 * 1024


def _cparams(*sem):
    return pltpu.CompilerParams(dimension_semantics=sem, vmem_limit_bytes=VMEM_LIMIT)


def _dot(a, b):
    return jnp.dot(a, b, preferred_element_type=F32)


def _dot_nt(a, b):
    return lax.dot_general(a, b, (((1,), (1,)), ((), ())), preferred_element_type=F32)


def _rms(x, g):
    return x * lax.rsqrt(jnp.mean(x * x, axis=-1, keepdims=True) + RMS_EPS) * g


def _sigmoid(x):
    return 0.5 * jnp.tanh(0.5 * x) + 0.5


class _CastJob(NamedTuple):
    src: jax.Array
    layer: int
    rows: int
    n_blocks: int
    offset: Callable
    patch: Optional[jax.Array] = None
    patch_block: int = 0
    patch_row: int = 0


class _CastMeta(NamedTuple):
    n_blocks: int
    has_patch: bool
    patch_block: int
    patch_row: int


def _plain_job(src, layer, rows):
    R = src.shape[1]
    assert R % rows == 0 and rows % (2 * SUBLANES) == 0
    return _CastJob(src, layer, rows, R // rows, lambda s: s * (rows // SUBLANES))


def _job_plumbing(jobs, step_of, n_steps):
    in_specs, args, out_specs, out_shapes, metas = [], [], [], [], []
    for job in jobs:
        assert job.n_blocks <= n_steps, "carrier kernel has too few grid steps for this cast job"
        C = job.src.shape[2]
        blk = lambda *g, job=job: jnp.minimum(step_of(*g), job.n_blocks - 1)
        in_specs.append(pl.BlockSpec(
            (None, pl.Element(job.rows), pl.Element(C)),
            lambda *g, job=job, blk=blk: (job.layer, job.offset(blk(*g)) * SUBLANES, 0)))
        args.append(job.src)
        if job.patch is not None:
            in_specs.append(pl.BlockSpec(job.patch.shape, lambda *g: (0, 0)))
            args.append(job.patch)
        out_specs.append(pl.BlockSpec((job.rows, C), lambda *g, blk=blk: (blk(*g), 0)))
        out_shapes.append(jax.ShapeDtypeStruct((job.n_blocks * job.rows, C), BF))
        metas.append(_CastMeta(job.n_blocks, job.patch is not None, job.patch_block, job.patch_row))
    return in_specs, args, out_specs, out_shapes, tuple(metas)


def _run_cast_jobs(step, metas, in_refs, out_refs):
    in_refs = iter(in_refs)
    for meta, dst in zip(metas, out_refs):
        src = next(in_refs)
        patch = next(in_refs) if meta.has_patch else None

        @pl.when(step < meta.n_blocks)
        def _(meta=meta, src=src, patch=patch, dst=dst):
            dst[...] = src[...].astype(BF)
            if patch is not None:
                @pl.when(step == meta.patch_block)
                def _():
                    dst[pl.ds(meta.patch_row, LANES), :] = patch[...].astype(BF)


def _n_job_inputs(metas):
    return sum(2 if m.has_patch else 1 for m in metas)


def _cast_kernel(*refs, metas):
    n_in = _n_job_inputs(metas)
    _run_cast_jobs(pl.program_id(0), metas, refs[:n_in], refs[n_in:])


def _cast(job):
    in_specs, args, out_specs, out_shapes, metas = _job_plumbing([job], lambda s: s, job.n_blocks)
    return pl.pallas_call(
        functools.partial(_cast_kernel, metas=metas),
        grid=(job.n_blocks,),
        in_specs=in_specs,
        out_specs=out_specs,
        out_shape=out_shapes,
        compiler_params=_cparams("arbitrary"),
        name="cast",
    )(*args)[0]


def _ffn_kernel(*refs, tail, metas):
    n_in = _n_job_inputs(metas)
    h_ref, pre_g_ref, wg_ref, wu_ref, wd_ref, post_g_ref = refs[:6]
    job_in = refs[6:6 + n_in]
    o_ref = refs[6 + n_in]
    job_out = refs[7 + n_in:7 + n_in + len(metas)]
    xn_scr, acc_scr = refs[7 + n_in + len(metas):]
    i, f = pl.program_id(0), pl.program_id(1)
    last = pl.num_programs(1) - 1
    tf = wd_ref.shape[0]

    @pl.when(f == 0)
    def _():
        xn_scr[...] = _rms(h_ref[...], pre_g_ref[...]).astype(BF)
        acc_scr[...] = jnp.zeros_like(acc_scr)

    def accumulate(lo):
        xn = xn_scr[...]
        g = _dot(xn, wg_ref[:, lo:])
        u = _dot(xn, wu_ref[:, lo:])
        a = (g * _sigmoid(g) * u).astype(BF)
        acc_scr[...] += _dot(a, wd_ref[lo:, :])

    @pl.when(f < last)
    def _():
        accumulate(0)

    @pl.when(f == last)
    def _():
        accumulate(tf - tail)
        o_ref[...] = h_ref[...] + 0.5 * _rms(acc_scr[...], post_g_ref[...])

    _run_cast_jobs(i * pl.num_programs(1) + f, metas, job_in, job_out)


def _ffn(h, pre_g, wgu, wd, post_g, jobs=(), *, tm=512, tf=FFN_TF):
    M, D = h.shape
    F = wd.shape[0]
    nf = -(-F // tf)
    tail = F - (nf - 1) * tf
    assert wgu.shape == (D, 2 * F) and F % LANES == 0 and tf % LANES == 0 and tail % LANES == 0
    start = lambda f, base=0: (base // LANES + jnp.minimum(f * (tf // LANES), (F - tf) // LANES)) * LANES
    j_in, j_args, j_out, j_shapes, metas = _job_plumbing(jobs, lambda i, f: i * nf + f, M // tm * nf)
    return pl.pallas_call(
        functools.partial(_ffn_kernel, tail=tail, metas=metas),
        grid=(M // tm, nf),
        in_specs=[
            pl.BlockSpec((tm, D), lambda i, f: (i, 0)),
            pl.BlockSpec((1, D), lambda i, f: (0, 0)),
            pl.BlockSpec((pl.Element(D), pl.Element(tf)), lambda i, f: (0, start(f))),
            pl.BlockSpec((pl.Element(D), pl.Element(tf)), lambda i, f: (0, start(f, F))),
            pl.BlockSpec((pl.Element(tf), pl.Element(D)), lambda i, f: (start(f), 0)),
            pl.BlockSpec((1, D), lambda i, f: (0, 0)),
        ] + j_in,
        out_specs=[pl.BlockSpec((tm, D), lambda i, f: (i, 0))] + j_out,
        out_shape=[jax.ShapeDtypeStruct((M, D), F32)] + j_shapes,
        scratch_shapes=[pltpu.VMEM((tm, D), BF), pltpu.VMEM((tm, D), F32)],
        compiler_params=_cparams("arbitrary", "arbitrary"),
        name="ffn",
    )(h, pre_g, wgu, wgu, wd, post_g, *j_args)


def _gelu_tanh(x):
    c = math.sqrt(2.0 / math.pi)
    return x * (0.5 * (1.0 + jnp.tanh(c * (x + 0.044715 * (x * x * x)))))


def _nmm_kernel(*refs, n_plain, metas):
    n_in = _n_job_inputs(metas)
    h_ref, g_ref, w_ref = refs[:3]
    job_in = refs[3:3 + n_in]
    o_ref, gate_ref = refs[3 + n_in:5 + n_in]
    job_out = refs[5 + n_in:5 + n_in + len(metas)]
    xn_scr = refs[-1]
    i, j = pl.program_id(0), pl.program_id(1)

    @pl.when(j == 0)
    def _():
        xn_scr[...] = _rms(h_ref[...], g_ref[...]).astype(BF)

    @pl.when(j < n_plain)
    def _():
        o_ref[...] = _dot_nt(xn_scr[...], w_ref[...])

    @pl.when(j >= n_plain)
    def _():
        gate_ref[...] = _sigmoid(_dot_nt(xn_scr[...], w_ref[...])).astype(gate_ref.dtype)

    _run_cast_jobs(i * pl.num_programs(1) + j, metas, job_in, job_out)


def _norm_matmul(h, g, w_t, jobs=(), *, tm=1024, tn=1024):
    M, D = h.shape
    N = w_t.shape[0]
    nj = N // tn
    n_plain = COL_GATE // tn
    assert N == N_PROJ and nj * tn == N and n_plain * tn == COL_GATE
    j_in, j_args, j_out, j_shapes, metas = _job_plumbing(jobs, lambda i, j: i * nj + j, M // tm * nj)
    return pl.pallas_call(
        functools.partial(_nmm_kernel, n_plain=n_plain, metas=metas),
        grid=(M // tm, nj),
        in_specs=[
            pl.BlockSpec((tm, D), lambda i, j: (i, 0)),
            pl.BlockSpec((1, D), lambda i, j: (0, 0)),
            pl.BlockSpec((tn, D), lambda i, j: (j, 0)),
        ] + j_in,
        out_specs=[pl.BlockSpec((tm, tn), lambda i, j: (i, jnp.minimum(j, n_plain - 1))),
                   pl.BlockSpec((tm, tn), lambda i, j: (i, jnp.maximum(j - n_plain, 0)))] + j_out,
        out_shape=[jax.ShapeDtypeStruct((M, COL_GATE), F32),
                   jax.ShapeDtypeStruct((M, N - COL_GATE), BF)] + j_shapes,
        scratch_shapes=[pltpu.VMEM((tm, D), BF)],
        compiler_params=_cparams("arbitrary", "arbitrary"),
        name="in_proj",
    )(h, g, w_t, *j_args)


def _prep_kernel(*refs, seq, has_vres):
    if has_vres:
        (ps_ref, prev_ref, mu_ref, w0_ref, w2_ref, a0_ref, a2_ref, g2_ref, kk_ref, ka_ref,
         vf_ref, v0_ref, vw2_ref,
         r_o, k_o, v_o, lw_o, kk_o, a_o, g_o) = refs
    else:
        (ps_ref, prev_ref, mu_ref, w0_ref, w2_ref, a0_ref, a2_ref, g2_ref, kk_ref, ka_ref,
         r_o, k_o, v_o, lw_o, kk_o, a_o, g_o) = refs
    tm = ps_ref.shape[0]
    W = RWKV_WIDTH
    x = ps_ref[...]
    at_seq_start = (pl.program_id(0) * tm) % seq == 0
    last_prev = jnp.where(at_seq_start, 0.0, prev_ref[7:8, :])
    row = lax.broadcasted_iota(jnp.int32, x.shape, 0)
    prev = jnp.where(row == 0, last_prev, pltpu.roll(x, 1, axis=0))
    xs = x + (prev - x) * mu_ref[...]

    xr = xs[:, SEG_R:SEG_R + W]
    xk = xs[:, SEG_K:SEG_K + W]
    xv = xs[:, SEG_V:SEG_V + W]
    xwa = xs[:, SEG_WA:SEG_WA + 128]
    xg = xs[:, SEG_G:SEG_G + 256]

    z = w0_ref[...] + _dot(jnp.tanh(xwa).astype(BF), w2_ref[...])
    w_log = -jax.nn.softplus(-z) - 0.5
    lw_o[...] = -jnp.exp(w_log)
    a = jax.nn.sigmoid(a0_ref[...] + _dot(xwa.astype(BF), a2_ref[...]))
    a_o[...] = a
    g_o[...] = _dot(jax.nn.sigmoid(xg).astype(BF), g2_ref[...]).astype(g_o.dtype)
    kk_o[...] = xk * kk_ref[...]
    k_o[...] = xk * (1.0 + (a - 1.0) * ka_ref[...])
    r_o[...] = xr
    if has_vres:
        pv = xs[:, SEG_P:SEG_P + 128]
        mix = jax.nn.sigmoid(v0_ref[...] + _dot(pv.astype(BF), vw2_ref[...]))
        v_o[...] = xv + (vf_ref[...] - xv) * mix
    else:
        v_o[...] = xv


def _rwkv_prep(ps, mu, w0, w2, a0, a2, g2, k_k, k_a, vres, *, seq, tm=256):
    M = ps.shape[0]
    NA = COL_QKV
    W = RWKV_WIDTH
    row = lambda n: pl.BlockSpec((1, n), lambda i: (0, 0))
    full = lambda a: pl.BlockSpec(a.shape, lambda i: (0, 0))
    tok = pl.BlockSpec((tm, W), lambda i: (i, 0))
    in_specs = [
        pl.BlockSpec((tm, NA), lambda i: (i, 0)),
        pl.BlockSpec((8, NA), lambda i: (jnp.maximum(i * (tm // 8) - 1, 0), 0)),
        row(NA), row(W), full(w2), row(W), full(a2), full(g2), row(W), row(W),
    ]
    args = [ps, ps, mu, w0, w2, a0, a2, g2, k_k, k_a]
    if vres is not None:
        v_first, v0, vw2 = vres
        in_specs += [tok, row(W), full(vw2)]
        args += [v_first, v0, vw2]
    return pl.pallas_call(
        functools.partial(_prep_kernel, seq=seq, has_vres=vres is not None),
        grid=(M // tm,),
        in_specs=in_specs,
        out_specs=[tok] * 7,
        out_shape=[jax.ShapeDtypeStruct((M, W), F32)] * 6 + [jax.ShapeDtypeStruct((M, W), BF)],
        compiler_params=_cparams("parallel"),
        name="rwkv_prep",
    )(*args)


def _wkv_kernel(r_ref, k_ref, v_ref, lw_ref, kk_ref, a_ref, g_ref, rk_ref, lnw_ref, lnb_ref,
                y_ref, p_scr):
    C = WKV_CHUNK
    C2 = 2 * C

    @pl.when(pl.program_id(1) == 0)
    def _():
        p_scr[...] = jnp.zeros_like(p_scr)

    lane = lax.broadcasted_iota(jnp.int32, (C, LANES), 1)
    lo = lane < RWKV_HEAD_DIM
    ti = lax.broadcasted_iota(jnp.int32, (C, C), 0)
    tj = lax.broadcasted_iota(jnp.int32, (C, C), 1)
    tri = jnp.where(ti >= tj, 1.0, 0.0).astype(BF)
    tq = lax.broadcasted_iota(jnp.int32, (C, LANES), 0)
    sq = lane % C
    strict = tq > sq
    incl = tq >= sq
    eye_c = jnp.where(tq == sq, 1.0, 0.0)
    si = lax.broadcasted_iota(jnp.int32, (C2, LANES), 0)
    sj = lax.broadcasted_iota(jnp.int32, (C2, LANES), 1)
    eye = si == sj

    def seg_sum(x):
        s_lo = jnp.sum(jnp.where(lo, x, 0.0), axis=1, keepdims=True)
        s_hi = jnp.sum(jnp.where(lo, 0.0, x), axis=1, keepdims=True)
        return jnp.where(lo, s_lo, s_hi)

    def stack(x):
        return jnp.concatenate([jnp.where(lo, x, 0.0), jnp.where(lo, 0.0, x)], axis=0)

    def bstack(x):
        return stack(x).astype(BF)

    n_pairs = r_ref.shape[1] // LANES
    n_chunks = r_ref.shape[0] // C
    items = [(c, p) for c in range(n_chunks) for p in range(n_pairs)]
    rows = [slice(c * C, (c + 1) * C) for c, _ in items]
    cols = [slice(p * LANES, (p + 1) * LANES) for _, p in items]
    each = lambda f, *xs: [f(*t) for t in zip(*xs)]
    bf = lambda xs: [x.astype(BF) for x in xs]
    load = lambda ref: [ref[rs, cs] for rs, cs in zip(rows, cols)]

    r, k, v, lw, kkr, a = (load(ref) for ref in (r_ref, k_ref, v_ref, lw_ref, kk_ref, a_ref))

    def cumsum(x):
        hi = x.astype(BF)
        both = _dot(tri, jnp.concatenate([hi, (x - hi.astype(F32)).astype(BF)], axis=1))
        return both[:, :LANES] + both[:, LANES:]

    cum = each(cumsum, lw)
    tot = [c[C - 1:C, :] for c in cum]
    kk = each(lambda x: x / jnp.maximum(jnp.sqrt(seg_sum(x * x)), 1e-12), kkr)
    kka = each(lambda x, y: x * y, kk, a)
    A_c = each(lambda x, c, w: -x * jnp.exp(c - w), kk, cum, lw)
    R_c = bf(each(lambda x, c: x * jnp.exp(c), r, cum))
    AR = each(lambda a_, r_: jnp.concatenate([a_.astype(BF), r_], axis=0), A_c, R_c)
    BK_st = each(lambda x, y, c: jnp.concatenate([bstack(x * jnp.exp(-c)), bstack(y * jnp.exp(-c))], axis=0),
                 kka, k, cum)
    V_st = each(bstack, v)
    BKpT = bf(each(lambda x, y, c, t: jnp.concatenate([stack(x * jnp.exp(t - c)).T,
                                                       stack(y * jnp.exp(t - c)).T], axis=1),
                   kka, k, cum, tot))
    gcol = [jnp.exp(jnp.sum(jnp.where(eye, jnp.broadcast_to(t, (C2, LANES)), 0.0), axis=1, keepdims=True))
            for t in tot]

    G = each(_dot_nt, AR, BK_st)
    Aab = [jnp.where(strict, x[:C, :LANES], 0.0) for x in G]
    Aak = bf([jnp.where(strict, x[:C, LANES:], 0.0) for x in G])
    Mrb = bf([jnp.where(incl, x[C:, :LANES], 0.0) for x in G])
    Mrk = bf([jnp.where(incl, x[C:, LANES:], 0.0) for x in G])
    AkV = each(_dot, Aak, V_st)

    S = [eye_c + x for x in Aab]
    Pw = each(lambda x: _dot(x.astype(BF), bstack(x)), Aab)
    n = 2
    while 2 * n < C:
        prod = each(lambda p_, s_: _dot(p_.astype(BF), jnp.concatenate([bstack(p_), bstack(s_)], axis=1)),
                    Pw, S)
        S = each(lambda s_, pr: s_ + pr[:, LANES:], S, prod)
        Pw = [pr[:, :LANES] for pr in prod]
        n *= 2
    S = each(lambda s_, p_: s_ + _dot(p_.astype(BF), bstack(s_)), S, Pw)
    WU = each(lambda s_, a_, u0: _dot(s_.astype(BF), jnp.concatenate([bstack(a_), bstack(u0)], axis=1)),
              S, A_c, AkV)

    def finish(y_, r_, k_, v_, rs, cs):
        mu = seg_sum(y_) * (1.0 / RWKV_HEAD_DIM)
        yc = y_ - mu
        var = seg_sum(yc * yc) * (1.0 / RWKV_HEAD_DIM)
        yn = yc * lax.rsqrt(var + LNX_EPS) * lnw_ref[:, cs] + lnb_ref[:, cs]
        bonus = seg_sum(r_ * k_ * rk_ref[:, cs]) * v_
        return ((yn + bonus) * g_ref[rs, cs]).astype(y_ref.dtype)

    P = [p_scr[p] for p in range(n_pairs)]
    for c in range(n_chunks):
        sel = lambda xs: xs[c * n_pairs:(c + 1) * n_pairs]
        Pb = bf(P)
        U = each(lambda wu, pb: _dot(wu[:, :LANES].astype(BF), pb) + wu[:, LANES:], sel(WU), Pb)
        U_st = each(bstack, U)
        y = each(lambda r_, mb, mk, pb, u, vs: _dot(jnp.concatenate([r_, mb, mk], axis=1),
                                                    jnp.concatenate([pb, u, vs], axis=0)),
                 sel(R_c), sel(Mrb), sel(Mrk), Pb, U_st, sel(V_st))
        P = each(lambda gc, p_, m, u, vs: gc * p_ + _dot(m, jnp.concatenate([u, vs], axis=0)),
                 sel(gcol), P, sel(BKpT), U_st, sel(V_st))
        out = each(finish, y, sel(r), sel(k), sel(v), sel(rows), sel(cols))
        for o, rs, cs in zip(out, sel(rows), sel(cols)):
            y_ref[rs, cs] = o
    for p in range(n_pairs):
        p_scr[p] = P[p]


def _wkv(r, k, v, lw, kkr, a, g, r_k, lnx_w, lnx_b, *, batch, seq, chunks_per_step=4):
    M, W = r.shape
    rows = chunks_per_step * WKV_CHUNK
    steps = seq // rows
    tok = pl.BlockSpec((rows, W), lambda b, c: (b * steps + c, 0))
    par = pl.BlockSpec((1, W), lambda b, c: (0, 0))
    return pl.pallas_call(
        _wkv_kernel,
        grid=(batch, steps),
        in_specs=[tok] * 7 + [par] * 3,
        out_specs=tok,
        out_shape=jax.ShapeDtypeStruct((M, W), BF),
        scratch_shapes=[pltpu.VMEM((W // LANES, LANES, LANES), F32)],
        compiler_params=_cparams("parallel", "arbitrary"),
        name="wkv",
    )(r, k, v, lw, kkr, a, g, r_k, lnx_w, lnx_b)


def _attn_kernel(slopes_ref, q_ref, k_ref, v_ref, o_ref, og_scr, lse_scr, *, seq):
    slot = pl.program_id(1)
    grp = pl.program_id(2)
    blk = ATTN_BLK
    scale = ATTN_HEAD_DIM ** -0.5
    qi = lax.broadcasted_iota(jnp.int32, (blk, 2 * blk), 0)
    kj = lax.broadcasted_iota(jnp.int32, (blk, 2 * blk), 1)
    rel = (qi + blk - kj).astype(F32)
    in_band = (kj >= qi) & (kj <= qi + blk)
    is_prev = kj < blk
    nb = ATTN_BLOCKS_PER_ITER
    each = lambda f, *xs: [f(*t) for t in zip(*xs)]

    def rows(start, d):
        if d == 1:
            return pl.ds(start, blk)
        return pl.ds(start, blk, stride=d)

    for gi, (window, d) in enumerate(ATTN_GROUPS):
        assert window // d == blk and seq % window == 0 and (seq // blk) % nb == 0

        @pl.when(grp == gi)
        def _(gi=gi, d=d):
            slope = slopes_ref[gi * ATTN_HEADS_PER_GROUP + slot]
            bias = jnp.where(in_band, -slope * float(d) * rel, NEG_BIG)
            span = blk * d

            def body(it, carry):
                ids = [it * nb + j for j in range(nb)]
                ns = [i // d for i in ids]
                cur = [rows(n * span + i % d, d) for n, i in zip(ns, ids)]
                prv = [rows(jnp.maximum(n - 1, 0) * span + i % d, d) for n, i in zip(ns, ids)]
                q = [q_ref[c, :].astype(BF) for c in cur]
                kcat = [jnp.concatenate([k_ref[p_, :], k_ref[c, :]], axis=0).astype(BF) for p_, c in zip(prv, cur)]
                vcat = [jnp.concatenate([v_ref[p_, :], v_ref[c, :]], axis=0).astype(BF) for p_, c in zip(prv, cur)]
                s = each(lambda q_, k_: _dot_nt(q_, k_) * scale + bias, q, kcat)
                s = each(lambda s_, n: jnp.where(is_prev & (n == 0), NEG_BIG, s_), s, ns)
                m = [jnp.max(s_, axis=1, keepdims=True) for s_ in s]
                p = each(lambda s_, m_: jnp.exp(s_ - m_), s, m)
                l = [jnp.sum(p_, axis=1, keepdims=True) for p_ in p]
                o = each(lambda p_, v_, l_: _dot(p_.astype(BF), v_) / l_, p, vcat, l)
                lse = each(lambda m_, l_: m_ + jnp.log(l_), m, l)
                for c, o_, e_ in zip(cur, o, lse):
                    og_scr[gi, c, :] = o_
                    lse_scr[gi, c, :] = jnp.broadcast_to(e_, (blk, LANES))
                return carry

            lax.fori_loop(0, seq // blk // nb, body, 0)

    @pl.when(grp == len(ATTN_GROUPS) - 1)
    def _():
        tile = 8 * blk

        def merge(t, carry):
            rs = pl.ds(pl.multiple_of(t * tile, tile), tile)
            lses = [lse_scr[g, rs, :] for g in range(len(ATTN_GROUPS))]
            top = functools.reduce(jnp.maximum, lses)
            w = [jnp.exp(e - top) for e in lses]
            num = sum(w_ * og_scr[g, rs, :] for g, w_ in enumerate(w))
            o_ref[rs, :] = (num / sum(w)).astype(o_ref.dtype)
            return carry

        lax.fori_loop(0, seq // tile, merge, 0)


def _attention(qkv, slopes, *, batch, seq):
    M = qkv.shape[0]
    E = ATTN_HEAD_DIM
    hpg = ATTN_HEADS_PER_GROUP

    def spec(which):
        first = COL_QKV // E + which * ATTN_HEADS
        return pl.BlockSpec((seq, E), lambda b, s, g: (b, first + g * hpg + s))

    return pl.pallas_call(
        functools.partial(_attn_kernel, seq=seq),
        grid=(batch, hpg, len(ATTN_GROUPS)),
        in_specs=[pl.BlockSpec(memory_space=pltpu.SMEM), spec(0), spec(1), spec(2)],
        out_specs=pl.BlockSpec((seq, E), lambda b, s, g: (b, s)),
        out_shape=jax.ShapeDtypeStruct((M, hpg * E), BF),
        scratch_shapes=[pltpu.VMEM((len(ATTN_GROUPS), seq, LANES), F32)] * 2,
        compiler_params=_cparams("parallel", "parallel", "arbitrary"),
        name="dilated_attn",
    )(slopes, qkv, qkv, qkv)


def _sgu_kernel(x_ref, lg_ref, lb_ref, ws_ref, bt_ref, o_ref):
    tm = x_ref.shape[0]
    CH = SGU_CHUNK
    GD = SGU_WIDTH // SGU_GROUPS
    u = _gelu_tanh(x_ref[:, :SGU_WIDTH])
    v = _gelu_tanh(x_ref[:, SGU_WIDTH:])
    mu = jnp.mean(v, axis=-1, keepdims=True)
    vc = v - mu
    var = jnp.mean(vc * vc, axis=-1, keepdims=True)
    vn = (vc * lax.rsqrt(var + LN_EPS) * lg_ref[...] + lb_ref[...]).astype(BF)
    ti = lax.broadcasted_iota(jnp.int32, (CH, CH), 0)
    si = lax.broadcasted_iota(jnp.int32, (CH, CH), 1)
    causal = ti >= si
    for gidx in range(SGU_GROUPS):
        w = jnp.where(causal, ws_ref[gidx], 0.0).astype(BF)
        bias = bt_ref[:, gidx:gidx + 1]
        cs = slice(gidx * GD, (gidx + 1) * GD)
        for c in range(tm // CH):
            rs = slice(c * CH, (c + 1) * CH)
            mixed = _dot(w, vn[rs, cs]) + bias
            o_ref[rs, cs] = (u[rs, cs] * mixed).astype(o_ref.dtype)


def _sgu(proj, ln_g, ln_b, w_s, b_t, *, tm=512):
    M = proj.shape[0]
    return pl.pallas_call(
        _sgu_kernel,
        grid=(M // tm,),
        in_specs=[
            pl.BlockSpec((tm, N_SGU), lambda i: (i, COL_SGU // N_SGU)),
            pl.BlockSpec((1, SGU_WIDTH), lambda i: (0, 0)),
            pl.BlockSpec((1, SGU_WIDTH), lambda i: (0, 0)),
            pl.BlockSpec(w_s.shape, lambda i: (0, 0, 0)),
            pl.BlockSpec(b_t.shape, lambda i: (0, 0)),
        ],
        out_specs=pl.BlockSpec((tm, SGU_WIDTH), lambda i: (i, 0)),
        out_shape=jax.ShapeDtypeStruct((M, SGU_WIDTH), BF),
        compiler_params=_cparams("parallel"),
        name="sgu",
    )(proj, ln_g, ln_b, w_s, b_t)


def _merge_kernel(h_ref, yr_ref, ya_ref, ys_ref, g0_ref, g1_ref, g2_ref, wr_ref, wa_ref, ws_ref,
                  wo_ref, pg_ref, o_ref, acc_scr):
    j = pl.program_id(1)

    @pl.when(j == 0)
    def _():
        acc_scr[...] = jnp.zeros_like(acc_scr)

    merged = (g0_ref[...] * _dot(yr_ref[...], wr_ref[...])
              + g1_ref[...] * _dot(ya_ref[...], wa_ref[...])
              + g2_ref[...] * _dot(ys_ref[...], ws_ref[...]))
    acc_scr[...] += _dot(merged.astype(BF), wo_ref[...])

    @pl.when(j == pl.num_programs(1) - 1)
    def _():
        o_ref[...] = h_ref[...] + _rms(acc_scr[...], pg_ref[...])


def _merge(h, yr, ya, ys, gates, wr, wa, ws, wo, post_g, *, tm=512, tn=1024):
    M, D = h.shape
    nj = D // tn
    act = lambda a: pl.BlockSpec((tm, a.shape[1]), lambda i, j: (i, 0))
    gate = lambda br: pl.BlockSpec((tm, tn), lambda i, j: (i, br * nj + j))
    wcol = lambda a: pl.BlockSpec((a.shape[0], tn), lambda i, j: (0, j))
    return pl.pallas_call(
        _merge_kernel,
        grid=(M // tm, nj),
        in_specs=[
            pl.BlockSpec((tm, D), lambda i, j: (i, 0)),
            act(yr), act(ya), act(ys), gate(0), gate(1), gate(2),
            wcol(wr), wcol(wa), wcol(ws),
            pl.BlockSpec((tn, D), lambda i, j: (j, 0)),
            pl.BlockSpec((1, D), lambda i, j: (0, 0)),
        ],
        out_specs=pl.BlockSpec((tm, D), lambda i, j: (i, 0)),
        out_shape=jax.ShapeDtypeStruct((M, D), F32),
        scratch_shapes=[pltpu.VMEM((tm, D), F32)],
        compiler_params=_cparams("parallel", "arbitrary"),
        name="merge_out",
    )(h, yr, ya, ys, gates, gates, gates, wr, wa, ws, wo, post_g)


def _alibi_slopes(n_heads):
    def geometric(n):
        start = 2.0 ** (-8.0 / n)
        return [start ** (i + 1) for i in range(n)]
    closest = 2 ** int(math.floor(math.log2(n_heads)))
    slopes = geometric(closest)
    if closest < n_heads:
        slopes += geometric(2 * closest)[0::2][: n_heads - closest]
    return np.array(sorted(slopes, reverse=True), dtype=np.float32)


def _pad_rows(x, before, total):
    return jnp.pad(x, ((before, total - before - x.shape[0]), (0, 0)))


def _w_in_job(w_in_t, extra_t, l, *, rows=128):
    n_in = w_in_t.shape[1]
    assert n_in - N_SHIFT == N_PROJ - COL_QKV and (COL_QKV - N_SHIFT) % SUBLANES == 0
    assert COL_QKV % rows == 0 and N_PROJ % rows == 0 and SEG_P % rows + LANES <= rows
    n_head = COL_QKV // rows

    def offset(s):
        return s * (rows // SUBLANES) - jnp.where(s < n_head, 0, (COL_QKV - N_SHIFT) // SUBLANES)

    return _CastJob(w_in_t, l, rows, N_PROJ // rows, offset, extra_t, SEG_P // rows, SEG_P % rows)


def kernel(x, ffn1_pre_g, ffn1_w_gu, ffn1_w_down, ffn1_post_g, mix_pre_g, w_in, shift_mu, decay_w0, decay_w2, iclr_a0, iclr_a2, gate_g2, k_k, k_a, r_k, lnx_w, lnx_b, vres_w1, vres_mu, vres_v0, vres_w2, sgu_ln_g, sgu_ln_b, sgu_w_s, sgu_b, w_b_rwkv, w_b_attn, w_b_sgu, w_out, mix_post_g, ffn2_pre_g, ffn2_w_gu, ffn2_w_down, ffn2_post_g):
    B, T, D = x.shape
    depth = w_in.shape[0]
    slopes = jnp.asarray(_alibi_slopes(ATTN_HEADS))
    row = lambda p: p.reshape(1, -1)
    ffn_jobs = lambda w_gu, w_down, l, fine=False: [_plain_job(w_gu, l, 16 if fine else 32),
                                                    _plain_job(w_down, l, 64 if fine else 128)]

    w_in_t = jnp.swapaxes(w_in, 1, 2)
    h = x.reshape(B * T, D)
    v_first = None
    wgu = _cast(_plain_job(ffn1_w_gu, 0, 128))
    wd = _cast(_plain_job(ffn1_w_down, 0, ffn1_w_down.shape[1] // 8))
    for l in range(depth):
        if l == 0:
            extra_t = jnp.zeros((LANES, D), F32)
            extra_mu = jnp.zeros((LANES,), F32)
        else:
            extra_t = _pad_rows(vres_w1[l - 1].T, 0, LANES)
            extra_mu = jnp.pad(vres_mu[l - 1], (0, LANES - VRES_LORA))
        jobs = [_w_in_job(w_in_t, extra_t, l)] + [_plain_job(w, l, 64)
                                                    for w in (w_b_rwkv, w_b_attn, w_b_sgu, w_out)]
        h, w_proj, wr, wa, ws, wo = _ffn(h, row(ffn1_pre_g[l]), wgu, wd, row(ffn1_post_g[l]), jobs)

        mu = jnp.concatenate([jnp.pad(shift_mu[l], (0, SEG_P - N_SHIFT)), extra_mu]).reshape(1, -1)
        proj, gates, wgu, wd = _norm_matmul(h, row(mix_pre_g[l]), w_proj,
                                            ffn_jobs(ffn2_w_gu, ffn2_w_down, l))

        vres = None
        if l > 0:
            vres = (v_first, row(vres_v0[l - 1]), _pad_rows(vres_w2[l - 1], 0, LANES).astype(BF))
        r, k, v, lw, kkr, a, g = _rwkv_prep(
            proj, mu, row(decay_w0[l]), _pad_rows(decay_w2[l], 0, LANES).astype(BF),
            row(iclr_a0[l]), _pad_rows(iclr_a2[l], DECAY_LORA, LANES).astype(BF),
            _pad_rows(gate_g2[l], 0, 2 * LANES).astype(BF), row(k_k[l]), row(k_a[l]), vres, seq=T)
        if l == 0:
            v_first = v
        y_rwkv = _wkv(r, k, v, lw, kkr, a, g, row(r_k[l]), row(lnx_w[l]), row(lnx_b[l]),
                      batch=B, seq=T)

        y_attn = _attention(proj, slopes, batch=B, seq=T)
        y_sgu = _sgu(proj, row(sgu_ln_g[l]), row(sgu_ln_b[l]), sgu_w_s[l], sgu_b[l].T)

        h = _merge(h, y_rwkv, y_attn, y_sgu, gates, wr, wa, ws, wo, row(mix_post_g[l]))

        jobs = ffn_jobs(ffn1_w_gu, ffn1_w_down, l + 1, fine=True) if l + 1 < depth else []
        h, *cast = _ffn(h, row(ffn2_pre_g[l]), wgu, wd, row(ffn2_post_g[l]), jobs)
        if cast:
            wgu, wd = cast
    return h.reshape(B, T, D)
```

```python
import functools
import math
from typing import Callable, NamedTuple, Optional

import numpy as np
import jax
import jax.numpy as jnp
from jax import lax
from jax.experimental import pallas as pl
from jax.experimental.pallas import tpu as pltpu

BF = jnp.bfloat16
F32 = jnp.float32

D_MODEL = 2048
RWKV_HEAD_DIM = 64
RWKV_WIDTH = D_MODEL // 2
DECAY_LORA = 64
ICLR_LORA = 64
VRES_LORA = 32
GATE_LORA = 160
LNX_EPS = 64e-5
ATTN_GROUPS = ((128, 1), (512, 4), (2048, 16))
ATTN_HEADS_PER_GROUP = 4
ATTN_HEAD_DIM = 128
ATTN_HEADS = len(ATTN_GROUPS) * ATTN_HEADS_PER_GROUP
ATTN_BLK = 128
ATTN_BLOCKS_PER_ITER = 8
SGU_CHUNK = 128
SGU_GROUPS = 4
SGU_WIDTH = D_MODEL // 4
RMS_EPS = 1e-6
LN_EPS = 1e-5
N_BRANCHES = 3

LANES = 128
SUBLANES = 8
WKV_CHUNK = 64
FFN_TF = 512
NEG_BIG = -1e30

N_SHIFT = 3 * RWKV_WIDTH + DECAY_LORA + ICLR_LORA + GATE_LORA
SEG_R, SEG_K, SEG_V = 0, RWKV_WIDTH, 2 * RWKV_WIDTH
SEG_WA = 3 * RWKV_WIDTH
SEG_G = SEG_WA + 128
SEG_P = SEG_G + 256
N_QKV = 3 * ATTN_HEADS * ATTN_HEAD_DIM
N_SGU = 2 * SGU_WIDTH
N_GATE = N_BRANCHES * D_MODEL
COL_SHIFT = 0
COL_QKV = SEG_P + LANES
COL_SGU = COL_QKV + N_QKV
COL_GATE = COL_SGU + N_SGU
N_PROJ = COL_GATE + N_GATE

VMEM_LIMIT = 60 * 1024 * 1024


def _cparams(*sem):
    return pltpu.CompilerParams(dimension_semantics=sem, vmem_limit_bytes=VMEM_LIMIT)


def _dot(a, b):
    return jnp.dot(a, b, preferred_element_type=F32)


def _dot_nt(a, b):
    return lax.dot_general(a, b, (((1,), (1,)), ((), ())), preferred_element_type=F32)


def _rms(x, g):
    return x * lax.rsqrt(jnp.mean(x * x, axis=-1, keepdims=True) + RMS_EPS) * g


def _sigmoid(x):
    return 0.5 * jnp.tanh(0.5 * x) + 0.5


class _CastJob(NamedTuple):
    src: jax.Array
    layer: int
    rows: int
    n_blocks: int
    offset: Callable
    patch: Optional[jax.Array] = None
    patch_block: int = 0
    patch_row: int = 0


class _CastMeta(NamedTuple):
    n_blocks: int
    has_patch: bool
    patch_block: int
    patch_row: int


def _plain_job(src, layer, rows):
    R = src.shape[1]
    assert R % rows == 0 and rows % (2 * SUBLANES) == 0
    return _CastJob(src, layer, rows, R // rows, lambda s: s * (rows // SUBLANES))


def _job_plumbing(jobs, step_of, n_steps):
    in_specs, args, out_specs, out_shapes, metas = [], [], [], [], []
    for job in jobs:
        assert job.n_blocks <= n_steps, "carrier kernel has too few grid steps for this cast job"
        C = job.src.shape[2]
        blk = lambda *g, job=job: jnp.minimum(step_of(*g), job.n_blocks - 1)
        in_specs.append(pl.BlockSpec(
            (None, pl.Element(job.rows), pl.Element(C)),
            lambda *g, job=job, blk=blk: (job.layer, job.offset(blk(*g)) * SUBLANES, 0)))
        args.append(job.src)
        if job.patch is not None:
            in_specs.append(pl.BlockSpec(job.patch.shape, lambda *g: (0, 0)))
            args.append(job.patch)
        out_specs.append(pl.BlockSpec((job.rows, C), lambda *g, blk=blk: (blk(*g), 0)))
        out_shapes.append(jax.ShapeDtypeStruct((job.n_blocks * job.rows, C), BF))
        metas.append(_CastMeta(job.n_blocks, job.patch is not None, job.patch_block, job.patch_row))
    return in_specs, args, out_specs, out_shapes, tuple(metas)


def _run_cast_jobs(step, metas, in_refs, out_refs):
    in_refs = iter(in_refs)
    for meta, dst in zip(metas, out_refs):
        src = next(in_refs)
        patch = next(in_refs) if meta.has_patch else None

        @pl.when(step < meta.n_blocks)
        def _(meta=meta, src=src, patch=patch, dst=dst):
            dst[...] = src[...].astype(BF)
            if patch is not None:
                @pl.when(step == meta.patch_block)
                def _():
                    dst[pl.ds(meta.patch_row, LANES), :] = patch[...].astype(BF)


def _n_job_inputs(metas):
    return sum(2 if m.has_patch else 1 for m in metas)


def _cast_kernel(*refs, metas):
    n_in = _n_job_inputs(metas)
    _run_cast_jobs(pl.program_id(0), metas, refs[:n_in], refs[n_in:])


def _cast(job):
    in_specs, args, out_specs, out_shapes, metas = _job_plumbing([job], lambda s: s, job.n_blocks)
    return pl.pallas_call(
        functools.partial(_cast_kernel, metas=metas),
        grid=(job.n_blocks,),
        in_specs=in_specs,
        out_specs=out_specs,
        out_shape=out_shapes,
        compiler_params=_cparams("arbitrary"),
        name="cast",
    )(*args)[0]


def _ffn_kernel(*refs, tail, metas):
    n_in = _n_job_inputs(metas)
    h_ref, pre_g_ref, wg_ref, wu_ref, wd_ref, post_g_ref = refs[:6]
    job_in = refs[6:6 + n_in]
    o_ref = refs[6 + n_in]
    job_out = refs[7 + n_in:7 + n_in + len(metas)]
    xn_scr, acc_scr = refs[7 + n_in + len(metas):]
    i, f = pl.program_id(0), pl.program_id(1)
    last = pl.num_programs(1) - 1
    tf = wd_ref.shape[0]

    @pl.when(f == 0)
    def _():
        xn_scr[...] = _rms(h_ref[...], pre_g_ref[...]).astype(BF)
        acc_scr[...] = jnp.zeros_like(acc_scr)

    def accumulate(lo):
        xn = xn_scr[...]
        g = _dot(xn, wg_ref[:, lo:])
        u = _dot(xn, wu_ref[:, lo:])
        a = (g * _sigmoid(g) * u).astype(BF)
        acc_scr[...] += _dot(a, wd_ref[lo:, :])

    @pl.when(f < last)
    def _():
        accumulate(0)

    @pl.when(f == last)
    def _():
        accumulate(tf - tail)
        o_ref[...] = h_ref[...] + 0.5 * _rms(acc_scr[...], post_g_ref[...])

    _run_cast_jobs(i * pl.num_programs(1) + f, metas, job_in, job_out)


def _ffn(h, pre_g, wgu, wd, post_g, jobs=(), *, tm=512, tf=FFN_TF):
    M, D = h.shape
    F = wd.shape[0]
    nf = -(-F // tf)
    tail = F - (nf - 1) * tf
    assert wgu.shape == (D, 2 * F) and F % LANES == 0 and tf % LANES == 0 and tail % LANES == 0
    start = lambda f, base=0: (base // LANES + jnp.minimum(f * (tf // LANES), (F - tf) // LANES)) * LANES
    j_in, j_args, j_out, j_shapes, metas = _job_plumbing(jobs, lambda i, f: i * nf + f, M // tm * nf)
    return pl.pallas_call(
        functools.partial(_ffn_kernel, tail=tail, metas=metas),
        grid=(M // tm, nf),
        in_specs=[
            pl.BlockSpec((tm, D), lambda i, f: (i, 0)),
            pl.BlockSpec((1, D), lambda i, f: (0, 0)),
            pl.BlockSpec((pl.Element(D), pl.Element(tf)), lambda i, f: (0, start(f))),
            pl.BlockSpec((pl.Element(D), pl.Element(tf)), lambda i, f: (0, start(f, F))),
            pl.BlockSpec((pl.Element(tf), pl.Element(D)), lambda i, f: (start(f), 0)),
            pl.BlockSpec((1, D), lambda i, f: (0, 0)),
        ] + j_in,
        out_specs=[pl.BlockSpec((tm, D), lambda i, f: (i, 0))] + j_out,
        out_shape=[jax.ShapeDtypeStruct((M, D), F32)] + j_shapes,
        scratch_shapes=[pltpu.VMEM((tm, D), BF), pltpu.VMEM((tm, D), F32)],
        compiler_params=_cparams("arbitrary", "arbitrary"),
        name="ffn",
    )(h, pre_g, wgu, wgu, wd, post_g, *j_args)


def _gelu_tanh(x):
    c = math.sqrt(2.0 / math.pi)
    return x * (0.5 * (1.0 + jnp.tanh(c * (x + 0.044715 * (x * x * x)))))


def _nmm_kernel(*refs, n_plain, metas):
    n_in = _n_job_inputs(metas)
    h_ref, g_ref, w_ref = refs[:3]
    job_in = refs[3:3 + n_in]
    o_ref, gate_ref = refs[3 + n_in:5 + n_in]
    job_out = refs[5 + n_in:5 + n_in + len(metas)]
    xn_scr = refs[-1]
    i, j = pl.program_id(0), pl.program_id(1)

    @pl.when(j == 0)
    def _():
        xn_scr[...] = _rms(h_ref[...], g_ref[...]).astype(BF)

    @pl.when(j < n_plain)
    def _():
        o_ref[...] = _dot_nt(xn_scr[...], w_ref[...])

    @pl.when(j >= n_plain)
    def _():
        gate_ref[...] = _sigmoid(_dot_nt(xn_scr[...], w_ref[...])).astype(gate_ref.dtype)

    _run_cast_jobs(i * pl.num_programs(1) + j, metas, job_in, job_out)


def _norm_matmul(h, g, w_t, jobs=(), *, tm=1024, tn=1024):
    M, D = h.shape
    N = w_t.shape[0]
    nj = N // tn
    n_plain = COL_GATE // tn
    assert N == N_PROJ and nj * tn == N and n_plain * tn == COL_GATE
    j_in, j_args, j_out, j_shapes, metas = _job_plumbing(jobs, lambda i, j: i * nj + j, M // tm * nj)
    return pl.pallas_call(
        functools.partial(_nmm_kernel, n_plain=n_plain, metas=metas),
        grid=(M // tm, nj),
        in_specs=[
            pl.BlockSpec((tm, D), lambda i, j: (i, 0)),
            pl.BlockSpec((1, D), lambda i, j: (0, 0)),
            pl.BlockSpec((tn, D), lambda i, j: (j, 0)),
        ] + j_in,
        out_specs=[pl.BlockSpec((tm, tn), lambda i, j: (i, jnp.minimum(j, n_plain - 1))),
                   pl.BlockSpec((tm, tn), lambda i, j: (i, jnp.maximum(j - n_plain, 0)))] + j_out,
        out_shape=[jax.ShapeDtypeStruct((M, COL_GATE), F32),
                   jax.ShapeDtypeStruct((M, N - COL_GATE), BF)] + j_shapes,
        scratch_shapes=[pltpu.VMEM((tm, D), BF)],
        compiler_params=_cparams("arbitrary", "arbitrary"),
        name="in_proj",
    )(h, g, w_t, *j_args)


def _prep_kernel(*refs, seq, has_vres):
    if has_vres:
        (ps_ref, prev_ref, mu_ref, w0_ref, w2_ref, a0_ref, a2_ref, g2_ref, kk_ref, ka_ref,
         vf_ref, v0_ref, vw2_ref,
         r_o, k_o, v_o, lw_o, kk_o, a_o, g_o) = refs
    else:
        (ps_ref, prev_ref, mu_ref, w0_ref, w2_ref, a0_ref, a2_ref, g2_ref, kk_ref, ka_ref,
         r_o, k_o, v_o, lw_o, kk_o, a_o, g_o) = refs
    tm = ps_ref.shape[0]
    W = RWKV_WIDTH
    x = ps_ref[...]
    at_seq_start = (pl.program_id(0) * tm) % seq == 0
    last_prev = jnp.where(at_seq_start, 0.0, prev_ref[7:8, :])
    row = lax.broadcasted_iota(jnp.int32, x.shape, 0)
    prev = jnp.where(row == 0, last_prev, pltpu.roll(x, 1, axis=0))
    xs = x + (prev - x) * mu_ref[...]

    xr = xs[:, SEG_R:SEG_R + W]
    xk = xs[:, SEG_K:SEG_K + W]
    xv = xs[:, SEG_V:SEG_V + W]
    xwa = xs[:, SEG_WA:SEG_WA + 128]
    xg = xs[:, SEG_G:SEG_G + 256]

    z = w0_ref[...] + _dot(jnp.tanh(xwa).astype(BF), w2_ref[...])
    w_log = -jax.nn.softplus(-z) - 0.5
    lw_o[...] = -jnp.exp(w_log)
    a = jax.nn.sigmoid(a0_ref[...] + _dot(xwa.astype(BF), a2_ref[...]))
    a_o[...] = a
    g_o[...] = _dot(jax.nn.sigmoid(xg).astype(BF), g2_ref[...]).astype(g_o.dtype)
    kk_o[...] = xk * kk_ref[...]
    k_o[...] = xk * (1.0 + (a - 1.0) * ka_ref[...])
    r_o[...] = xr
    if has_vres:
        pv = xs[:, SEG_P:SEG_P + 128]
        mix = jax.nn.sigmoid(v0_ref[...] + _dot(pv.astype(BF), vw2_ref[...]))
        v_o[...] = xv + (vf_ref[...] - xv) * mix
    else:
        v_o[...] = xv


def _rwkv_prep(ps, mu, w0, w2, a0, a2, g2, k_k, k_a, vres, *, seq, tm=256):
    M = ps.shape[0]
    NA = COL_QKV
    W = RWKV_WIDTH
    row = lambda n: pl.BlockSpec((1, n), lambda i: (0, 0))
    full = lambda a: pl.BlockSpec(a.shape, lambda i: (0, 0))
    tok = pl.BlockSpec((tm, W), lambda i: (i, 0))
    in_specs = [
        pl.BlockSpec((tm, NA), lambda i: (i, 0)),
        pl.BlockSpec((8, NA), lambda i: (jnp.maximum(i * (tm // 8) - 1, 0), 0)),
        row(NA), row(W), full(w2), row(W), full(a2), full(g2), row(W), row(W),
    ]
    args = [ps, ps, mu, w0, w2, a0, a2, g2, k_k, k_a]
    if vres is not None:
        v_first, v0, vw2 = vres
        in_specs += [tok, row(W), full(vw2)]
        args += [v_first, v0, vw2]
    return pl.pallas_call(
        functools.partial(_prep_kernel, seq=seq, has_vres=vres is not None),
        grid=(M // tm,),
        in_specs=in_specs,
        out_specs=[tok] * 7,
        out_shape=[jax.ShapeDtypeStruct((M, W), F32)] * 6 + [jax.ShapeDtypeStruct((M, W), BF)],
        compiler_params=_cparams("parallel"),
        name="rwkv_prep",
    )(*args)


def _wkv_kernel(r_ref, k_ref, v_ref, lw_ref, kk_ref, a_ref, g_ref, rk_ref, lnw_ref, lnb_ref,
                y_ref, p_scr):
    C = WKV_CHUNK
    C2 = 2 * C

    @pl.when(pl.program_id(1) == 0)
    def _():
        p_scr[...] = jnp.zeros_like(p_scr)

    lane = lax.broadcasted_iota(jnp.int32, (C, LANES), 1)
    lo = lane < RWKV_HEAD_DIM
    ti = lax.broadcasted_iota(jnp.int32, (C, C), 0)
    tj = lax.broadcasted_iota(jnp.int32, (C, C), 1)
    tri = jnp.where(ti >= tj, 1.0, 0.0).astype(BF)
    tq = lax.broadcasted_iota(jnp.int32, (C, LANES), 0)
    sq = lane % C
    strict = tq > sq
    incl = tq >= sq
    eye_c = jnp.where(tq == sq, 1.0, 0.0)
    si = lax.broadcasted_iota(jnp.int32, (C2, LANES), 0)
    sj = lax.broadcasted_iota(jnp.int32, (C2, LANES), 1)
    eye = si == sj

    def seg_sum(x):
        s_lo = jnp.sum(jnp.where(lo, x, 0.0), axis=1, keepdims=True)
        s_hi = jnp.sum(jnp.where(lo, 0.0, x), axis=1, keepdims=True)
        return jnp.where(lo, s_lo, s_hi)

    def stack(x):
        return jnp.concatenate([jnp.where(lo, x, 0.0), jnp.where(lo, 0.0, x)], axis=0)

    def bstack(x):
        return stack(x).astype(BF)

    n_pairs = r_ref.shape[1] // LANES
    n_chunks = r_ref.shape[0] // C
    items = [(c, p) for c in range(n_chunks) for p in range(n_pairs)]
    rows = [slice(c * C, (c + 1) * C) for c, _ in items]
    cols = [slice(p * LANES, (p + 1) * LANES) for _, p in items]
    each = lambda f, *xs: [f(*t) for t in zip(*xs)]
    bf = lambda xs: [x.astype(BF) for x in xs]
    load = lambda ref: [ref[rs, cs] for rs, cs in zip(rows, cols)]

    r, k, v, lw, kkr, a = (load(ref) for ref in (r_ref, k_ref, v_ref, lw_ref, kk_ref, a_ref))

    def cumsum(x):
        hi = x.astype(BF)
        both = _dot(tri, jnp.concatenate([hi, (x - hi.astype(F32)).astype(BF)], axis=1))
        return both[:, :LANES] + both[:, LANES:]

    cum = each(cumsum, lw)
    tot = [c[C - 1:C, :] for c in cum]
    kk = each(lambda x: x / jnp.maximum(jnp.sqrt(seg_sum(x * x)), 1e-12), kkr)
    kka = each(lambda x, y: x * y, kk, a)
    A_c = each(lambda x, c, w: -x * jnp.exp(c - w), kk, cum, lw)
    R_c = bf(each(lambda x, c: x * jnp.exp(c), r, cum))
    AR = each(lambda a_, r_: jnp.concatenate([a_.astype(BF), r_], axis=0), A_c, R_c)
    BK_st = each(lambda x, y, c: jnp.concatenate([bstack(x * jnp.exp(-c)), bstack(y * jnp.exp(-c))], axis=0),
                 kka, k, cum)
    V_st = each(bstack, v)
    BKpT = bf(each(lambda x, y, c, t: jnp.concatenate([stack(x * jnp.exp(t - c)).T,
                                                       stack(y * jnp.exp(t - c)).T], axis=1),
                   kka, k, cum, tot))
    gcol = [jnp.exp(jnp.sum(jnp.where(eye, jnp.broadcast_to(t, (C2, LANES)), 0.0), axis=1, keepdims=True))
            for t in tot]

    G = each(_dot_nt, AR, BK_st)
    Aab = [jnp.where(strict, x[:C, :LANES], 0.0) for x in G]
    Aak = bf([jnp.where(strict, x[:C, LANES:], 0.0) for x in G])
    Mrb = bf([jnp.where(incl, x[C:, :LANES], 0.0) for x in G])
    Mrk = bf([jnp.where(incl, x[C:, LANES:], 0.0) for x in G])
    AkV = each(_dot, Aak, V_st)

    S = [eye_c + x for x in Aab]
    Pw = each(lambda x: _dot(x.astype(BF), bstack(x)), Aab)
    n = 2
    while 2 * n < C:
        prod = each(lambda p_, s_: _dot(p_.astype(BF), jnp.concatenate([bstack(p_), bstack(s_)], axis=1)),
                    Pw, S)
        S = each(lambda s_, pr: s_ + pr[:, LANES:], S, prod)
        Pw = [pr[:, :LANES] for pr in prod]
        n *= 2
    S = each(lambda s_, p_: s_ + _dot(p_.astype(BF), bstack(s_)), S, Pw)
    WU = each(lambda s_, a_, u0: _dot(s_.astype(BF), jnp.concatenate([bstack(a_), bstack(u0)], axis=1)),
              S, A_c, AkV)

    def finish(y_, r_, k_, v_, rs, cs):
        mu = seg_sum(y_) * (1.0 / RWKV_HEAD_DIM)
        yc = y_ - mu
        var = seg_sum(yc * yc) * (1.0 / RWKV_HEAD_DIM)
        yn = yc * lax.rsqrt(var + LNX_EPS) * lnw_ref[:, cs] + lnb_ref[:, cs]
        bonus = seg_sum(r_ * k_ * rk_ref[:, cs]) * v_
        return ((yn + bonus) * g_ref[rs, cs]).astype(y_ref.dtype)

    P = [p_scr[p] for p in range(n_pairs)]
    for c in range(n_chunks):
        sel = lambda xs: xs[c * n_pairs:(c + 1) * n_pairs]
        Pb = bf(P)
        U = each(lambda wu, pb: _dot(wu[:, :LANES].astype(BF), pb) + wu[:, LANES:], sel(WU), Pb)
        U_st = each(bstack, U)
        y = each(lambda r_, mb, mk, pb, u, vs: _dot(jnp.concatenate([r_, mb, mk], axis=1),
                                                    jnp.concatenate([pb, u, vs], axis=0)),
                 sel(R_c), sel(Mrb), sel(Mrk), Pb, U_st, sel(V_st))
        P = each(lambda gc, p_, m, u, vs: gc * p_ + _dot(m, jnp.concatenate([u, vs], axis=0)),
                 sel(gcol), P, sel(BKpT), U_st, sel(V_st))
        out = each(finish, y, sel(r), sel(k), sel(v), sel(rows), sel(cols))
        for o, rs, cs in zip(out, sel(rows), sel(cols)):
            y_ref[rs, cs] = o
    for p in range(n_pairs):
        p_scr[p] = P[p]


def _wkv(r, k, v, lw, kkr, a, g, r_k, lnx_w, lnx_b, *, batch, seq, chunks_per_step=4):
    M, W = r.shape
    rows = chunks_per_step * WKV_CHUNK
    steps = seq // rows
    tok = pl.BlockSpec((rows, W), lambda b, c: (b * steps + c, 0))
    par = pl.BlockSpec((1, W), lambda b, c: (0, 0))
    return pl.pallas_call(
        _wkv_kernel,
        grid=(batch, steps),
        in_specs=[tok] * 7 + [par] * 3,
        out_specs=tok,
        out_shape=jax.ShapeDtypeStruct((M, W), BF),
        scratch_shapes=[pltpu.VMEM((W // LANES, LANES, LANES), F32)],
        compiler_params=_cparams("parallel", "arbitrary"),
        name="wkv",
    )(r, k, v, lw, kkr, a, g, r_k, lnx_w, lnx_b)


def _attn_kernel(slopes_ref, q_ref, k_ref, v_ref, o_ref, og_scr, lse_scr, *, seq):
    slot = pl.program_id(1)
    grp = pl.program_id(2)
    blk = ATTN_BLK
    scale = ATTN_HEAD_DIM ** -0.5
    qi = lax.broadcasted_iota(jnp.int32, (blk, 2 * blk), 0)
    kj = lax.broadcasted_iota(jnp.int32, (blk, 2 * blk), 1)
    rel = (qi + blk - kj).astype(F32)
    in_band = (kj >= qi) & (kj <= qi + blk)
    is_prev = kj < blk
    nb = ATTN_BLOCKS_PER_ITER
    each = lambda f, *xs: [f(*t) for t in zip(*xs)]

    def rows(start, d):
        if d == 1:
            return pl.ds(start, blk)
        return pl.ds(start, blk, stride=d)

    for gi, (window, d) in enumerate(ATTN_GROUPS):
        assert window // d == blk and seq % window == 0 and (seq // blk) % nb == 0

        @pl.when(grp == gi)
        def _(gi=gi, d=d):
            slope = slopes_ref[gi * ATTN_HEADS_PER_GROUP + slot]
            bias = jnp.where(in_band, -slope * float(d) * rel, NEG_BIG)
            span = blk * d

            def body(it, carry):
                ids = [it * nb + j for j in range(nb)]
                ns = [i // d for i in ids]
                cur = [rows(n * span + i % d, d) for n, i in zip(ns, ids)]
                prv = [rows(jnp.maximum(n - 1, 0) * span + i % d, d) for n, i in zip(ns, ids)]
                q = [q_ref[c, :].astype(BF) for c in cur]
                kcat = [jnp.concatenate([k_ref[p_, :], k_ref[c, :]], axis=0).astype(BF) for p_, c in zip(prv, cur)]
                vcat = [jnp.concatenate([v_ref[p_, :], v_ref[c, :]], axis=0).astype(BF) for p_, c in zip(prv, cur)]
                s = each(lambda q_, k_: _dot_nt(q_, k_) * scale + bias, q, kcat)
                s = each(lambda s_, n: jnp.where(is_prev & (n == 0), NEG_BIG, s_), s, ns)
                m = [jnp.max(s_, axis=1, keepdims=True) for s_ in s]
                p = each(lambda s_, m_: jnp.exp(s_ - m_), s, m)
                l = [jnp.sum(p_, axis=1, keepdims=True) for p_ in p]
                o = each(lambda p_, v_, l_: _dot(p_.astype(BF), v_) / l_, p, vcat, l)
                lse = each(lambda m_, l_: m_ + jnp.log(l_), m, l)
                for c, o_, e_ in zip(cur, o, lse):
                    og_scr[gi, c, :] = o_
                    lse_scr[gi, c, :] = jnp.broadcast_to(e_, (blk, LANES))
                return carry

            lax.fori_loop(0, seq // blk // nb, body, 0)

    @pl.when(grp == len(ATTN_GROUPS) - 1)
    def _():
        tile = 8 * blk

        def merge(t, carry):
            rs = pl.ds(pl.multiple_of(t * tile, tile), tile)
            lses = [lse_scr[g, rs, :] for g in range(len(ATTN_GROUPS))]
            top = functools.reduce(jnp.maximum, lses)
            w = [jnp.exp(e - top) for e in lses]
            num = sum(w_ * og_scr[g, rs, :] for g, w_ in enumerate(w))
            o_ref[rs, :] = (num / sum(w)).astype(o_ref.dtype)
            return carry

        lax.fori_loop(0, seq // tile, merge, 0)


def _attention(qkv, slopes, *, batch, seq):
    M = qkv.shape[0]
    E = ATTN_HEAD_DIM
    hpg = ATTN_HEADS_PER_GROUP

    def spec(which):
        first = COL_QKV // E + which * ATTN_HEADS
        return pl.BlockSpec((seq, E), lambda b, s, g: (b, first + g * hpg + s))

    return pl.pallas_call(
        functools.partial(_attn_kernel, seq=seq),
        grid=(batch, hpg, len(ATTN_GROUPS)),
        in_specs=[pl.BlockSpec(memory_space=pltpu.SMEM), spec(0), spec(1), spec(2)],
        out_specs=pl.BlockSpec((seq, E), lambda b, s, g: (b, s)),
        out_shape=jax.ShapeDtypeStruct((M, hpg * E), BF),
        scratch_shapes=[pltpu.VMEM((len(ATTN_GROUPS), seq, LANES), F32)] * 2,
        compiler_params=_cparams("parallel", "parallel", "arbitrary"),
        name="dilated_attn",
    )(slopes, qkv, qkv, qkv)


def _sgu_kernel(x_ref, lg_ref, lb_ref, ws_ref, bt_ref, o_ref):
    tm = x_ref.shape[0]
    CH = SGU_CHUNK
    GD = SGU_WIDTH // SGU_GROUPS
    u = _gelu_tanh(x_ref[:, :SGU_WIDTH])
    v = _gelu_tanh(x_ref[:, SGU_WIDTH:])
    mu = jnp.mean(v, axis=-1, keepdims=True)
    vc = v - mu
    var = jnp.mean(vc * vc, axis=-1, keepdims=True)
    vn = (vc * lax.rsqrt(var + LN_EPS) * lg_ref[...] + lb_ref[...]).astype(BF)
    ti = lax.broadcasted_iota(jnp.int32, (CH, CH), 0)
    si = lax.broadcasted_iota(jnp.int32, (CH, CH), 1)
    causal = ti >= si
    for gidx in range(SGU_GROUPS):
        w = jnp.where(causal, ws_ref[gidx], 0.0).astype(BF)
        bias = bt_ref[:, gidx:gidx + 1]
        cs = slice(gidx * GD, (gidx + 1) * GD)
        for c in range(tm // CH):
            rs = slice(c * CH, (c + 1) * CH)
            mixed = _dot(w, vn[rs, cs]) + bias
            o_ref[rs, cs] = (u[rs, cs] * mixed).astype(o_ref.dtype)


def _sgu(proj, ln_g, ln_b, w_s, b_t, *, tm=512):
    M = proj.shape[0]
    return pl.pallas_call(
        _sgu_kernel,
        grid=(M // tm,),
        in_specs=[
            pl.BlockSpec((tm, N_SGU), lambda i: (i, COL_SGU // N_SGU)),
            pl.BlockSpec((1, SGU_WIDTH), lambda i: (0, 0)),
            pl.BlockSpec((1, SGU_WIDTH), lambda i: (0, 0)),
            pl.BlockSpec(w_s.shape, lambda i: (0, 0, 0)),
            pl.BlockSpec(b_t.shape, lambda i: (0, 0)),
        ],
        out_specs=pl.BlockSpec((tm, SGU_WIDTH), lambda i: (i, 0)),
        out_shape=jax.ShapeDtypeStruct((M, SGU_WIDTH), BF),
        compiler_params=_cparams("parallel"),
        name="sgu",
    )(proj, ln_g, ln_b, w_s, b_t)


def _merge_kernel(h_ref, yr_ref, ya_ref, ys_ref, g0_ref, g1_ref, g2_ref, wr_ref, wa_ref, ws_ref,
                  wo_ref, pg_ref, o_ref):
    merged = (g0_ref[...] * _dot(yr_ref[...], wr_ref[...])
              + g1_ref[...] * _dot(ya_ref[...], wa_ref[...])
              + g2_ref[...] * _dot(ys_ref[...], ws_ref[...]))
    o_ref[...] = h_ref[...] + _rms(_dot(merged.astype(BF), wo_ref[...]), pg_ref[...])


def _merge(h, yr, ya, ys, gates, wr, wa, ws, wo, post_g, *, tm=512):
    M, D = h.shape
    act = lambda a: pl.BlockSpec((tm, a.shape[1]), lambda i: (i, 0))
    gate = lambda br: pl.BlockSpec((tm, D), lambda i: (i, br))
    resident = lambda a: pl.BlockSpec(a.shape, lambda i: (0, 0), pipeline_mode=pl.Buffered(1))
    return pl.pallas_call(
        _merge_kernel,
        grid=(M // tm,),
        in_specs=[
            pl.BlockSpec((tm, D), lambda i: (i, 0)),
            act(yr), act(ya), act(ys), gate(0), gate(1), gate(2),
            resident(wr), resident(wa), resident(ws), resident(wo),
            pl.BlockSpec((1, D), lambda i: (0, 0)),
        ],
        out_specs=pl.BlockSpec((tm, D), lambda i: (i, 0)),
        out_shape=jax.ShapeDtypeStruct((M, D), F32),
        compiler_params=_cparams("parallel"),
        name="merge_out",
    )(h, yr, ya, ys, gates, gates, gates, wr, wa, ws, wo, post_g)


def _alibi_slopes(n_heads):
    def geometric(n):
        start = 2.0 ** (-8.0 / n)
        return [start ** (i + 1) for i in range(n)]
    closest = 2 ** int(math.floor(math.log2(n_heads)))
    slopes = geometric(closest)
    if closest < n_heads:
        slopes += geometric(2 * closest)[0::2][: n_heads - closest]
    return np.array(sorted(slopes, reverse=True), dtype=np.float32)


def _pad_rows(x, before, total):
    return jnp.pad(x, ((before, total - before - x.shape[0]), (0, 0)))


def _w_in_job(w_in_t, extra_t, l, *, rows=128):
    n_in = w_in_t.shape[1]
    assert n_in - N_SHIFT == N_PROJ - COL_QKV and (COL_QKV - N_SHIFT) % SUBLANES == 0
    assert COL_QKV % rows == 0 and N_PROJ % rows == 0 and SEG_P % rows + LANES <= rows
    n_head = COL_QKV // rows

    def offset(s):
        return s * (rows // SUBLANES) - jnp.where(s < n_head, 0, (COL_QKV - N_SHIFT) // SUBLANES)

    return _CastJob(w_in_t, l, rows, N_PROJ // rows, offset, extra_t, SEG_P // rows, SEG_P % rows)


def kernel(x, ffn1_pre_g, ffn1_w_gu, ffn1_w_down, ffn1_post_g, mix_pre_g, w_in, shift_mu, decay_w0, decay_w2, iclr_a0, iclr_a2, gate_g2, k_k, k_a, r_k, lnx_w, lnx_b, vres_w1, vres_mu, vres_v0, vres_w2, sgu_ln_g, sgu_ln_b, sgu_w_s, sgu_b, w_b_rwkv, w_b_attn, w_b_sgu, w_out, mix_post_g, ffn2_pre_g, ffn2_w_gu, ffn2_w_down, ffn2_post_g):
    B, T, D = x.shape
    depth = w_in.shape[0]
    slopes = jnp.asarray(_alibi_slopes(ATTN_HEADS))
    row = lambda p: p.reshape(1, -1)
    ffn_jobs = lambda w_gu, w_down, l, fine=False: [_plain_job(w_gu, l, 16 if fine else 32),
                                                    _plain_job(w_down, l, 64 if fine else 128)]

    w_in_t = jnp.swapaxes(w_in, 1, 2)
    h = x.reshape(B * T, D)
    v_first = None
    wgu = _cast(_plain_job(ffn1_w_gu, 0, 128))
    wd = _cast(_plain_job(ffn1_w_down, 0, ffn1_w_down.shape[1] // 8))
    for l in range(depth):
        if l == 0:
            extra_t = jnp.zeros((LANES, D), F32)
            extra_mu = jnp.zeros((LANES,), F32)
        else:
            extra_t = _pad_rows(vres_w1[l - 1].T, 0, LANES)
            extra_mu = jnp.pad(vres_mu[l - 1], (0, LANES - VRES_LORA))
        jobs = [_w_in_job(w_in_t, extra_t, l)] + [_plain_job(w, l, 64)
                                                    for w in (w_b_rwkv, w_b_attn, w_b_sgu, w_out)]
        h, w_proj, wr, wa, ws, wo = _ffn(h, row(ffn1_pre_g[l]), wgu, wd, row(ffn1_post_g[l]), jobs)

        mu = jnp.concatenate([jnp.pad(shift_mu[l], (0, SEG_P - N_SHIFT)), extra_mu]).reshape(1, -1)
        proj, gates, wgu, wd = _norm_matmul(h, row(mix_pre_g[l]), w_proj,
                                            ffn_jobs(ffn2_w_gu, ffn2_w_down, l))

        vres = None
        if l > 0:
            vres = (v_first, row(vres_v0[l - 1]), _pad_rows(vres_w2[l - 1], 0, LANES).astype(BF))
        r, k, v, lw, kkr, a, g = _rwkv_prep(
            proj, mu, row(decay_w0[l]), _pad_rows(decay_w2[l], 0, LANES).astype(BF),
            row(iclr_a0[l]), _pad_rows(iclr_a2[l], DECAY_LORA, LANES).astype(BF),
            _pad_rows(gate_g2[l], 0, 2 * LANES).astype(BF), row(k_k[l]), row(k_a[l]), vres, seq=T)
        if l == 0:
            v_first = v
        y_rwkv = _wkv(r, k, v, lw, kkr, a, g, row(r_k[l]), row(lnx_w[l]), row(lnx_b[l]),
                      batch=B, seq=T)

        y_attn = _attention(proj, slopes, batch=B, seq=T)
        y_sgu = _sgu(proj, row(sgu_ln_g[l]), row(sgu_ln_b[l]), sgu_w_s[l], sgu_b[l].T)

        h = _merge(h, y_rwkv, y_attn, y_sgu, gates, wr, wa, ws, wo, row(mix_post_g[l]))

        jobs = ffn_jobs(ffn1_w_gu, ffn1_w_down, l + 1, fine=True) if l + 1 < depth else []
        h, *cast = _ffn(h, row(ffn2_pre_g[l]), wgu, wd, row(ffn2_post_g[l]), jobs)
        if cast:
            wgu, wd = cast
    return h.reshape(B, T, D)
```

```python
import functools
import math
from typing import Callable, NamedTuple, Optional

import numpy as np
import jax
import jax.numpy as jnp
from jax import lax
from jax.experimental import pallas as pl
from jax.experimental.pallas import tpu as pltpu

BF = jnp.bfloat16
F32 = jnp.float32

D_MODEL = 2048
RWKV_HEAD_DIM = 64
RWKV_WIDTH = D_MODEL // 2
DECAY_LORA = 64
ICLR_LORA = 64
VRES_LORA = 32
GATE_LORA = 160
LNX_EPS = 64e-5
ATTN_GROUPS = ((128, 1), (512, 4), (2048, 16))
ATTN_HEADS_PER_GROUP = 4
ATTN_HEAD_DIM = 128
ATTN_HEADS = len(ATTN_GROUPS) * ATTN_HEADS_PER_GROUP
ATTN_BLK = 128
ATTN_BLOCKS_PER_ITER = 8
SGU_CHUNK = 128
SGU_GROUPS = 4
SGU_WIDTH = D_MODEL // 4
RMS_EPS = 1e-6
LN_EPS = 1e-5
N_BRANCHES = 3

LANES = 128
SUBLANES = 8
WKV_CHUNK = 64
FFN_TF = 512
NEG_BIG = -1e30

N_SHIFT = 3 * RWKV_WIDTH + DECAY_LORA + ICLR_LORA + GATE_LORA
SEG_R, SEG_K, SEG_V = 0, RWKV_WIDTH, 2 * RWKV_WIDTH
SEG_WA = 3 * RWKV_WIDTH
SEG_G = SEG_WA + 128
SEG_P = SEG_G + 256
N_QKV = 3 * ATTN_HEADS * ATTN_HEAD_DIM
N_SGU = 2 * SGU_WIDTH
N_GATE = N_BRANCHES * D_MODEL
COL_SHIFT = 0
COL_QKV = SEG_P + LANES
COL_SGU = COL_QKV + N_QKV
COL_GATE = COL_SGU + N_SGU
N_PROJ = COL_GATE + N_GATE

VMEM_LIMIT = 60 * 1024 * 1024


def _cparams(*sem):
    return pltpu.CompilerParams(dimension_semantics=sem, vmem_limit_bytes=VMEM_LIMIT)


def _dot(a, b):
    return jnp.dot(a, b, preferred_element_type=F32)


def _dot_nt(a, b):
    return lax.dot_general(a, b, (((1,), (1,)), ((), ())), preferred_element_type=F32)


def _rms(x, g):
    return x * lax.rsqrt(jnp.mean(x * x, axis=-1, keepdims=True) + RMS_EPS) * g


def _sigmoid(x):
    return 0.5 * jnp.tanh(0.5 * x) + 0.5


class _CastJob(NamedTuple):
    src: jax.Array
    layer: int
    rows: int
    n_blocks: int
    offset: Callable
    patch: Optional[jax.Array] = None
    patch_block: int = 0
    patch_row: int = 0


class _CastMeta(NamedTuple):
    n_blocks: int
    has_patch: bool
    patch_block: int
    patch_row: int


def _plain_job(src, layer, rows):
    R = src.shape[1]
    assert R % rows == 0 and rows % (2 * SUBLANES) == 0
    return _CastJob(src, layer, rows, R // rows, lambda s: s * (rows // SUBLANES))


def _job_plumbing(jobs, step_of, n_steps):
    in_specs, args, out_specs, out_shapes, metas = [], [], [], [], []
    for job in jobs:
        assert job.n_blocks <= n_steps, "carrier kernel has too few grid steps for this cast job"
        C = job.src.shape[2]
        blk = lambda *g, job=job: jnp.minimum(step_of(*g), job.n_blocks - 1)
        in_specs.append(pl.BlockSpec(
            (None, pl.Element(job.rows), pl.Element(C)),
            lambda *g, job=job, blk=blk: (job.layer, job.offset(blk(*g)) * SUBLANES, 0)))
        args.append(job.src)
        if job.patch is not None:
            in_specs.append(pl.BlockSpec(job.patch.shape, lambda *g: (0, 0)))
            args.append(job.patch)
        out_specs.append(pl.BlockSpec((job.rows, C), lambda *g, blk=blk: (blk(*g), 0)))
        out_shapes.append(jax.ShapeDtypeStruct((job.n_blocks * job.rows, C), BF))
        metas.append(_CastMeta(job.n_blocks, job.patch is not None, job.patch_block, job.patch_row))
    return in_specs, args, out_specs, out_shapes, tuple(metas)


def _run_cast_jobs(step, metas, in_refs, out_refs):
    in_refs = iter(in_refs)
    for meta, dst in zip(metas, out_refs):
        src = next(in_refs)
        patch = next(in_refs) if meta.has_patch else None

        @pl.when(step < meta.n_blocks)
        def _(meta=meta, src=src, patch=patch, dst=dst):
            dst[...] = src[...].astype(BF)
            if patch is not None:
                @pl.when(step == meta.patch_block)
                def _():
                    dst[pl.ds(meta.patch_row, LANES), :] = patch[...].astype(BF)


def _n_job_inputs(metas):
    return sum(2 if m.has_patch else 1 for m in metas)


def _cast_kernel(*refs, metas):
    n_in = _n_job_inputs(metas)
    _run_cast_jobs(pl.program_id(0), metas, refs[:n_in], refs[n_in:])


def _cast(job):
    in_specs, args, out_specs, out_shapes, metas = _job_plumbing([job], lambda s: s, job.n_blocks)
    return pl.pallas_call(
        functools.partial(_cast_kernel, metas=metas),
        grid=(job.n_blocks,),
        in_specs=in_specs,
        out_specs=out_specs,
        out_shape=out_shapes,
        compiler_params=_cparams("arbitrary"),
        name="cast",
    )(*args)[0]


def _ffn_kernel(*refs, tail, metas, emit_norm):
    n_in = _n_job_inputs(metas)
    n_main = 7 if emit_norm else 6
    h_ref, pre_g_ref, wg_ref, wu_ref, wd_ref, post_g_ref = refs[:6]
    job_in = refs[n_main:n_main + n_in]
    outs = refs[n_main + n_in:]
    o_ref = outs[0]
    n_out = 2 if emit_norm else 1
    job_out = outs[n_out:n_out + len(metas)]
    xn_scr, acc_scr = outs[n_out + len(metas):]
    i, f = pl.program_id(0), pl.program_id(1)
    last = pl.num_programs(1) - 1
    tf = wd_ref.shape[0]

    @pl.when(f == 0)
    def _():
        xn_scr[...] = _rms(h_ref[...], pre_g_ref[...]).astype(BF)
        acc_scr[...] = jnp.zeros_like(acc_scr)

    def accumulate(lo):
        xn = xn_scr[...]
        g = _dot(xn, wg_ref[:, lo:])
        u = _dot(xn, wu_ref[:, lo:])
        a = (g * _sigmoid(g) * u).astype(BF)
        acc_scr[...] += _dot(a, wd_ref[lo:, :])

    @pl.when(f < last)
    def _():
        accumulate(0)

    @pl.when(f == last)
    def _():
        accumulate(tf - tail)
        out = h_ref[...] + 0.5 * _rms(acc_scr[...], post_g_ref[...])
        o_ref[...] = out
        if emit_norm:
            outs[1][...] = _rms(out, refs[6][...]).astype(BF)

    _run_cast_jobs(i * pl.num_programs(1) + f, metas, job_in, job_out)


def _ffn(h, pre_g, wgu, wd, post_g, jobs=(), *, next_g=None, tm=512, tf=FFN_TF):
    M, D = h.shape
    F = wd.shape[0]
    nf = -(-F // tf)
    tail = F - (nf - 1) * tf
    assert wgu.shape == (D, 2 * F) and F % LANES == 0 and tf % LANES == 0 and tail % LANES == 0
    start = lambda f, base=0: (base // LANES + jnp.minimum(f * (tf // LANES), (F - tf) // LANES)) * LANES
    j_in, j_args, j_out, j_shapes, metas = _job_plumbing(jobs, lambda i, f: i * nf + f, M // tm * nf)
    emit_norm = next_g is not None
    row_spec = pl.BlockSpec((1, D), lambda i, f: (0, 0))
    tile_spec = pl.BlockSpec((tm, D), lambda i, f: (i, 0))
    return pl.pallas_call(
        functools.partial(_ffn_kernel, tail=tail, metas=metas, emit_norm=emit_norm),
        grid=(M // tm, nf),
        in_specs=[
            pl.BlockSpec((tm, D), lambda i, f: (i, 0)),
            pl.BlockSpec((1, D), lambda i, f: (0, 0)),
            pl.BlockSpec((pl.Element(D), pl.Element(tf)), lambda i, f: (0, start(f))),
            pl.BlockSpec((pl.Element(D), pl.Element(tf)), lambda i, f: (0, start(f, F))),
            pl.BlockSpec((pl.Element(tf), pl.Element(D)), lambda i, f: (start(f), 0)),
            pl.BlockSpec((1, D), lambda i, f: (0, 0)),
        ] + [row_spec] * emit_norm + j_in,
        out_specs=[tile_spec] + [tile_spec] * emit_norm + j_out,
        out_shape=([jax.ShapeDtypeStruct((M, D), F32)] + [jax.ShapeDtypeStruct((M, D), BF)] * emit_norm
                   + j_shapes),
        scratch_shapes=[pltpu.VMEM((tm, D), BF), pltpu.VMEM((tm, D), F32)],
        compiler_params=_cparams("arbitrary", "arbitrary"),
        name="ffn",
    )(h, pre_g, wgu, wgu, wd, post_g, *([next_g] * emit_norm), *j_args)


def _gelu_tanh(x):
    c = math.sqrt(2.0 / math.pi)
    return x * (0.5 * (1.0 + jnp.tanh(c * (x + 0.044715 * (x * x * x)))))


def _nmm_kernel(*refs, n_plain, metas):
    n_in = _n_job_inputs(metas)
    xn_ref, w_ref = refs[:2]
    job_in = refs[2:2 + n_in]
    o_ref, gate_ref = refs[2 + n_in:4 + n_in]
    job_out = refs[4 + n_in:4 + n_in + len(metas)]
    i, j = pl.program_id(0), pl.program_id(1)

    @pl.when(j < n_plain)
    def _():
        o_ref[...] = _dot_nt(xn_ref[...], w_ref[...])

    @pl.when(j >= n_plain)
    def _():
        gate_ref[...] = _sigmoid(_dot_nt(xn_ref[...], w_ref[...])).astype(gate_ref.dtype)

    _run_cast_jobs(i * pl.num_programs(1) + j, metas, job_in, job_out)


def _norm_matmul(xn, w_t, jobs=(), *, tm=1024, tn=1536):
    M, D = xn.shape
    N = w_t.shape[0]
    nj = N // tn
    n_plain = COL_GATE // tn
    assert N == N_PROJ and nj * tn == N and n_plain * tn == COL_GATE
    j_in, j_args, j_out, j_shapes, metas = _job_plumbing(jobs, lambda i, j: i * nj + j, M // tm * nj)
    return pl.pallas_call(
        functools.partial(_nmm_kernel, n_plain=n_plain, metas=metas),
        grid=(M // tm, nj),
        in_specs=[
            pl.BlockSpec((tm, D), lambda i, j: (i, 0)),
            pl.BlockSpec((tn, D), lambda i, j: (j, 0)),
        ] + j_in,
        out_specs=[pl.BlockSpec((tm, tn), lambda i, j: (i, jnp.minimum(j, n_plain - 1))),
                   pl.BlockSpec((tm, tn), lambda i, j: (i, jnp.maximum(j - n_plain, 0)))] + j_out,
        out_shape=[jax.ShapeDtypeStruct((M, COL_GATE), F32),
                   jax.ShapeDtypeStruct((M, N - COL_GATE), BF)] + j_shapes,
        compiler_params=_cparams("arbitrary", "arbitrary"),
        name="in_proj",
    )(xn, w_t, *j_args)


def _prep_kernel(*refs, seq, has_vres):
    if has_vres:
        (ps_ref, prev_ref, mu_ref, w0_ref, w2_ref, a0_ref, a2_ref, g2_ref, kk_ref, ka_ref,
         vf_ref, v0_ref, vw2_ref,
         r_o, k_o, v_o, lw_o, kk_o, a_o, g_o) = refs
    else:
        (ps_ref, prev_ref, mu_ref, w0_ref, w2_ref, a0_ref, a2_ref, g2_ref, kk_ref, ka_ref,
         r_o, k_o, v_o, lw_o, kk_o, a_o, g_o) = refs
    tm = ps_ref.shape[0]
    W = RWKV_WIDTH
    x = ps_ref[...]
    at_seq_start = (pl.program_id(0) * tm) % seq == 0
    last_prev = jnp.where(at_seq_start, 0.0, prev_ref[7:8, :])
    row = lax.broadcasted_iota(jnp.int32, x.shape, 0)
    prev = jnp.where(row == 0, last_prev, pltpu.roll(x, 1, axis=0))
    xs = x + (prev - x) * mu_ref[...]

    xr = xs[:, SEG_R:SEG_R + W]
    xk = xs[:, SEG_K:SEG_K + W]
    xv = xs[:, SEG_V:SEG_V + W]
    xwa = xs[:, SEG_WA:SEG_WA + 128]
    xg = xs[:, SEG_G:SEG_G + 256]

    z = w0_ref[...] + _dot(jnp.tanh(xwa).astype(BF), w2_ref[...])
    w_log = -jax.nn.softplus(-z) - 0.5
    lw_o[...] = -jnp.exp(w_log)
    a = jax.nn.sigmoid(a0_ref[...] + _dot(xwa.astype(BF), a2_ref[...]))
    a_o[...] = a
    g_o[...] = _dot(jax.nn.sigmoid(xg).astype(BF), g2_ref[...]).astype(g_o.dtype)
    kk_o[...] = xk * kk_ref[...]
    k_o[...] = xk * (1.0 + (a - 1.0) * ka_ref[...])
    r_o[...] = xr
    if has_vres:
        pv = xs[:, SEG_P:SEG_P + 128]
        mix = jax.nn.sigmoid(v0_ref[...] + _dot(pv.astype(BF), vw2_ref[...]))
        v_o[...] = xv + (vf_ref[...] - xv) * mix
    else:
        v_o[...] = xv


def _rwkv_prep(ps, mu, w0, w2, a0, a2, g2, k_k, k_a, vres, *, seq, tm=256):
    M = ps.shape[0]
    NA = COL_QKV
    W = RWKV_WIDTH
    row = lambda n: pl.BlockSpec((1, n), lambda i: (0, 0))
    full = lambda a: pl.BlockSpec(a.shape, lambda i: (0, 0))
    tok = pl.BlockSpec((tm, W), lambda i: (i, 0))
    in_specs = [
        pl.BlockSpec((tm, NA), lambda i: (i, 0)),
        pl.BlockSpec((8, NA), lambda i: (jnp.maximum(i * (tm // 8) - 1, 0), 0)),
        row(NA), row(W), full(w2), row(W), full(a2), full(g2), row(W), row(W),
    ]
    args = [ps, ps, mu, w0, w2, a0, a2, g2, k_k, k_a]
    if vres is not None:
        v_first, v0, vw2 = vres
        in_specs += [tok, row(W), full(vw2)]
        args += [v_first, v0, vw2]
    return pl.pallas_call(
        functools.partial(_prep_kernel, seq=seq, has_vres=vres is not None),
        grid=(M // tm,),
        in_specs=in_specs,
        out_specs=[tok] * 7,
        out_shape=[jax.ShapeDtypeStruct((M, W), F32)] * 6 + [jax.ShapeDtypeStruct((M, W), BF)],
        compiler_params=_cparams("parallel"),
        name="rwkv_prep",
    )(*args)


def _wkv_kernel(r_ref, k_ref, v_ref, lw_ref, kk_ref, a_ref, g_ref, rk_ref, lnw_ref, lnb_ref,
                y_ref, p_scr):
    C = WKV_CHUNK
    C2 = 2 * C

    @pl.when(pl.program_id(1) == 0)
    def _():
        p_scr[...] = jnp.zeros_like(p_scr)

    lane = lax.broadcasted_iota(jnp.int32, (C, LANES), 1)
    lo = lane < RWKV_HEAD_DIM
    ti = lax.broadcasted_iota(jnp.int32, (C, C), 0)
    tj = lax.broadcasted_iota(jnp.int32, (C, C), 1)
    tri = jnp.where(ti >= tj, 1.0, 0.0).astype(BF)
    tq = lax.broadcasted_iota(jnp.int32, (C, LANES), 0)
    sq = lane % C
    strict = tq > sq
    incl = tq >= sq
    eye_c = jnp.where(tq == sq, 1.0, 0.0)
    si = lax.broadcasted_iota(jnp.int32, (C2, LANES), 0)
    sj = lax.broadcasted_iota(jnp.int32, (C2, LANES), 1)
    eye = si == sj

    def seg_sum(x):
        s_lo = jnp.sum(jnp.where(lo, x, 0.0), axis=1, keepdims=True)
        s_hi = jnp.sum(jnp.where(lo, 0.0, x), axis=1, keepdims=True)
        return jnp.where(lo, s_lo, s_hi)

    def stack(x):
        return jnp.concatenate([jnp.where(lo, x, 0.0), jnp.where(lo, 0.0, x)], axis=0)

    def bstack(x):
        return stack(x).astype(BF)

    n_pairs = r_ref.shape[1] // LANES
    n_chunks = r_ref.shape[0] // C
    items = [(c, p) for c in range(n_chunks) for p in range(n_pairs)]
    rows = [slice(c * C, (c + 1) * C) for c, _ in items]
    cols = [slice(p * LANES, (p + 1) * LANES) for _, p in items]
    each = lambda f, *xs: [f(*t) for t in zip(*xs)]
    bf = lambda xs: [x.astype(BF) for x in xs]
    load = lambda ref: [ref[rs, cs] for rs, cs in zip(rows, cols)]

    r, k, v, lw, kkr, a = (load(ref) for ref in (r_ref, k_ref, v_ref, lw_ref, kk_ref, a_ref))

    def cumsum(x):
        hi = x.astype(BF)
        both = _dot(tri, jnp.concatenate([hi, (x - hi.astype(F32)).astype(BF)], axis=1))
        return both[:, :LANES] + both[:, LANES:]

    cum = each(cumsum, lw)
    tot = [c[C - 1:C, :] for c in cum]
    kk = each(lambda x: x / jnp.maximum(jnp.sqrt(seg_sum(x * x)), 1e-12), kkr)
    kka = each(lambda x, y: x * y, kk, a)
    A_c = each(lambda x, c, w: -x * jnp.exp(c - w), kk, cum, lw)
    R_c = bf(each(lambda x, c: x * jnp.exp(c), r, cum))
    AR = each(lambda a_, r_: jnp.concatenate([a_.astype(BF), r_], axis=0), A_c, R_c)
    BK_st = each(lambda x, y, c: jnp.concatenate([bstack(x * jnp.exp(-c)), bstack(y * jnp.exp(-c))], axis=0),
                 kka, k, cum)
    V_st = each(bstack, v)
    BKpT = bf(each(lambda x, y, c, t: jnp.concatenate([stack(x * jnp.exp(t - c)).T,
                                                       stack(y * jnp.exp(t - c)).T], axis=1),
                   kka, k, cum, tot))
    gcol = [jnp.exp(jnp.sum(jnp.where(eye, jnp.broadcast_to(t, (C2, LANES)), 0.0), axis=1, keepdims=True))
            for t in tot]

    G = each(_dot_nt, AR, BK_st)
    Aab = [jnp.where(strict, x[:C, :LANES], 0.0) for x in G]
    Aak = bf([jnp.where(strict, x[:C, LANES:], 0.0) for x in G])
    Mrb = bf([jnp.where(incl, x[C:, :LANES], 0.0) for x in G])
    Mrk = bf([jnp.where(incl, x[C:, LANES:], 0.0) for x in G])
    AkV = each(_dot, Aak, V_st)

    S = [eye_c + x for x in Aab]
    Pw = each(lambda x: _dot(x.astype(BF), bstack(x)), Aab)
    n = 2
    while 2 * n < C:
        prod = each(lambda p_, s_: _dot(p_.astype(BF), jnp.concatenate([bstack(p_), bstack(s_)], axis=1)),
                    Pw, S)
        S = each(lambda s_, pr: s_ + pr[:, LANES:], S, prod)
        Pw = [pr[:, :LANES] for pr in prod]
        n *= 2
    S = each(lambda s_, p_: s_ + _dot(p_.astype(BF), bstack(s_)), S, Pw)
    WU = each(lambda s_, a_, u0: _dot(s_.astype(BF), jnp.concatenate([bstack(a_), bstack(u0)], axis=1)),
              S, A_c, AkV)

    def finish(y_, r_, k_, v_, rs, cs):
        mu = seg_sum(y_) * (1.0 / RWKV_HEAD_DIM)
        yc = y_ - mu
        var = seg_sum(yc * yc) * (1.0 / RWKV_HEAD_DIM)
        yn = yc * lax.rsqrt(var + LNX_EPS) * lnw_ref[:, cs] + lnb_ref[:, cs]
        bonus = seg_sum(r_ * k_ * rk_ref[:, cs]) * v_
        return ((yn + bonus) * g_ref[rs, cs]).astype(y_ref.dtype)

    P = [p_scr[p] for p in range(n_pairs)]
    for c in range(n_chunks):
        sel = lambda xs: xs[c * n_pairs:(c + 1) * n_pairs]
        Pb = bf(P)
        U = each(lambda wu, pb: _dot(wu[:, :LANES].astype(BF), pb) + wu[:, LANES:], sel(WU), Pb)
        U_st = each(bstack, U)
        y = each(lambda r_, mb, mk, pb, u, vs: _dot(jnp.concatenate([r_, mb, mk], axis=1),
                                                    jnp.concatenate([pb, u, vs], axis=0)),
                 sel(R_c), sel(Mrb), sel(Mrk), Pb, U_st, sel(V_st))
        P = each(lambda gc, p_, m, u, vs: gc * p_ + _dot(m, jnp.concatenate([u, vs], axis=0)),
                 sel(gcol), P, sel(BKpT), U_st, sel(V_st))
        out = each(finish, y, sel(r), sel(k), sel(v), sel(rows), sel(cols))
        for o, rs, cs in zip(out, sel(rows), sel(cols)):
            y_ref[rs, cs] = o
    for p in range(n_pairs):
        p_scr[p] = P[p]


def _wkv(r, k, v, lw, kkr, a, g, r_k, lnx_w, lnx_b, *, batch, seq, chunks_per_step=4):
    M, W = r.shape
    rows = chunks_per_step * WKV_CHUNK
    steps = seq // rows
    tok = pl.BlockSpec((rows, W), lambda b, c: (b * steps + c, 0))
    par = pl.BlockSpec((1, W), lambda b, c: (0, 0))
    return pl.pallas_call(
        _wkv_kernel,
        grid=(batch, steps),
        in_specs=[tok] * 7 + [par] * 3,
        out_specs=tok,
        out_shape=jax.ShapeDtypeStruct((M, W), BF),
        scratch_shapes=[pltpu.VMEM((W // LANES, LANES, LANES), F32)],
        compiler_params=_cparams("parallel", "arbitrary"),
        name="wkv",
    )(r, k, v, lw, kkr, a, g, r_k, lnx_w, lnx_b)


def _attn_kernel(slopes_ref, q_ref, k_ref, v_ref, o_ref, og_scr, lse_scr, *, seq):
    slot = pl.program_id(1)
    grp = pl.program_id(2)
    blk = ATTN_BLK
    scale = ATTN_HEAD_DIM ** -0.5
    qi = lax.broadcasted_iota(jnp.int32, (blk, 2 * blk), 0)
    kj = lax.broadcasted_iota(jnp.int32, (blk, 2 * blk), 1)
    rel = (qi + blk - kj).astype(F32)
    in_band = (kj >= qi) & (kj <= qi + blk)
    is_prev = kj < blk
    nb = ATTN_BLOCKS_PER_ITER
    each = lambda f, *xs: [f(*t) for t in zip(*xs)]

    def rows(start, d):
        if d == 1:
            return pl.ds(start, blk)
        return pl.ds(start, blk, stride=d)

    for gi, (window, d) in enumerate(ATTN_GROUPS):
        assert window // d == blk and seq % window == 0 and (seq // blk) % nb == 0

        @pl.when(grp == gi)
        def _(gi=gi, d=d):
            slope = slopes_ref[gi * ATTN_HEADS_PER_GROUP + slot]
            bias = jnp.where(in_band, -slope * float(d) * rel, NEG_BIG)
            span = blk * d

            def body(it, carry):
                ids = [it * nb + j for j in range(nb)]
                ns = [i // d for i in ids]
                cur = [rows(n * span + i % d, d) for n, i in zip(ns, ids)]
                prv = [rows(jnp.maximum(n - 1, 0) * span + i % d, d) for n, i in zip(ns, ids)]
                q = [q_ref[c, :].astype(BF) for c in cur]
                kcat = [jnp.concatenate([k_ref[p_, :], k_ref[c, :]], axis=0).astype(BF) for p_, c in zip(prv, cur)]
                vcat = [jnp.concatenate([v_ref[p_, :], v_ref[c, :]], axis=0).astype(BF) for p_, c in zip(prv, cur)]
                s = each(lambda q_, k_: _dot_nt(q_, k_) * scale + bias, q, kcat)
                s = each(lambda s_, n: jnp.where(is_prev & (n == 0), NEG_BIG, s_), s, ns)
                m = [jnp.max(s_, axis=1, keepdims=True) for s_ in s]
                p = each(lambda s_, m_: jnp.exp(s_ - m_), s, m)
                l = [jnp.sum(p_, axis=1, keepdims=True) for p_ in p]
                o = each(lambda p_, v_, l_: _dot(p_.astype(BF), v_) / l_, p, vcat, l)
                lse = each(lambda m_, l_: m_ + jnp.log(l_), m, l)
                for c, o_, e_ in zip(cur, o, lse):
                    og_scr[gi, c, :] = o_
                    lse_scr[gi, c, :] = jnp.broadcast_to(e_, (blk, LANES))
                return carry

            lax.fori_loop(0, seq // blk // nb, body, 0)

    @pl.when(grp == len(ATTN_GROUPS) - 1)
    def _():
        tile = 8 * blk

        def merge(t, carry):
            rs = pl.ds(pl.multiple_of(t * tile, tile), tile)
            lses = [lse_scr[g, rs, :] for g in range(len(ATTN_GROUPS))]
            top = functools.reduce(jnp.maximum, lses)
            w = [jnp.exp(e - top) for e in lses]
            num = sum(w_ * og_scr[g, rs, :] for g, w_ in enumerate(w))
            o_ref[rs, :] = (num / sum(w)).astype(o_ref.dtype)
            return carry

        lax.fori_loop(0, seq // tile, merge, 0)


def _attention(qkv, slopes, *, batch, seq):
    M = qkv.shape[0]
    E = ATTN_HEAD_DIM
    hpg = ATTN_HEADS_PER_GROUP

    def spec(which):
        first = COL_QKV // E + which * ATTN_HEADS
        return pl.BlockSpec((seq, E), lambda b, s, g: (b, first + g * hpg + s))

    return pl.pallas_call(
        functools.partial(_attn_kernel, seq=seq),
        grid=(batch, hpg, len(ATTN_GROUPS)),
        in_specs=[pl.BlockSpec(memory_space=pltpu.SMEM), spec(0), spec(1), spec(2)],
        out_specs=pl.BlockSpec((seq, E), lambda b, s, g: (b, s)),
        out_shape=jax.ShapeDtypeStruct((M, hpg * E), BF),
        scratch_shapes=[pltpu.VMEM((len(ATTN_GROUPS), seq, LANES), F32)] * 2,
        compiler_params=_cparams("parallel", "parallel", "arbitrary"),
        name="dilated_attn",
    )(slopes, qkv, qkv, qkv)


def _sgu_kernel(x_ref, lg_ref, lb_ref, ws_ref, bt_ref, o_ref):
    tm = x_ref.shape[0]
    CH = SGU_CHUNK
    GD = SGU_WIDTH // SGU_GROUPS
    u = _gelu_tanh(x_ref[:, :SGU_WIDTH])
    v = _gelu_tanh(x_ref[:, SGU_WIDTH:])
    mu = jnp.mean(v, axis=-1, keepdims=True)
    vc = v - mu
    var = jnp.mean(vc * vc, axis=-1, keepdims=True)
    vn = (vc * lax.rsqrt(var + LN_EPS) * lg_ref[...] + lb_ref[...]).astype(BF)
    ti = lax.broadcasted_iota(jnp.int32, (CH, CH), 0)
    si = lax.broadcasted_iota(jnp.int32, (CH, CH), 1)
    causal = ti >= si
    for gidx in range(SGU_GROUPS):
        w = jnp.where(causal, ws_ref[gidx], 0.0).astype(BF)
        bias = bt_ref[:, gidx:gidx + 1]
        cs = slice(gidx * GD, (gidx + 1) * GD)
        for c in range(tm // CH):
            rs = slice(c * CH, (c + 1) * CH)
            mixed = _dot(w, vn[rs, cs]) + bias
            o_ref[rs, cs] = (u[rs, cs] * mixed).astype(o_ref.dtype)


def _sgu(proj, ln_g, ln_b, w_s, b_t, *, tm=512):
    M = proj.shape[0]
    return pl.pallas_call(
        _sgu_kernel,
        grid=(M // tm,),
        in_specs=[
            pl.BlockSpec((tm, N_SGU), lambda i: (i, COL_SGU // N_SGU)),
            pl.BlockSpec((1, SGU_WIDTH), lambda i: (0, 0)),
            pl.BlockSpec((1, SGU_WIDTH), lambda i: (0, 0)),
            pl.BlockSpec(w_s.shape, lambda i: (0, 0, 0)),
            pl.BlockSpec(b_t.shape, lambda i: (0, 0)),
        ],
        out_specs=pl.BlockSpec((tm, SGU_WIDTH), lambda i: (i, 0)),
        out_shape=jax.ShapeDtypeStruct((M, SGU_WIDTH), BF),
        compiler_params=_cparams("parallel"),
        name="sgu",
    )(proj, ln_g, ln_b, w_s, b_t)


def _merge_kernel(h_ref, yr_ref, ya_ref, ys_ref, g0_ref, g1_ref, g2_ref, wr_ref, wa_ref, ws_ref,
                  wo_ref, pg_ref, o_ref):
    merged = (g0_ref[...] * _dot(yr_ref[...], wr_ref[...])
              + g1_ref[...] * _dot(ya_ref[...], wa_ref[...])
              + g2_ref[...] * _dot(ys_ref[...], ws_ref[...]))
    o_ref[...] = h_ref[...] + _rms(_dot(merged.astype(BF), wo_ref[...]), pg_ref[...])


def _merge(h, yr, ya, ys, gates, wr, wa, ws, wo, post_g, *, tm=512):
    M, D = h.shape
    act = lambda a: pl.BlockSpec((tm, a.shape[1]), lambda i: (i, 0))
    gate = lambda br: pl.BlockSpec((tm, D), lambda i: (i, br))
    resident = lambda a: pl.BlockSpec(a.shape, lambda i: (0, 0), pipeline_mode=pl.Buffered(1))
    return pl.pallas_call(
        _merge_kernel,
        grid=(M // tm,),
        in_specs=[
            pl.BlockSpec((tm, D), lambda i: (i, 0)),
            act(yr), act(ya), act(ys), gate(0), gate(1), gate(2),
            resident(wr), resident(wa), resident(ws), resident(wo),
            pl.BlockSpec((1, D), lambda i: (0, 0)),
        ],
        out_specs=pl.BlockSpec((tm, D), lambda i: (i, 0)),
        out_shape=jax.ShapeDtypeStruct((M, D), F32),
        compiler_params=_cparams("parallel"),
        name="merge_out",
    )(h, yr, ya, ys, gates, gates, gates, wr, wa, ws, wo, post_g)


def _alibi_slopes(n_heads):
    def geometric(n):
        start = 2.0 ** (-8.0 / n)
        return [start ** (i + 1) for i in range(n)]
    closest = 2 ** int(math.floor(math.log2(n_heads)))
    slopes = geometric(closest)
    if closest < n_heads:
        slopes += geometric(2 * closest)[0::2][: n_heads - closest]
    return np.array(sorted(slopes, reverse=True), dtype=np.float32)


def _pad_rows(x, before, total):
    return jnp.pad(x, ((before, total - before - x.shape[0]), (0, 0)))


def _w_in_job(w_in_t, extra_t, l, *, rows=128):
    n_in = w_in_t.shape[1]
    assert n_in - N_SHIFT == N_PROJ - COL_QKV and (COL_QKV - N_SHIFT) % SUBLANES == 0
    assert COL_QKV % rows == 0 and N_PROJ % rows == 0 and SEG_P % rows + LANES <= rows
    n_head = COL_QKV // rows

    def offset(s):
        return s * (rows // SUBLANES) - jnp.where(s < n_head, 0, (COL_QKV - N_SHIFT) // SUBLANES)

    return _CastJob(w_in_t, l, rows, N_PROJ // rows, offset, extra_t, SEG_P // rows, SEG_P % rows)


def kernel(x, ffn1_pre_g, ffn1_w_gu, ffn1_w_down, ffn1_post_g, mix_pre_g, w_in, shift_mu, decay_w0, decay_w2, iclr_a0, iclr_a2, gate_g2, k_k, k_a, r_k, lnx_w, lnx_b, vres_w1, vres_mu, vres_v0, vres_w2, sgu_ln_g, sgu_ln_b, sgu_w_s, sgu_b, w_b_rwkv, w_b_attn, w_b_sgu, w_out, mix_post_g, ffn2_pre_g, ffn2_w_gu, ffn2_w_down, ffn2_post_g):
    B, T, D = x.shape
    depth = w_in.shape[0]
    slopes = jnp.asarray(_alibi_slopes(ATTN_HEADS))
    row = lambda p: p.reshape(1, -1)
    ffn_jobs = lambda w_gu, w_down, l, fine=False: [_plain_job(w_gu, l, 16 if fine else 32),
                                                    _plain_job(w_down, l, 64 if fine else 128)]

    w_in_t = jnp.swapaxes(w_in, 1, 2)
    h = x.reshape(B * T, D)
    v_first = None
    wgu = _cast(_plain_job(ffn1_w_gu, 0, 128))
    wd = _cast(_plain_job(ffn1_w_down, 0, ffn1_w_down.shape[1] // 8))
    for l in range(depth):
        if l == 0:
            extra_t = jnp.zeros((LANES, D), F32)
            extra_mu = jnp.zeros((LANES,), F32)
        else:
            extra_t = _pad_rows(vres_w1[l - 1].T, 0, LANES)
            extra_mu = jnp.pad(vres_mu[l - 1], (0, LANES - VRES_LORA))
        jobs = [_w_in_job(w_in_t, extra_t, l)] + [_plain_job(w, l, 64)
                                                    for w in (w_b_rwkv, w_b_attn, w_b_sgu, w_out)]
        h, xn, w_proj, wr, wa, ws, wo = _ffn(h, row(ffn1_pre_g[l]), wgu, wd, row(ffn1_post_g[l]), jobs,
                                             next_g=row(mix_pre_g[l]))

        mu = jnp.concatenate([jnp.pad(shift_mu[l], (0, SEG_P - N_SHIFT)), extra_mu]).reshape(1, -1)
        proj, gates, wgu, wd = _norm_matmul(xn, w_proj,
                                            ffn_jobs(ffn2_w_gu, ffn2_w_down, l))

        vres = None
        if l > 0:
            vres = (v_first, row(vres_v0[l - 1]), _pad_rows(vres_w2[l - 1], 0, LANES).astype(BF))
        r, k, v, lw, kkr, a, g = _rwkv_prep(
            proj, mu, row(decay_w0[l]), _pad_rows(decay_w2[l], 0, LANES).astype(BF),
            row(iclr_a0[l]), _pad_rows(iclr_a2[l], DECAY_LORA, LANES).astype(BF),
            _pad_rows(gate_g2[l], 0, 2 * LANES).astype(BF), row(k_k[l]), row(k_a[l]), vres, seq=T)
        if l == 0:
            v_first = v
        y_rwkv = _wkv(r, k, v, lw, kkr, a, g, row(r_k[l]), row(lnx_w[l]), row(lnx_b[l]),
                      batch=B, seq=T)

        y_attn = _attention(proj, slopes, batch=B, seq=T)
        y_sgu = _sgu(proj, row(sgu_ln_g[l]), row(sgu_ln_b[l]), sgu_w_s[l], sgu_b[l].T)

        h = _merge(h, y_rwkv, y_attn, y_sgu, gates, wr, wa, ws, wo, row(mix_post_g[l]))

        jobs = ffn_jobs(ffn1_w_gu, ffn1_w_down, l + 1, fine=True) if l + 1 < depth else []
        h, *cast = _ffn(h, row(ffn2_pre_g[l]), wgu, wd, row(ffn2_post_g[l]), jobs)
        if cast:
            wgu, wd = cast
    return h.reshape(B, T, D)
```
